```python
import math
import jax, jax.numpy as jnp
from jax import lax
import numpy as np

D_MODEL = 4096
BATCH = 4
SEQ = 2048
DEPTH = 2

PLE_DIM = 256
N_BRANCHES = 3
BRANCH_WIDTH = D_MODEL // 2
GMLP_CHUNK = 128
GMLP_GROUPS = 8
SWA_HEAD_DIM = 64
SWA_Q_HEADS = BRANCH_WIDTH // SWA_HEAD_DIM
SWA_KV_HEADS = 4
SWA_WINDOW = 128
SWA_BLOCK = 128
ROT_DIM = SWA_HEAD_DIM // 4
ROPE_THETA = 500000.0
MLSTM_HEADS = 4
MLSTM_V_DIM = BRANCH_WIDTH // MLSTM_HEADS
MLSTM_QK_DIM = MLSTM_V_DIM // 2
MLSTM_CHUNK = 64
GATE_SOFTCAP = 15.0
NORM_EPS = 1e-6
NEG_INF = -1e30

SPLIT_SIZES = (
    BRANCH_WIDTH, BRANCH_WIDTH, BRANCH_WIDTH,
    SWA_Q_HEADS * SWA_HEAD_DIM, SWA_KV_HEADS * SWA_HEAD_DIM,
    SWA_KV_HEADS * SWA_HEAD_DIM, BRANCH_WIDTH,
    MLSTM_HEADS * MLSTM_QK_DIM, MLSTM_HEADS * MLSTM_QK_DIM,
    MLSTM_HEADS * MLSTM_V_DIM, MLSTM_HEADS, MLSTM_HEADS,
    BRANCH_WIDTH, BRANCH_WIDTH,
    N_BRANCHES * D_MODEL,
)
N_IN = sum(SPLIT_SIZES)

kernel_name = 'hybrid_gmlp_swa_mlstm_block'


def rms_norm(x, g):
    xf = x.astype(jnp.float32)
    y = xf * lax.rsqrt(jnp.mean(xf * xf, axis=-1, keepdims=True) + NORM_EPS)
    return (y * g.astype(jnp.float32)).astype(x.dtype)


def layer_norm(x, g, b):
    xf = x.astype(jnp.float32)
    mu = jnp.mean(xf, axis=-1, keepdims=True)
    var = jnp.mean(jnp.square(xf - mu), axis=-1, keepdims=True)
    y = (xf - mu) * lax.rsqrt(var + NORM_EPS)
    return (y * g.astype(jnp.float32) + b.astype(jnp.float32)).astype(x.dtype)


def softcap(z):
    return GATE_SOFTCAP * jnp.tanh(z / GATE_SOFTCAP)


def apply_partial_rope(t, cos, sin):
    half = ROT_DIM // 2
    t1 = t[..., :half].astype(jnp.float32)
    t2 = t[..., half:ROT_DIM].astype(jnp.float32)
    rot = jnp.concatenate([t1 * cos - t2 * sin, t2 * cos + t1 * sin], axis=-1).astype(t.dtype)
    return jnp.concatenate([rot, t[..., ROT_DIM:]], axis=-1)


def gmlp_branch(u, v, ln_g, ln_b, ws, bs):
    B, S, W = v.shape
    nc = S // GMLP_CHUNK
    u = jax.nn.gelu(u, approximate=False)
    v = layer_norm(jax.nn.gelu(v, approximate=False), ln_g, ln_b)
    vg = v.reshape(B, nc, GMLP_CHUNK, GMLP_GROUPS, W // GMLP_GROUPS)
    w_causal = jnp.tril(ws)
    mixed = jnp.einsum('gts,bnsgc->bntgc', w_causal, vg) + bs.T[None, None, :, :, None]
    return u * mixed.reshape(B, S, W)


def swa_branch(q, k, v, sinks, cos, sin):
    B, S, _ = q.shape
    G = SWA_Q_HEADS // SWA_KV_HEADS
    nb = S // SWA_BLOCK
    q = apply_partial_rope(q.reshape(B, S, SWA_Q_HEADS, SWA_HEAD_DIM), cos, sin)
    k = apply_partial_rope(k.reshape(B, S, SWA_KV_HEADS, SWA_HEAD_DIM), cos, sin)
    v = v.reshape(B, S, SWA_KV_HEADS, SWA_HEAD_DIM)
    qb = q.reshape(B, nb, SWA_BLOCK, SWA_KV_HEADS, G, SWA_HEAD_DIM)

    def band(t):
        tp = jnp.pad(t, ((0, 0), (SWA_BLOCK, 0), (0, 0), (0, 0)))
        tp = tp.reshape(B, nb + 1, SWA_BLOCK, SWA_KV_HEADS, SWA_HEAD_DIM)
        return jnp.concatenate([tp[:, :-1], tp[:, 1:]], axis=2)

    kb, vb = band(k), band(v)
    s = jnp.einsum('bnqhgd,bnkhd->bnhgqk', qb, kb).astype(jnp.float32) * (SWA_HEAD_DIM ** -0.5)
    qi = jnp.arange(SWA_BLOCK)[:, None]
    kj = jnp.arange(2 * SWA_BLOCK)[None, :]
    diff = qi + SWA_BLOCK - kj
    local = (diff >= 0) & (diff < SWA_WINDOW)
    not_pad = (jnp.arange(nb)[:, None, None] > 0) | (kj >= SWA_BLOCK)[None]
    mask = local[None] & not_pad
    s = jnp.where(mask[None, :, None, None], s, NEG_INF)
    sink = sinks.astype(jnp.float32).reshape(SWA_KV_HEADS, G)[None, None, :, :, None, None]
    sink = jnp.broadcast_to(sink, s.shape[:-1] + (1,))
    probs = jax.nn.softmax(jnp.concatenate([s, sink], axis=-1), axis=-1)[..., :-1]
    o = jnp.einsum('bnhgqk,bnkhd->bnqhgd', probs.astype(v.dtype), vb)
    return o.reshape(B, S, SWA_Q_HEADS * SWA_HEAD_DIM)


def mlstm_chunkwise(q, k, v, ig, lf):
    B, S, H, dk = q.shape
    dv = v.shape[-1]
    L = MLSTM_CHUNK
    nc = S // L

    def to_chunks(t):
        if t.ndim == 4:
            return t.reshape(B, nc, L, H, t.shape[-1]).transpose(1, 0, 3, 2, 4)
        return t.reshape(B, nc, L, H).transpose(1, 0, 3, 2)

    causal = jnp.tril(jnp.ones((L, L), dtype=bool))

    def step(carry, xs):
        C, n, m = carry
        qc, kc, vc, ic, fc = xs
        b = jnp.cumsum(fc, axis=-1)
        log_d = b[..., :, None] - b[..., None, :] + ic[..., None, :]
        log_d = jnp.where(causal, log_d, NEG_INF)
        m_inter = b + m[..., None]
        m_t = jnp.maximum(m_inter, jnp.max(log_d, axis=-1))
        s = jnp.einsum('bhqd,bhkd->bhqk', qc, kc) * jnp.exp(log_d - m_t[..., None])
        a = jnp.exp(m_inter - m_t)
        num = jnp.einsum('bhqk,bhkv->bhqv', s, vc) + a[..., None] * jnp.einsum('bhqd,bhdv->bhqv', qc, C)
        den = jnp.sum(s, axis=-1) + a * jnp.einsum('bhqd,bhd->bhq', qc, n)
        h = num / jnp.maximum(jnp.abs(den), jnp.exp(-m_t))[..., None]
        g = b[..., -1]
        w_log = g[..., None] - b + ic
        m_new = jnp.maximum(g + m, jnp.max(w_log, axis=-1))
        w = jnp.exp(w_log - m_new[..., None])
        decay = jnp.exp(g + m - m_new)
        C = decay[..., None, None] * C + jnp.einsum('bhl,bhld,bhlv->bhdv', w, kc, vc)
        n = decay[..., None] * n + jnp.einsum('bhl,bhld->bhd', w, kc)
        return (C, n, m_new), h

    init = (jnp.zeros((B, H, dk, dv), jnp.float32), jnp.zeros((B, H, dk), jnp.float32),
            jnp.zeros((B, H), jnp.float32))
    _, hs = lax.scan(step, init, (to_chunks(q), to_chunks(k), to_chunks(v), to_chunks(ig), to_chunks(lf)))
    return hs.transpose(1, 0, 3, 2, 4).reshape(B, S, H, dv)


def mlstm_branch(q, k, v, i_pre, f_pre, o_pre, ib, fb, norm_g):
    B, S, _ = q.shape
    qf = q.reshape(B, S, MLSTM_HEADS, MLSTM_QK_DIM).astype(jnp.float32) * (MLSTM_QK_DIM ** -0.5)
    kf = k.reshape(B, S, MLSTM_HEADS, MLSTM_QK_DIM).astype(jnp.float32)
    vf = v.reshape(B, S, MLSTM_HEADS, MLSTM_V_DIM).astype(jnp.float32)
    ig = softcap(i_pre.astype(jnp.float32) + ib.astype(jnp.float32))
    lf = jax.nn.log_sigmoid(softcap(f_pre.astype(jnp.float32) + fb.astype(jnp.float32)))
    h = mlstm_chunkwise(qf, kf, vf, ig, lf)
    h = h * lax.rsqrt(jnp.mean(h * h, axis=-1, keepdims=True) + NORM_EPS)
    h = h.reshape(B, S, MLSTM_HEADS * MLSTM_V_DIM) * norm_g.astype(jnp.float32)
    return h.astype(o_pre.dtype) * jax.nn.sigmoid(o_pre)


def split_columns(proj):
    idx, acc = [], 0
    for sz in SPLIT_SIZES[:-1]:
        acc += sz
        idx.append(acc)
    return jnp.split(proj, idx, axis=-1)


def hybrid_layer(x, p_l, cos, sin, norm_pre, w_in, gmlp_ln_g, gmlp_ln_b, gmlp_ws, gmlp_bs,
                 attn_sinks, mlstm_ib, mlstm_fb, mlstm_norm_g, w_branch, w_out, norm_post,
                 ple_proj, ple_norm, ple_gate):
    B, S, D = x.shape
    h = rms_norm(x, norm_pre)
    proj = jnp.einsum('bsd,dn->bsn', h, w_in)
    (a_u, a_v, a_z, b_q, b_k, b_v, b_z,
     c_q, c_k, c_v, c_i, c_f, c_o, c_z, gates) = split_columns(proj)
    y_a = gmlp_branch(a_u, a_v, gmlp_ln_g, gmlp_ln_b, gmlp_ws, gmlp_bs) * jax.nn.silu(a_z)
    y_b = swa_branch(b_q, b_k, b_v, attn_sinks, cos, sin) * jax.nn.silu(b_z)
    y_c = mlstm_branch(c_q, c_k, c_v, c_i, c_f, c_o, mlstm_ib, mlstm_fb, mlstm_norm_g) * jax.nn.silu(c_z)
    ys = jnp.stack([y_a, y_b, y_c], axis=2)
    br = jnp.einsum('bsjc,jcd->bsjd', ys, w_branch)
    g = jax.nn.sigmoid(gates.reshape(B, S, N_BRANCHES, D))
    mixed = jnp.sum(g * br, axis=2)
    x = x + rms_norm(jnp.einsum('bsd,de->bse', mixed, w_out), norm_post)
    e = rms_norm(jnp.einsum('bsp,pd->bsd', p_l, ple_proj), ple_norm)
    x = x + jax.nn.sigmoid(jnp.einsum('bsd,de->bse', x, ple_gate)) * e
    return x


def setup_inputs(seed: int = 0) -> dict:
    key = jax.random.key(seed)
    ks = jax.random.split(key, 20)

    def nrm(k, shape, scale):
        return jax.random.normal(k, shape, jnp.float32) * scale

    x = nrm(ks[0], (BATCH, SEQ, D_MODEL), 1.0)
    p = nrm(ks[1], (DEPTH, BATCH, SEQ, PLE_DIM), 1.0)
    start = jax.random.randint(ks[2], (BATCH, 1), 0, 4096, dtype=jnp.int32)
    positions = start + jnp.arange(SEQ, dtype=jnp.int32)[None, :]
    norm_pre = 1.0 + nrm(ks[3], (DEPTH, D_MODEL), 0.05)
    w_in = nrm(ks[4], (DEPTH, D_MODEL, N_IN), D_MODEL ** -0.5)
    gmlp_ln_g = 1.0 + nrm(ks[5], (DEPTH, BRANCH_WIDTH), 0.05)
    gmlp_ln_b = nrm(ks[6], (DEPTH, BRANCH_WIDTH), 0.02)
    gmlp_ws = nrm(ks[7], (DEPTH, GMLP_GROUPS, GMLP_CHUNK, GMLP_CHUNK), GMLP_CHUNK ** -0.5)
    gmlp_bs = 1.0 + nrm(ks[8], (DEPTH, GMLP_GROUPS, GMLP_CHUNK), 0.05)
    attn_sinks = nrm(ks[9], (DEPTH, SWA_Q_HEADS), 0.5)
    mlstm_ib = nrm(ks[10], (DEPTH, MLSTM_HEADS), 0.1)
    mlstm_fb = jnp.linspace(3.0, 6.0, MLSTM_HEADS, dtype=jnp.float32)[None, :] + nrm(ks[11], (DEPTH, MLSTM_HEADS), 0.1)
    mlstm_norm_g = 1.0 + nrm(ks[12], (DEPTH, BRANCH_WIDTH), 0.05)
    w_branch = nrm(ks[13], (DEPTH, N_BRANCHES, BRANCH_WIDTH, D_MODEL), BRANCH_WIDTH ** -0.5)
    w_out = nrm(ks[14], (DEPTH, D_MODEL, D_MODEL), D_MODEL ** -0.5)
    norm_post = 1.0 + nrm(ks[15], (DEPTH, D_MODEL), 0.05)
    ple_proj = nrm(ks[16], (DEPTH, PLE_DIM, D_MODEL), PLE_DIM ** -0.5)
    ple_norm = 1.0 + nrm(ks[17], (DEPTH, D_MODEL), 0.05)
    ple_gate = nrm(ks[18], (DEPTH, D_MODEL, D_MODEL), D_MODEL ** -0.5)
    return {'x': x, 'p': p, 'positions': positions, 'norm_pre': norm_pre, 'w_in': w_in,
            'gmlp_ln_g': gmlp_ln_g, 'gmlp_ln_b': gmlp_ln_b, 'gmlp_ws': gmlp_ws, 'gmlp_bs': gmlp_bs,
            'attn_sinks': attn_sinks, 'mlstm_ib': mlstm_ib, 'mlstm_fb': mlstm_fb,
            'mlstm_norm_g': mlstm_norm_g, 'w_branch': w_branch, 'w_out': w_out,
            'norm_post': norm_post, 'ple_proj': ple_proj, 'ple_norm': ple_norm, 'ple_gate': ple_gate}


def reference(x, p, positions, norm_pre, w_in, gmlp_ln_g, gmlp_ln_b, gmlp_ws, gmlp_bs,
              attn_sinks, mlstm_ib, mlstm_fb, mlstm_norm_g, w_branch, w_out, norm_post,
              ple_proj, ple_norm, ple_gate):
    inv_freq = ROPE_THETA ** (-jnp.arange(0, ROT_DIM, 2, dtype=jnp.float32) / ROT_DIM)
    ang = positions.astype(jnp.float32)[..., None] * inv_freq
    cos = jnp.cos(ang)[:, :, None, :]
    sin = jnp.sin(ang)[:, :, None, :]
    for i in range(DEPTH):
        x = hybrid_layer(x, p[i], cos, sin, norm_pre[i], w_in[i], gmlp_ln_g[i], gmlp_ln_b[i],
                         gmlp_ws[i], gmlp_bs[i], attn_sinks[i], mlstm_ib[i], mlstm_fb[i],
                         mlstm_norm_g[i], w_branch[i], w_out[i], norm_post[i],
                         ple_proj[i], ple_norm[i], ple_gate[i])
    return x
```

```python
import functools

import jax
import jax.numpy as jnp
from jax import lax
from jax.experimental import pallas as pl
from jax.experimental.pallas import tpu as pltpu

F32 = jnp.float32
BF16 = jnp.bfloat16

D_MODEL = 4096
PLE_DIM = 256
N_BRANCHES = 3
BRANCH_WIDTH = D_MODEL // 2
GMLP_CHUNK = 128
GMLP_GROUPS = 8
GMLP_GROUP_WIDTH = BRANCH_WIDTH // GMLP_GROUPS
SWA_HEAD_DIM = 64
SWA_Q_HEADS = BRANCH_WIDTH // SWA_HEAD_DIM
SWA_KV_HEADS = 4
SWA_GROUP = SWA_Q_HEADS // SWA_KV_HEADS
SWA_BLOCK = 128
SWA_KV_WIDTH = SWA_KV_HEADS * SWA_HEAD_DIM
ROT_DIM = SWA_HEAD_DIM // 4
ROPE_THETA = 500000.0
MLSTM_HEADS = 4
MLSTM_V_DIM = BRANCH_WIDTH // MLSTM_HEADS
MLSTM_QK_DIM = MLSTM_V_DIM // 2
MLSTM_CHUNK = 64
GATE_SOFTCAP = 15.0
NORM_EPS = 1e-6
NEG_INF = -1e30

LANES = 128
GATE_ROWS = 16
VMEM_LIMIT = 56 * 1024 * 1024

_SPLIT = (
    ("a_u", BRANCH_WIDTH), ("a_v", BRANCH_WIDTH), ("a_z", BRANCH_WIDTH),
    ("b_q", BRANCH_WIDTH), ("b_k", SWA_KV_WIDTH), ("b_v", SWA_KV_WIDTH), ("b_z", BRANCH_WIDTH),
    ("c_q", MLSTM_HEADS * MLSTM_QK_DIM), ("c_k", MLSTM_HEADS * MLSTM_QK_DIM),
    ("c_v", BRANCH_WIDTH), ("c_i", MLSTM_HEADS), ("c_f", MLSTM_HEADS),
    ("c_o", BRANCH_WIDTH), ("c_z", BRANCH_WIDTH), ("gates", N_BRANCHES * D_MODEL),
)
_SRC = {}
_off = 0
for _name, _size in _SPLIT:
    _SRC[_name] = (_off, _size)
    _off += _size

_GROUPS = {
    "gelu": ("a_u", "a_v"),
    "silu": ("a_z", "b_z", "c_z"),
    "sigmoid": ("c_o", "gates"),
    "linear": ("b_q", "b_k", "b_v", "c_q", "c_k", "c_v"),
}


def _group_offset(group, name):
    off = 0
    for n in _GROUPS[group]:
        if n == name:
            return off
        off += _SRC[n][1]
    raise KeyError(name)


def _params(*sem):
    return pltpu.CompilerParams(dimension_semantics=sem, vmem_limit_bytes=VMEM_LIMIT)


def _gelu(x):
    return 0.5 * x * (1.0 + lax.erf(x * (0.5 ** 0.5)))


def _sigmoid(x):
    return jax.nn.sigmoid(x)


def _silu(x):
    return x * jax.nn.sigmoid(x)


def _identity(x):
    return x


_ACTS = {"gelu": _gelu, "silu": _silu, "sigmoid": _sigmoid, "linear": _identity}


def _softcap(z):
    return GATE_SOFTCAP * jnp.tanh(z / GATE_SOFTCAP)


def _log_sigmoid(x):
    return -(jnp.maximum(-x, 0.0) + jnp.log1p(jnp.exp(-jnp.abs(x))))


def _prenorm_body(x_ref, g_ref, wif_ref, wift_ref, brow_ref, bcol_ref, h_ref, gc_ref, gt_ref):
    x = x_ref[...]
    y = x * lax.rsqrt(jnp.mean(x * x, axis=-1, keepdims=True) + NORM_EPS) * g_ref[...]
    hb = y.astype(BF16)
    h_ref[...] = hb
    pre_c = jnp.dot(hb, wif_ref[...], preferred_element_type=F32) + brow_ref[...]
    sc = _softcap(pre_c)
    lane = lax.broadcasted_iota(jnp.int32, sc.shape, 1)
    gc_ref[...] = jnp.where(lane >= MLSTM_HEADS, _log_sigmoid(sc), sc)
    pre_t = lax.dot_general(wift_ref[...], hb, (((1,), (1,)), ((), ())),
                            preferred_element_type=F32) + bcol_ref[...]
    st = _softcap(pre_t)
    row = lax.broadcasted_iota(jnp.int32, st.shape, 0)
    gt_ref[...] = jnp.where(row >= MLSTM_HEADS, _log_sigmoid(st), st)


def _prenorm(x2d, g, wif, wift, brow, bcol, tm=256):
    m = x2d.shape[0]
    return pl.pallas_call(
        _prenorm_body,
        grid=(m // tm,),
        in_specs=[
            pl.BlockSpec((tm, D_MODEL), lambda i: (i, 0)),
            pl.BlockSpec((1, D_MODEL), lambda i: (0, 0)),
            pl.BlockSpec((D_MODEL, LANES), lambda i: (0, 0)),
            pl.BlockSpec((GATE_ROWS, D_MODEL), lambda i: (0, 0)),
            pl.BlockSpec((1, LANES), lambda i: (0, 0)),
            pl.BlockSpec((GATE_ROWS, 1), lambda i: (0, 0)),
        ],
        out_specs=[
            pl.BlockSpec((tm, D_MODEL), lambda i: (i, 0)),
            pl.BlockSpec((tm, LANES), lambda i: (i, 0)),
            pl.BlockSpec((GATE_ROWS, tm), lambda i: (0, i)),
        ],
        out_shape=[
            jax.ShapeDtypeStruct((m, D_MODEL), BF16),
            jax.ShapeDtypeStruct((m, LANES), F32),
            jax.ShapeDtypeStruct((GATE_ROWS, m), F32),
        ],
        compiler_params=_params("arbitrary"),
        name="prenorm",
    )(x2d, g, wif, wift, brow, bcol)


def _proj_body(act, h_ref, w_ref, o_ref):
    acc = jnp.dot(h_ref[...], w_ref[...], preferred_element_type=F32)
    o_ref[...] = act(acc).astype(o_ref.dtype)


def _proj(h, w, act_name, tm, tn):
    m, k = h.shape
    n = w.shape[1]
    return pl.pallas_call(
        functools.partial(_proj_body, _ACTS[act_name]),
        grid=(n // tn, m // tm),
        in_specs=[
            pl.BlockSpec((tm, k), lambda j, i: (i, 0)),
            pl.BlockSpec((k, tn), lambda j, i: (0, j)),
        ],
        out_specs=pl.BlockSpec((tm, tn), lambda j, i: (i, j)),
        out_shape=jax.ShapeDtypeStruct((m, n), BF16),
        compiler_params=_params("arbitrary", "arbitrary"),
        name="proj_" + act_name,
    )(h, w)


def _gmlp_body(chunks, u_ref, v_ref, z_ref, lg_ref, lb_ref, ws_ref, bst_ref, o_ref):
    r = lax.broadcasted_iota(jnp.int32, (GMLP_CHUNK, GMLP_CHUNK), 0)
    c = lax.broadcasted_iota(jnp.int32, (GMLP_CHUNK, GMLP_CHUNK), 1)
    tri = r >= c
    wmix = [jnp.where(tri, ws_ref[g], 0.0).astype(BF16) for g in range(GMLP_GROUPS)]
    for ch in range(chunks):
        rows = pl.ds(ch * GMLP_CHUNK, GMLP_CHUNK)
        v = v_ref[rows, :].astype(F32)
        mu = jnp.mean(v, axis=-1, keepdims=True)
        d = v - mu
        var = jnp.mean(d * d, axis=-1, keepdims=True)
        vn = (d * lax.rsqrt(var + NORM_EPS) * lg_ref[...] + lb_ref[...]).astype(BF16)
        for g in range(GMLP_GROUPS):
            cols = pl.ds(g * GMLP_GROUP_WIDTH, GMLP_GROUP_WIDTH)
            mixed = jnp.dot(wmix[g], vn[:, g * GMLP_GROUP_WIDTH:(g + 1) * GMLP_GROUP_WIDTH],
                            preferred_element_type=F32) + bst_ref[:, g:g + 1]
            y = u_ref[rows, cols].astype(F32) * mixed * z_ref[rows, cols].astype(F32)
            o_ref[rows, cols] = y.astype(o_ref.dtype)


def _gmlp(p_gelu, p_silu, ln_g, ln_b, ws, bst, ts=512):
    m = p_gelu.shape[0]
    w = BRANCH_WIDTH
    zblk = _group_offset("silu", "a_z") // w
    return pl.pallas_call(
        functools.partial(_gmlp_body, ts // GMLP_CHUNK),
        grid=(m // ts,),
        in_specs=[
            pl.BlockSpec((ts, w), lambda i: (i, 0)),
            pl.BlockSpec((ts, w), lambda i: (i, 1)),
            pl.BlockSpec((ts, w), lambda i: (i, zblk)),
            pl.BlockSpec((1, w), lambda i: (0, 0)),
            pl.BlockSpec((1, w), lambda i: (0, 0)),
            pl.BlockSpec((GMLP_GROUPS, GMLP_CHUNK, GMLP_CHUNK), lambda i: (0, 0, 0)),
            pl.BlockSpec((GMLP_CHUNK, GMLP_GROUPS), lambda i: (0, 0)),
        ],
        out_specs=pl.BlockSpec((ts, w), lambda i: (i, 0)),
        out_shape=jax.ShapeDtypeStruct((m, w), BF16),
        compiler_params=_params("arbitrary"),
        name="gmlp",
    )(p_gelu, p_gelu, p_silu, ln_g, ln_b, ws, bst)


def _rope_tables(pos_ref, inv_ref):
    ang = pos_ref[...] * inv_ref[...]
    d = lax.broadcasted_iota(jnp.int32, ang.shape, 1) % SWA_HEAD_DIM
    half = ROT_DIM // 2
    cosf = jnp.where(d < ROT_DIM, jnp.cos(ang), 1.0)
    sinf = jnp.sin(ang)
    s_lo = jnp.where(d < half, -sinf, 0.0)
    s_hi = jnp.where((d >= half) & (d < ROT_DIM), sinf, 0.0)
    return cosf, s_lo, s_hi


def _rope_tile(t, tables):
    cosf, s_lo, s_hi = tables
    half = ROT_DIM // 2
    up = pltpu.roll(t, LANES - half, 1)
    dn = pltpu.roll(t, half, 1)
    return t * cosf + up * s_lo + dn * s_hi


def _swa_body(q_ref, kc_ref, kp_ref, vc_ref, vp_ref, z_ref, pc_ref, pp_ref, inv_ref, sink_ref,
              o_ref):
    n = pl.program_id(1)
    tab_c = _rope_tables(pc_ref, inv_ref)
    tab_p = _rope_tables(pp_ref, inv_ref)
    lane = lax.broadcasted_iota(jnp.int32, (2 * SWA_BLOCK, LANES), 1)
    lane_half = lane // SWA_HEAD_DIM

    qi = lax.broadcasted_iota(jnp.int32, (SWA_BLOCK, 2 * SWA_BLOCK), 0)
    kj = lax.broadcasted_iota(jnp.int32, (SWA_BLOCK, 2 * SWA_BLOCK), 1)
    diff = qi + SWA_BLOCK - kj
    mask = (diff >= 0) & (diff < SWA_BLOCK) & ((n > 0) | (kj >= SWA_BLOCK))

    heads_per_tile = LANES // SWA_HEAD_DIM
    for kt in range(SWA_KV_WIDTH // LANES):
        cols = pl.ds(kt * LANES, LANES)
        k_band = jnp.concatenate(
            [_rope_tile(kp_ref[:, cols].astype(F32), tab_p),
             _rope_tile(kc_ref[:, cols].astype(F32), tab_c)], axis=0)
        v_band = jnp.concatenate([vp_ref[:, cols], vc_ref[:, cols]], axis=0).astype(F32)
        for hh in range(heads_per_tile):
            kvh = kt * heads_per_tile + hh
            k_keep = jnp.where(lane_half == hh, k_band, 0.0)
            v_keep = jnp.where(lane_half == hh, v_band, 0.0)
            k_swap = pltpu.roll(k_keep, SWA_HEAD_DIM, 1)
            v_swap = pltpu.roll(v_keep, SWA_HEAD_DIM, 1)
            k_at = [k_keep.astype(BF16), k_swap.astype(BF16)] if hh == 0 else \
                   [k_swap.astype(BF16), k_keep.astype(BF16)]
            v_at = [v_keep.astype(BF16), v_swap.astype(BF16)] if hh == 0 else \
                   [v_swap.astype(BF16), v_keep.astype(BF16)]
            for qt in range(SWA_GROUP // heads_per_tile):
                tile = (kvh * SWA_GROUP) // heads_per_tile + qt
                qcols = pl.ds(tile * LANES, LANES)
                q_rot = _rope_tile(q_ref[:, qcols].astype(F32), tab_c).astype(BF16)
                o_tile = jnp.zeros((SWA_BLOCK, LANES), F32)
                for qh in range(heads_per_tile):
                    head = tile * heads_per_tile + qh
                    s = lax.dot_general(q_rot, k_at[qh], (((1,), (1,)), ((), ())),
                                        preferred_element_type=F32) * (SWA_HEAD_DIM ** -0.5)
                    s = jnp.where(mask, s, NEG_INF)
                    sink = sink_ref[:, head:head + 1]
                    mx = jnp.maximum(jnp.max(s, axis=-1, keepdims=True), sink)
                    p = jnp.exp(s - mx)
                    den = jnp.sum(p, axis=-1, keepdims=True) + jnp.exp(sink - mx)
                    pv = jnp.dot(p.astype(BF16), v_at[qh], preferred_element_type=F32)
                    o_tile = o_tile + pv / den
                o_ref[:, qcols] = (o_tile * z_ref[:, qcols].astype(F32)).astype(o_ref.dtype)


def _swa(p_lin, p_silu, posf, inv_lane, sinks, batch, seq):
    m = p_lin.shape[0]
    nb = seq // SWA_BLOCK
    w = BRANCH_WIDTH
    kblk = _group_offset("linear", "b_k") // SWA_KV_WIDTH
    vblk = _group_offset("linear", "b_v") // SWA_KV_WIDTH
    zblk = _group_offset("silu", "b_z") // w

    def cur(b, n):
        return b * nb + n

    def prev(b, n):
        return b * nb + jnp.maximum(n - 1, 0)

    return pl.pallas_call(
        _swa_body,
        grid=(batch, nb),
        in_specs=[
            pl.BlockSpec((SWA_BLOCK, w), lambda b, n: (cur(b, n), 0)),
            pl.BlockSpec((SWA_BLOCK, SWA_KV_WIDTH), lambda b, n: (cur(b, n), kblk)),
            pl.BlockSpec((SWA_BLOCK, SWA_KV_WIDTH), lambda b, n: (prev(b, n), kblk)),
            pl.BlockSpec((SWA_BLOCK, SWA_KV_WIDTH), lambda b, n: (cur(b, n), vblk)),
            pl.BlockSpec((SWA_BLOCK, SWA_KV_WIDTH), lambda b, n: (prev(b, n), vblk)),
            pl.BlockSpec((SWA_BLOCK, w), lambda b, n: (cur(b, n), zblk)),
            pl.BlockSpec((SWA_BLOCK, 1), lambda b, n: (cur(b, n), 0)),
            pl.BlockSpec((SWA_BLOCK, 1), lambda b, n: (prev(b, n), 0)),
            pl.BlockSpec((1, LANES), lambda b, n: (0, 0)),
            pl.BlockSpec((1, SWA_Q_HEADS), lambda b, n: (0, 0)),
        ],
        out_specs=pl.BlockSpec((SWA_BLOCK, w), lambda b, n: (cur(b, n), 0)),
        out_shape=jax.ShapeDtypeStruct((m, w), BF16),
        compiler_params=_params("arbitrary", "arbitrary"),
        name="swa",
    )(p_lin, p_lin, p_lin, p_lin, p_lin, p_silu, posf, posf, inv_lane, sinks)


def _mlstm_body(chunks, q_ref, k_ref, v_ref, o_ref, z_ref, gc_ref, gt_ref, ng_ref, y_ref,
                c_ref, n_ref, m_ref):
    head = pl.program_id(1)
    step = pl.program_id(2)
    L = MLSTM_CHUNK

    @pl.when(step == 0)
    def _():
        c_ref[...] = jnp.zeros_like(c_ref)
        n_ref[...] = jnp.zeros_like(n_ref)
        m_ref[...] = jnp.zeros_like(m_ref)

    r = lax.broadcasted_iota(jnp.int32, (L, L), 0)
    c = lax.broadcasted_iota(jnp.int32, (L, L), 1)
    causal = r >= c
    lane_g = lax.broadcasted_iota(jnp.int32, (L, LANES), 1)
    row_g = lax.broadcasted_iota(jnp.int32, (GATE_ROWS, L), 0)

    for ch in range(chunks):
        rows = pl.ds(ch * L, L)
        gc = gc_ref[rows, :]
        gt = gt_ref[:, ch * L:(ch + 1) * L]
        i_col = jnp.sum(jnp.where(lane_g == head, gc, 0.0), axis=1, keepdims=True)
        f_col = jnp.sum(jnp.where(lane_g == head + MLSTM_HEADS, gc, 0.0), axis=1, keepdims=True)
        i_row = jnp.sum(jnp.where(row_g == head, gt, 0.0), axis=0, keepdims=True)
        f_row = jnp.sum(jnp.where(row_g == head + MLSTM_HEADS, gt, 0.0), axis=0, keepdims=True)

        b_col = jnp.sum(jnp.where(causal, f_row, 0.0), axis=1, keepdims=True)
        b_row = jnp.sum(jnp.where(r <= c, f_col, 0.0), axis=0, keepdims=True)
        g_tot = jnp.sum(f_row, axis=1, keepdims=True)
        m_prev = m_ref[...]

        log_d = jnp.where(causal, b_col - b_row + i_row, NEG_INF)
        m_inter = b_col + m_prev
        m_t = jnp.maximum(m_inter, jnp.max(log_d, axis=1, keepdims=True))
        dmat = jnp.exp(log_d - m_t)
        a = jnp.exp(m_inter - m_t)

        qf = q_ref[rows, :].astype(F32) * (MLSTM_QK_DIM ** -0.5)
        qs = qf.astype(BF16)
        k = k_ref[rows, :]
        kf = k.astype(F32)
        v = v_ref[rows, :]

        qk = lax.dot_general(qs, k, (((1,), (1,)), ((), ())), preferred_element_type=F32)
        s = qk * dmat
        c_prev = c_ref[...]
        num = jnp.dot(s.astype(BF16), v, preferred_element_type=F32) + \
            a * jnp.dot(qs, c_prev.astype(BF16), preferred_element_type=F32)
        qn = jnp.sum(qf * n_ref[...], axis=1, keepdims=True)
        den = jnp.sum(s, axis=1, keepdims=True) + a * qn
        hout = num / jnp.maximum(jnp.abs(den), jnp.exp(-m_t))

        w_col = g_tot - b_col + i_col
        w_row = g_tot - b_row + i_row
        m_new = jnp.maximum(g_tot + m_prev, jnp.max(w_row, axis=1, keepdims=True))
        wgt = jnp.exp(w_col - m_new)
        decay = jnp.exp(g_tot + m_prev - m_new)
        kw = kf * wgt
        c_ref[...] = decay * c_prev + lax.dot_general(
            kw.astype(BF16), v, (((0,), (0,)), ((), ())), preferred_element_type=F32)
        n_ref[...] = decay * n_ref[...] + jnp.sum(kw, axis=0, keepdims=True)
        m_ref[...] = m_new

        hn = hout * lax.rsqrt(jnp.mean(hout * hout, axis=1, keepdims=True) + NORM_EPS)
        hn = hn * ng_ref[...]
        y = hn * o_ref[rows, :].astype(F32) * z_ref[rows, :].astype(F32)
        y_ref[rows, :] = y.astype(y_ref.dtype)


def _mlstm(p_lin, p_sig, p_silu, gcol, gtr, norm_g, batch, seq, ts=256):
    m = p_lin.shape[0]
    nt = seq // ts
    qblk = _group_offset("linear", "c_q") // MLSTM_QK_DIM
    kblk = _group_offset("linear", "c_k") // MLSTM_QK_DIM
    vblk = _group_offset("linear", "c_v") // MLSTM_V_DIM
    oblk = _group_offset("sigmoid", "c_o") // MLSTM_V_DIM
    zblk = _group_offset("silu", "c_z") // MLSTM_V_DIM

    def row(b, t):
        return b * nt + t

    return pl.pallas_call(
        functools.partial(_mlstm_body, ts // MLSTM_CHUNK),
        grid=(batch, MLSTM_HEADS, nt),
        in_specs=[
            pl.BlockSpec((ts, MLSTM_QK_DIM), lambda b, h, t: (row(b, t), qblk + h)),
            pl.BlockSpec((ts, MLSTM_QK_DIM), lambda b, h, t: (row(b, t), kblk + h)),
            pl.BlockSpec((ts, MLSTM_V_DIM), lambda b, h, t: (row(b, t), vblk + h)),
            pl.BlockSpec((ts, MLSTM_V_DIM), lambda b, h, t: (row(b, t), oblk + h)),
            pl.BlockSpec((ts, MLSTM_V_DIM), lambda b, h, t: (row(b, t), zblk + h)),
            pl.BlockSpec((ts, LANES), lambda b, h, t: (row(b, t), 0)),
            pl.BlockSpec((GATE_ROWS, ts), lambda b, h, t: (0, row(b, t))),
            pl.BlockSpec((1, MLSTM_V_DIM), lambda b, h, t: (0, h)),
        ],
        out_specs=pl.BlockSpec((ts, MLSTM_V_DIM), lambda b, h, t: (row(b, t), h)),
        out_shape=jax.ShapeDtypeStruct((m, BRANCH_WIDTH), BF16),
        scratch_shapes=[
            pltpu.VMEM((MLSTM_QK_DIM, MLSTM_V_DIM), F32),
            pltpu.VMEM((1, MLSTM_QK_DIM), F32),
            pltpu.VMEM((1, 1), F32),
        ],
        compiler_params=_params("arbitrary", "arbitrary", "arbitrary"),
        name="mlstm",
    )(p_lin, p_lin, p_lin, p_sig, p_silu, gcol, gtr, norm_g)


def _merge_body(ya_ref, yb_ref, yc_ref, w_ref, g0_ref, g1_ref, g2_ref, o_ref):
    acc = g0_ref[...].astype(F32) * jnp.dot(ya_ref[...], w_ref[0], preferred_element_type=F32)
    acc = acc + g1_ref[...].astype(F32) * jnp.dot(yb_ref[...], w_ref[1],
                                                  preferred_element_type=F32)
    acc = acc + g2_ref[...].astype(F32) * jnp.dot(yc_ref[...], w_ref[2],
                                                  preferred_element_type=F32)
    o_ref[...] = acc.astype(o_ref.dtype)


def _merge(ya, yb, yc, wb, p_sig, tm=512, tn=1024):
    m = ya.shape[0]
    w = BRANCH_WIDTH
    g0 = _group_offset("sigmoid", "gates") // tn
    gstep = D_MODEL // tn
    yspec = pl.BlockSpec((tm, w), lambda j, i: (i, 0))
    return pl.pallas_call(
        _merge_body,
        grid=(D_MODEL // tn, m // tm),
        in_specs=[
            yspec, yspec, yspec,
            pl.BlockSpec((N_BRANCHES, w, tn), lambda j, i: (0, 0, j)),
            pl.BlockSpec((tm, tn), lambda j, i: (i, g0 + j)),
            pl.BlockSpec((tm, tn), lambda j, i: (i, g0 + gstep + j)),
            pl.BlockSpec((tm, tn), lambda j, i: (i, g0 + 2 * gstep + j)),
        ],
        out_specs=pl.BlockSpec((tm, tn), lambda j, i: (i, j)),
        out_shape=jax.ShapeDtypeStruct((m, D_MODEL), BF16),
        compiler_params=_params("arbitrary", "arbitrary"),
        name="merge",
    )(ya, yb, yc, wb, p_sig, p_sig, p_sig)


def _outproj_body(a_ref, w_ref, t_ref, ssq_ref):
    j = pl.program_id(1)
    acc = jnp.dot(a_ref[...], w_ref[...], preferred_element_type=F32)
    t_ref[...] = acc
    part = jnp.broadcast_to(jnp.sum(acc * acc, axis=-1, keepdims=True), ssq_ref.shape)

    @pl.when(j == 0)
    def _():
        ssq_ref[...] = part

    @pl.when(j > 0)
    def _():
        ssq_ref[...] += part


def _outproj(a, w, tm=1024, tn=512):
    m, k = a.shape
    n = w.shape[1]
    return pl.pallas_call(
        _outproj_body,
        grid=(m // tm, n // tn),
        in_specs=[
            pl.BlockSpec((tm, k), lambda i, j: (i, 0)),
            pl.BlockSpec((k, tn), lambda i, j: (0, j)),
        ],
        out_specs=[
            pl.BlockSpec((tm, tn), lambda i, j: (i, j)),
            pl.BlockSpec((tm, LANES), lambda i, j: (i, 0)),
        ],
        out_shape=[
            jax.ShapeDtypeStruct((m, n), F32),
            jax.ShapeDtypeStruct((m, LANES), F32),
        ],
        compiler_params=_params("arbitrary", "arbitrary"),
        name="outproj",
    )(a, w)


def _postnorm_body(x_ref, t_ref, ssq_ref, g_ref, x1_ref, x1b_ref):
    rs = lax.rsqrt(ssq_ref[:, 0:1] * (1.0 / D_MODEL) + NORM_EPS)
    x1 = x_ref[...] + t_ref[...] * rs * g_ref[...]
    x1_ref[...] = x1
    x1b_ref[...] = x1.astype(BF16)


def _postnorm(x2d, t, ssq, g, tm=256):
    m = x2d.shape[0]
    return pl.pallas_call(
        _postnorm_body,
        grid=(m // tm,),
        in_specs=[
            pl.BlockSpec((tm, D_MODEL), lambda i: (i, 0)),
            pl.BlockSpec((tm, D_MODEL), lambda i: (i, 0)),
            pl.BlockSpec((tm, LANES), lambda i: (i, 0)),
            pl.BlockSpec((1, D_MODEL), lambda i: (0, 0)),
        ],
        out_specs=[
            pl.BlockSpec((tm, D_MODEL), lambda i: (i, 0)),
            pl.BlockSpec((tm, D_MODEL), lambda i: (i, 0)),
        ],
        out_shape=[
            jax.ShapeDtypeStruct((m, D_MODEL), F32),
            jax.ShapeDtypeStruct((m, D_MODEL), BF16),
        ],
        compiler_params=_params("arbitrary"),
        name="postnorm",
    )(x2d, t, ssq, g)


def _ple_body(p_ref, w_ref, g_ref, e_ref):
    t = jnp.dot(p_ref[...].astype(BF16), w_ref[...], preferred_element_type=F32)
    e = t * lax.rsqrt(jnp.mean(t * t, axis=-1, keepdims=True) + NORM_EPS) * g_ref[...]
    e_ref[...] = e.astype(e_ref.dtype)


def _ple(p2d, w, g, tm=256):
    m = p2d.shape[0]
    return pl.pallas_call(
        _ple_body,
        grid=(m // tm,),
        in_specs=[
            pl.BlockSpec((tm, PLE_DIM), lambda i: (i, 0)),
            pl.BlockSpec((PLE_DIM, D_MODEL), lambda i: (0, 0)),
            pl.BlockSpec((1, D_MODEL), lambda i: (0, 0)),
        ],
        out_specs=pl.BlockSpec((tm, D_MODEL), lambda i: (i, 0)),
        out_shape=jax.ShapeDtypeStruct((m, D_MODEL), BF16),
        compiler_params=_params("arbitrary"),
        name="ple_embed",
    )(p2d, w, g)


def _plegate_body(a_ref, w_ref, x1_ref, e_ref, o_ref):
    acc = jnp.dot(a_ref[...], w_ref[...], preferred_element_type=F32)
    o_ref[...] = x1_ref[...] + _sigmoid(acc) * e_ref[...].astype(F32)


def _plegate(x1b, w, x1, e, tm=1024, tn=512):
    m, k = x1b.shape
    n = w.shape[1]
    return pl.pallas_call(
        _plegate_body,
        grid=(m // tm, n // tn),
        in_specs=[
            pl.BlockSpec((tm, k), lambda i, j: (i, 0)),
            pl.BlockSpec((k, tn), lambda i, j: (0, j)),
            pl.BlockSpec((tm, tn), lambda i, j: (i, j)),
            pl.BlockSpec((tm, tn), lambda i, j: (i, j)),
        ],
        out_specs=pl.BlockSpec((tm, tn), lambda i, j: (i, j)),
        out_shape=jax.ShapeDtypeStruct((m, n), F32),
        compiler_params=_params("arbitrary", "arbitrary"),
        name="ple_gate",
    )(x1b, w, x1, e)


def _group_weight(w_in_l, group):
    parts = [w_in_l[:, _SRC[n][0]:_SRC[n][0] + _SRC[n][1]] for n in _GROUPS[group]]
    return jnp.concatenate(parts, axis=1).astype(BF16)


def _gate_weights(w_in_l, ib, fb):
    i0 = _SRC["c_i"][0]
    wif = w_in_l[:, i0:i0 + 2 * MLSTM_HEADS].astype(BF16)
    wif_col = jnp.pad(wif, ((0, 0), (0, LANES - 2 * MLSTM_HEADS)))
    wif_row = jnp.pad(wif.T, ((0, GATE_ROWS - 2 * MLSTM_HEADS), (0, 0)))
    bias = jnp.concatenate([ib, fb]).astype(F32)
    brow = jnp.pad(bias, (0, LANES - 2 * MLSTM_HEADS)).reshape(1, LANES)
    bcol = jnp.pad(bias, (0, GATE_ROWS - 2 * MLSTM_HEADS)).reshape(GATE_ROWS, 1)
    return wif_col, wif_row, brow, bcol


def kernel(x, p, positions, norm_pre, w_in, gmlp_ln_g, gmlp_ln_b, gmlp_ws, gmlp_bs, attn_sinks,
           mlstm_ib, mlstm_fb, mlstm_norm_g, w_branch, w_out, norm_post, ple_proj, ple_norm,
           ple_gate):
    batch, seq, d = x.shape
    depth = w_in.shape[0]
    m = batch * seq
    assert d == D_MODEL and seq % 512 == 0

    inv_freq = ROPE_THETA ** (-jnp.arange(0, ROT_DIM, 2, dtype=F32) / ROT_DIM)
    lane_d = jnp.arange(LANES) % SWA_HEAD_DIM
    inv_lane = jnp.where(lane_d < ROT_DIM, inv_freq[lane_d % (ROT_DIM // 2)], 0.0).reshape(1, LANES)
    posf = positions.astype(F32).reshape(m, 1)

    xc = x.reshape(m, d)
    for l in range(depth):
        wif_col, wif_row, brow, bcol = _gate_weights(w_in[l], mlstm_ib[l], mlstm_fb[l])
        h, gcol, gtr = _prenorm(xc, norm_pre[l].reshape(1, d), wif_col, wif_row, brow, bcol)

        p_gelu = _proj(h, _group_weight(w_in[l], "gelu"), "gelu", 1024, 1024)
        p_silu = _proj(h, _group_weight(w_in[l], "silu"), "silu", 1024, 1024)
        p_sig = _proj(h, _group_weight(w_in[l], "sigmoid"), "sigmoid", 1024, 1024)
        p_lin = _proj(h, _group_weight(w_in[l], "linear"), "linear", 1024, 512)

        ya = _gmlp(p_gelu, p_silu, gmlp_ln_g[l].reshape(1, -1), gmlp_ln_b[l].reshape(1, -1),
                   gmlp_ws[l], gmlp_bs[l].T)
        yb = _swa(p_lin, p_silu, posf, inv_lane, attn_sinks[l].reshape(1, -1), batch, seq)
        yc = _mlstm(p_lin, p_sig, p_silu, gcol, gtr, mlstm_norm_g[l].reshape(1, -1), batch, seq)

        mixed = _merge(ya, yb, yc, w_branch[l].astype(BF16), p_sig)
        t, ssq = _outproj(mixed, w_out[l].astype(BF16))
        x1, x1b = _postnorm(xc, t, ssq, norm_post[l].reshape(1, d))
        e = _ple(p[l].reshape(m, PLE_DIM), ple_proj[l].astype(BF16), ple_norm[l].reshape(1, d))
        xc = _plegate(x1b, ple_gate[l].astype(BF16), x1, e)
    return xc.reshape(batch, seq, d)
```

```python
import functools

import jax
import jax.numpy as jnp
from jax import lax
from jax.experimental import pallas as pl
from jax.experimental.pallas import tpu as pltpu

F32 = jnp.float32
BF16 = jnp.bfloat16

D_MODEL = 4096
PLE_DIM = 256
N_BRANCHES = 3
BRANCH_WIDTH = D_MODEL // 2
GMLP_CHUNK = 128
GMLP_GROUPS = 8
GMLP_GROUP_WIDTH = BRANCH_WIDTH // GMLP_GROUPS
SWA_HEAD_DIM = 64
SWA_Q_HEADS = BRANCH_WIDTH // SWA_HEAD_DIM
SWA_KV_HEADS = 4
SWA_GROUP = SWA_Q_HEADS // SWA_KV_HEADS
SWA_BLOCK = 128
SWA_KV_WIDTH = SWA_KV_HEADS * SWA_HEAD_DIM
ROT_DIM = SWA_HEAD_DIM // 4
ROPE_THETA = 500000.0
MLSTM_HEADS = 4
MLSTM_V_DIM = BRANCH_WIDTH // MLSTM_HEADS
MLSTM_QK_DIM = MLSTM_V_DIM // 2
MLSTM_CHUNK = 64
GATE_SOFTCAP = 15.0
NORM_EPS = 1e-6
NEG_INF = -1e30

LANES = 128
GATE_ROWS = 16
VMEM_LIMIT = 56 * 1024 * 1024
CAST_ROWS = 256
TM = 1024
TN = 512

_SPLIT = (
    ("a_u", BRANCH_WIDTH), ("a_v", BRANCH_WIDTH), ("a_z", BRANCH_WIDTH),
    ("b_q", BRANCH_WIDTH), ("b_k", SWA_KV_WIDTH), ("b_v", SWA_KV_WIDTH), ("b_z", BRANCH_WIDTH),
    ("c_q", MLSTM_HEADS * MLSTM_QK_DIM), ("c_k", MLSTM_HEADS * MLSTM_QK_DIM),
    ("c_v", BRANCH_WIDTH), ("c_i", MLSTM_HEADS), ("c_f", MLSTM_HEADS),
    ("c_o", BRANCH_WIDTH), ("c_z", BRANCH_WIDTH), ("gates", N_BRANCHES * D_MODEL),
)
_SRC = {}
_off = 0
for _name, _size in _SPLIT:
    _SRC[_name] = (_off, _size)
    _off += _size
N_IN = _off
GATE_COLS = 2 * MLSTM_HEADS
GATE_START = _SRC["c_i"][0]
assert GATE_START % TN == 0


def _params(*sem):
    return pltpu.CompilerParams(dimension_semantics=sem, vmem_limit_bytes=VMEM_LIMIT)


def _gelu(x):
    return 0.5 * x * (1.0 + lax.erf(x * (0.5 ** 0.5)))


def _sigmoid(x):
    return jax.nn.sigmoid(x)


def _silu(x):
    return x * jax.nn.sigmoid(x)


def _identity(x):
    return x


_ACTS = {"gelu": _gelu, "silu": _silu, "sigmoid": _sigmoid, "linear": _identity}


def _softcap(z):
    return GATE_SOFTCAP * jnp.tanh(z / GATE_SOFTCAP)


def _log_sigmoid(x):
    return -(jnp.maximum(-x, 0.0) + jnp.log1p(jnp.exp(-jnp.abs(x))))


def _cast_weight(w_ref, wb_ref, wx_ref=None):
    k, tn = wb_ref.shape

    def body(r, carry):
        rs = pl.ds(pl.multiple_of(r * CAST_ROWS, CAST_ROWS), CAST_ROWS)
        if wx_ref is None:
            wb_ref[rs, :] = w_ref[rs, :].astype(BF16)
        else:
            keep = LANES - GATE_COLS
            lane = lax.broadcasted_iota(jnp.int32, (CAST_ROWS, LANES), 1)
            tiles = [w_ref[rs, t * LANES:(t + 1) * LANES] for t in range(tn // LANES)]
            tiles.append(wx_ref[rs, :])
            rolled = [pltpu.roll(t, keep, 1) for t in tiles]
            for t in range(tn // LANES):
                wb_ref[rs, t * LANES:(t + 1) * LANES] = jnp.where(
                    lane < keep, rolled[t], rolled[t + 1]).astype(BF16)
        return carry

    lax.fori_loop(0, k // CAST_ROWS, body, 0)


def _prenorm_body(x_ref, g_ref, wif_ref, brow_ref, h_ref, gc_ref, gt_ref, wb_ref):
    @pl.when(pl.program_id(0) == 0)
    def _():
        wb_ref[...] = wif_ref[...].astype(BF16)

    x = x_ref[...]
    y = x * lax.rsqrt(jnp.mean(x * x, axis=-1, keepdims=True) + NORM_EPS) * g_ref[...]
    hb = y.astype(BF16)
    h_ref[...] = hb
    pre = jnp.dot(hb, wb_ref[...], preferred_element_type=F32) + brow_ref[...]
    sc = _softcap(pre)
    lane = lax.broadcasted_iota(jnp.int32, sc.shape, 1)
    gc = jnp.where(lane >= MLSTM_HEADS, _log_sigmoid(sc), sc)
    gc_ref[...] = gc
    gt_ref[...] = gc.T[:GATE_ROWS, :]


def _prenorm(x2d, norm_pre, w_in, brow, layer, tm=256):
    m = x2d.shape[0]
    return pl.pallas_call(
        _prenorm_body,
        grid=(m // tm,),
        in_specs=[
            pl.BlockSpec((tm, D_MODEL), lambda i: (i, 0)),
            pl.BlockSpec((None, 1, D_MODEL), lambda i: (layer, 0, 0)),
            pl.BlockSpec((None, D_MODEL, LANES), lambda i: (layer, 0, GATE_START // LANES)),
            pl.BlockSpec((1, LANES), lambda i: (0, 0)),
        ],
        out_specs=[
            pl.BlockSpec((tm, D_MODEL), lambda i: (i, 0)),
            pl.BlockSpec((tm, LANES), lambda i: (i, 0)),
            pl.BlockSpec((GATE_ROWS, tm), lambda i: (0, i)),
        ],
        out_shape=[
            jax.ShapeDtypeStruct((m, D_MODEL), BF16),
            jax.ShapeDtypeStruct((m, LANES), F32),
            jax.ShapeDtypeStruct((GATE_ROWS, m), F32),
        ],
        scratch_shapes=[pltpu.VMEM((D_MODEL, LANES), BF16)],
        compiler_params=_params("arbitrary"),
        name="prenorm",
    )(x2d, norm_pre, w_in, brow)


def _proj_body(act, shifted, *refs):
    if shifted:
        h_ref, w_ref, wx_ref, o_ref, wb_ref = refs
    else:
        h_ref, w_ref, o_ref, wb_ref = refs
        wx_ref = None

    @pl.when(pl.program_id(1) == 0)
    def _():
        _cast_weight(w_ref, wb_ref, wx_ref)

    acc = jnp.dot(h_ref[...], wb_ref[...], preferred_element_type=F32)
    o_ref[...] = act(acc).astype(o_ref.dtype)


def _proj(h, w_in, layer, names, act_name):
    m, k = h.shape
    start = _SRC[names[0]][0]
    width = sum(_SRC[n][1] for n in names)
    shifted = start > GATE_START
    if shifted:
        start -= GATE_COLS
    assert start % TN == 0 and width % TN == 0
    blk0 = start // TN
    in_specs = [
        pl.BlockSpec((TM, k), lambda j, i: (i, 0)),
        pl.BlockSpec((None, k, TN), lambda j, i: (layer, 0, blk0 + j)),
    ]
    operands = [h, w_in]
    if shifted:
        per = TN // LANES
        in_specs.append(
            pl.BlockSpec((None, k, LANES), lambda j, i: (layer, 0, (blk0 + j + 1) * per)))
        operands.append(w_in)
    return pl.pallas_call(
        functools.partial(_proj_body, _ACTS[act_name], shifted),
        grid=(width // TN, m // TM),
        in_specs=in_specs,
        out_specs=pl.BlockSpec((TM, TN), lambda j, i: (i, j)),
        out_shape=jax.ShapeDtypeStruct((m, width), BF16),
        scratch_shapes=[pltpu.VMEM((k, TN), BF16)],
        compiler_params=_params("arbitrary", "arbitrary"),
        name="proj_" + names[0],
    )(*operands)


def _gmlp_body(chunks, u_ref, v_ref, z_ref, lg_ref, lb_ref, ws_ref, bst_ref, o_ref):
    r = lax.broadcasted_iota(jnp.int32, (GMLP_CHUNK, GMLP_CHUNK), 0)
    c = lax.broadcasted_iota(jnp.int32, (GMLP_CHUNK, GMLP_CHUNK), 1)
    tri = r >= c
    wmix = [jnp.where(tri, ws_ref[g], 0.0).astype(BF16) for g in range(GMLP_GROUPS)]
    for ch in range(chunks):
        rows = pl.ds(ch * GMLP_CHUNK, GMLP_CHUNK)
        v = v_ref[rows, :].astype(F32)
        mu = jnp.mean(v, axis=-1, keepdims=True)
        d = v - mu
        var = jnp.mean(d * d, axis=-1, keepdims=True)
        vn = (d * lax.rsqrt(var + NORM_EPS) * lg_ref[...] + lb_ref[...]).astype(BF16)
        for g in range(GMLP_GROUPS):
            cols = pl.ds(g * GMLP_GROUP_WIDTH, GMLP_GROUP_WIDTH)
            mixed = jnp.dot(wmix[g], vn[:, g * GMLP_GROUP_WIDTH:(g + 1) * GMLP_GROUP_WIDTH],
                            preferred_element_type=F32) + bst_ref[:, g:g + 1]
            y = u_ref[rows, cols].astype(F32) * mixed * z_ref[rows, cols].astype(F32)
            o_ref[rows, cols] = y.astype(o_ref.dtype)


def _gmlp(p_uv, p_z, ln_g, ln_b, ws, bst, layer, ts=512):
    m = p_uv.shape[0]
    w = BRANCH_WIDTH
    return pl.pallas_call(
        functools.partial(_gmlp_body, ts // GMLP_CHUNK),
        grid=(m // ts,),
        in_specs=[
            pl.BlockSpec((ts, w), lambda i: (i, 0)),
            pl.BlockSpec((ts, w), lambda i: (i, 1)),
            pl.BlockSpec((ts, w), lambda i: (i, 0)),
            pl.BlockSpec((None, 1, w), lambda i: (layer, 0, 0)),
            pl.BlockSpec((None, 1, w), lambda i: (layer, 0, 0)),
            pl.BlockSpec((None, GMLP_GROUPS, GMLP_CHUNK, GMLP_CHUNK), lambda i: (layer, 0, 0, 0)),
            pl.BlockSpec((None, GMLP_CHUNK, GMLP_GROUPS), lambda i: (layer, 0, 0)),
        ],
        out_specs=pl.BlockSpec((ts, w), lambda i: (i, 0)),
        out_shape=jax.ShapeDtypeStruct((m, w), BF16),
        compiler_params=_params("arbitrary"),
        name="gmlp",
    )(p_uv, p_uv, p_z, ln_g, ln_b, ws, bst)


def _rope_tables(pos_ref, inv_ref):
    ang = pos_ref[...] * inv_ref[...]
    d = lax.broadcasted_iota(jnp.int32, ang.shape, 1) % SWA_HEAD_DIM
    half = ROT_DIM // 2
    cosf = jnp.where(d < ROT_DIM, jnp.cos(ang), 1.0)
    sinf = jnp.sin(ang)
    s_lo = jnp.where(d < half, -sinf, 0.0)
    s_hi = jnp.where((d >= half) & (d < ROT_DIM), sinf, 0.0)
    return cosf, s_lo, s_hi


def _rope_tile(t, tables):
    cosf, s_lo, s_hi = tables
    half = ROT_DIM // 2
    up = pltpu.roll(t, LANES - half, 1)
    dn = pltpu.roll(t, half, 1)
    return t * cosf + up * s_lo + dn * s_hi


def _swa_body(q_ref, kc_ref, kp_ref, vc_ref, vp_ref, z_ref, pc_ref, pp_ref, inv_ref, sink_ref,
              o_ref):
    n = pl.program_id(1)
    tab_c = _rope_tables(pc_ref, inv_ref)
    tab_p = _rope_tables(pp_ref, inv_ref)
    heads_per_tile = LANES // SWA_HEAD_DIM
    tiles_per_group = SWA_GROUP // heads_per_tile
    rows = tiles_per_group * SWA_BLOCK

    lane_half = lax.broadcasted_iota(jnp.int32, (2 * SWA_BLOCK, LANES), 1) // SWA_HEAD_DIM
    qi = lax.broadcasted_iota(jnp.int32, (rows, 2 * SWA_BLOCK), 0) % SWA_BLOCK
    kj = lax.broadcasted_iota(jnp.int32, (rows, 2 * SWA_BLOCK), 1)
    diff = qi + SWA_BLOCK - kj
    mask = (diff >= 0) & (diff < SWA_BLOCK) & ((n > 0) | (kj >= SWA_BLOCK))

    scale = SWA_HEAD_DIM ** -0.5
    for kt in range(SWA_KV_WIDTH // LANES):
        cols = pl.ds(kt * LANES, LANES)
        k_band = jnp.concatenate(
            [_rope_tile(kp_ref[:, cols].astype(F32), tab_p),
             _rope_tile(kc_ref[:, cols].astype(F32), tab_c)], axis=0)
        v_band = jnp.concatenate([vp_ref[:, cols], vc_ref[:, cols]], axis=0).astype(F32)
        for hh in range(heads_per_tile):
            kvh = kt * heads_per_tile + hh
            k_keep = jnp.where(lane_half == hh, k_band, 0.0)
            v_keep = jnp.where(lane_half == hh, v_band, 0.0)
            k_swap = pltpu.roll(k_keep, SWA_HEAD_DIM, 1)
            v_swap = pltpu.roll(v_keep, SWA_HEAD_DIM, 1)
            k_at = [k_keep.astype(BF16), k_swap.astype(BF16)]
            v_at = [v_keep.astype(BF16), v_swap.astype(BF16)]
            if hh == 1:
                k_at.reverse()
                v_at.reverse()
            tile0 = kvh * tiles_per_group
            q_stack = jnp.concatenate(
                [(_rope_tile(q_ref[:, pl.ds((tile0 + t) * LANES, LANES)].astype(F32), tab_c)
                  * scale).astype(BF16) for t in range(tiles_per_group)], axis=0)
            o_stack = jnp.zeros((rows, LANES), F32)
            for qh in range(heads_per_tile):
                sink = jnp.concatenate(
                    [jnp.broadcast_to(
                        sink_ref[:, (tile0 + t) * heads_per_tile + qh:
                                 (tile0 + t) * heads_per_tile + qh + 1], (SWA_BLOCK, 1))
                     for t in range(tiles_per_group)], axis=0)
                s = lax.dot_general(q_stack, k_at[qh], (((1,), (1,)), ((), ())),
                                    preferred_element_type=F32)
                s = jnp.where(mask, s, NEG_INF)
                mx = jnp.maximum(jnp.max(s, axis=-1, keepdims=True), sink)
                p = jnp.exp(s - mx)
                den = jnp.sum(p, axis=-1, keepdims=True) + jnp.exp(sink - mx)
                pv = jnp.dot(p.astype(BF16), v_at[qh], preferred_element_type=F32)
                o_stack = o_stack + pv / den
            for t in range(tiles_per_group):
                qcols = pl.ds((tile0 + t) * LANES, LANES)
                o_t = o_stack[t * SWA_BLOCK:(t + 1) * SWA_BLOCK, :]
                o_ref[:, qcols] = (o_t * z_ref[:, qcols].astype(F32)).astype(o_ref.dtype)


def _swa(p_qkv, p_z, posf, inv_lane, sinks, layer, batch, seq):
    m = p_qkv.shape[0]
    nb = seq // SWA_BLOCK
    w = BRANCH_WIDTH
    kblk = w // SWA_KV_WIDTH
    vblk = kblk + 1

    def cur(b, n):
        return b * nb + n

    def prev(b, n):
        return b * nb + jnp.maximum(n - 1, 0)

    return pl.pallas_call(
        _swa_body,
        grid=(batch, nb),
        in_specs=[
            pl.BlockSpec((SWA_BLOCK, w), lambda b, n: (cur(b, n), 0)),
            pl.BlockSpec((SWA_BLOCK, SWA_KV_WIDTH), lambda b, n: (cur(b, n), kblk)),
            pl.BlockSpec((SWA_BLOCK, SWA_KV_WIDTH), lambda b, n: (prev(b, n), kblk)),
            pl.BlockSpec((SWA_BLOCK, SWA_KV_WIDTH), lambda b, n: (cur(b, n), vblk)),
            pl.BlockSpec((SWA_BLOCK, SWA_KV_WIDTH), lambda b, n: (prev(b, n), vblk)),
            pl.BlockSpec((SWA_BLOCK, w), lambda b, n: (cur(b, n), 0)),
            pl.BlockSpec((SWA_BLOCK, 1), lambda b, n: (cur(b, n), 0)),
            pl.BlockSpec((SWA_BLOCK, 1), lambda b, n: (prev(b, n), 0)),
            pl.BlockSpec((1, LANES), lambda b, n: (0, 0)),
            pl.BlockSpec((None, 1, SWA_Q_HEADS), lambda b, n: (layer, 0, 0)),
        ],
        out_specs=pl.BlockSpec((SWA_BLOCK, w), lambda b, n: (cur(b, n), 0)),
        out_shape=jax.ShapeDtypeStruct((m, w), BF16),
        compiler_params=_params("arbitrary", "arbitrary"),
        name="swa",
    )(p_qkv, p_qkv, p_qkv, p_qkv, p_qkv, p_z, posf, posf, inv_lane, sinks)


def _mlstm_body(chunks, q_ref, k_ref, v_ref, o_ref, z_ref, gc_ref, gt_ref, ng_ref, y_ref,
                c_ref, n_ref, m_ref):
    head = pl.program_id(1)
    step = pl.program_id(2)
    L = MLSTM_CHUNK

    @pl.when(step == 0)
    def _():
        c_ref[...] = jnp.zeros_like(c_ref)
        n_ref[...] = jnp.zeros_like(n_ref)
        m_ref[...] = jnp.zeros_like(m_ref)

    r = lax.broadcasted_iota(jnp.int32, (L, L), 0)
    c = lax.broadcasted_iota(jnp.int32, (L, L), 1)
    causal = r >= c
    lane_g = lax.broadcasted_iota(jnp.int32, (L, LANES), 1)
    row_g = lax.broadcasted_iota(jnp.int32, (GATE_ROWS, L), 0)

    for ch in range(chunks):
        rows = pl.ds(ch * L, L)
        gc = gc_ref[rows, :]
        gt = gt_ref[:, ch * L:(ch + 1) * L]
        i_col = jnp.sum(jnp.where(lane_g == head, gc, 0.0), axis=1, keepdims=True)
        f_col = jnp.sum(jnp.where(lane_g == head + MLSTM_HEADS, gc, 0.0), axis=1, keepdims=True)
        i_row = jnp.sum(jnp.where(row_g == head, gt, 0.0), axis=0, keepdims=True)
        f_row = jnp.sum(jnp.where(row_g == head + MLSTM_HEADS, gt, 0.0), axis=0, keepdims=True)

        b_col = jnp.sum(jnp.where(causal, f_row, 0.0), axis=1, keepdims=True)
        b_row = jnp.sum(jnp.where(r <= c, f_col, 0.0), axis=0, keepdims=True)
        g_tot = jnp.sum(f_row, axis=1, keepdims=True)
        m_prev = m_ref[...]

        log_d = jnp.where(causal, b_col - b_row + i_row, NEG_INF)
        m_inter = b_col + m_prev
        m_t = jnp.maximum(m_inter, jnp.max(log_d, axis=1, keepdims=True))
        dmat = jnp.exp(log_d - m_t)
        a = jnp.exp(m_inter - m_t)

        qf = q_ref[rows, :].astype(F32) * (MLSTM_QK_DIM ** -0.5)
        qs = qf.astype(BF16)
        k = k_ref[rows, :]
        kf = k.astype(F32)
        v = v_ref[rows, :]

        qk = lax.dot_general(qs, k, (((1,), (1,)), ((), ())), preferred_element_type=F32)
        s = qk * dmat
        c_prev = c_ref[...]
        num = jnp.dot(s.astype(BF16), v, preferred_element_type=F32) + \
            a * jnp.dot(qs, c_prev.astype(BF16), preferred_element_type=F32)
        qn = jnp.sum(qf * n_ref[...], axis=1, keepdims=True)
        den = jnp.sum(s, axis=1, keepdims=True) + a * qn
        hout = num / jnp.maximum(jnp.abs(den), jnp.exp(-m_t))

        w_col = g_tot - b_col + i_col
        w_row = g_tot - b_row + i_row
        m_new = jnp.maximum(g_tot + m_prev, jnp.max(w_row, axis=1, keepdims=True))
        wgt = jnp.exp(w_col - m_new)
        decay = jnp.exp(g_tot + m_prev - m_new)
        kw = kf * wgt
        c_ref[...] = decay * c_prev + lax.dot_general(
            kw.astype(BF16), v, (((0,), (0,)), ((), ())), preferred_element_type=F32)
        n_ref[...] = decay * n_ref[...] + jnp.sum(kw, axis=0, keepdims=True)
        m_ref[...] = m_new

        hn = hout * lax.rsqrt(jnp.mean(hout * hout, axis=1, keepdims=True) + NORM_EPS)
        hn = hn * ng_ref[...]
        y = hn * o_ref[rows, :].astype(F32) * z_ref[rows, :].astype(F32)
        y_ref[rows, :] = y.astype(y_ref.dtype)


def _mlstm(p_qkv, p_o, p_z, gcol, gtr, norm_g, layer, batch, seq, ts=256):
    m = p_qkv.shape[0]
    nt = seq // ts
    kblk = MLSTM_HEADS
    vblk = 2 * MLSTM_HEADS * MLSTM_QK_DIM // MLSTM_V_DIM

    def row(b, t):
        return b * nt + t

    return pl.pallas_call(
        functools.partial(_mlstm_body, ts // MLSTM_CHUNK),
        grid=(batch, MLSTM_HEADS, nt),
        in_specs=[
            pl.BlockSpec((ts, MLSTM_QK_DIM), lambda b, h, t: (row(b, t), h)),
            pl.BlockSpec((ts, MLSTM_QK_DIM), lambda b, h, t: (row(b, t), kblk + h)),
            pl.BlockSpec((ts, MLSTM_V_DIM), lambda b, h, t: (row(b, t), vblk + h)),
            pl.BlockSpec((ts, MLSTM_V_DIM), lambda b, h, t: (row(b, t), h)),
            pl.BlockSpec((ts, MLSTM_V_DIM), lambda b, h, t: (row(b, t), h)),
            pl.BlockSpec((ts, LANES), lambda b, h, t: (row(b, t), 0)),
            pl.BlockSpec((GATE_ROWS, ts), lambda b, h, t: (0, row(b, t))),
            pl.BlockSpec((None, 1, MLSTM_V_DIM), lambda b, h, t: (layer, 0, h)),
        ],
        out_specs=pl.BlockSpec((ts, MLSTM_V_DIM), lambda b, h, t: (row(b, t), h)),
        out_shape=jax.ShapeDtypeStruct((m, BRANCH_WIDTH), BF16),
        scratch_shapes=[
            pltpu.VMEM((MLSTM_QK_DIM, MLSTM_V_DIM), F32),
            pltpu.VMEM((1, MLSTM_QK_DIM), F32),
            pltpu.VMEM((1, 1), F32),
        ],
        compiler_params=_params("arbitrary", "arbitrary", "arbitrary"),
        name="mlstm",
    )(p_qkv, p_qkv, p_qkv, p_o, p_z, gcol, gtr, norm_g)


def _merge_body(ya_ref, yb_ref, yc_ref, w_ref, g0_ref, g1_ref, g2_ref, o_ref, wb_ref):
    @pl.when(pl.program_id(1) == 0)
    def _():
        for b in range(N_BRANCHES):
            _cast_weight(w_ref.at[b], wb_ref.at[b])

    acc = g0_ref[...].astype(F32) * jnp.dot(ya_ref[...], wb_ref[0], preferred_element_type=F32)
    acc = acc + g1_ref[...].astype(F32) * jnp.dot(yb_ref[...], wb_ref[1],
                                                  preferred_element_type=F32)
    acc = acc + g2_ref[...].astype(F32) * jnp.dot(yc_ref[...], wb_ref[2],
                                                  preferred_element_type=F32)
    o_ref[...] = acc.astype(o_ref.dtype)


def _merge(ya, yb, yc, w_branch, p_gates, layer, tm=512):
    m = ya.shape[0]
    w = BRANCH_WIDTH
    gstep = D_MODEL // TN
    yspec = pl.BlockSpec((tm, w), lambda j, i: (i, 0))
    return pl.pallas_call(
        _merge_body,
        grid=(D_MODEL // TN, m // tm),
        in_specs=[
            yspec, yspec, yspec,
            pl.BlockSpec((None, N_BRANCHES, w, TN), lambda j, i: (layer, 0, 0, j)),
            pl.BlockSpec((tm, TN), lambda j, i: (i, j)),
            pl.BlockSpec((tm, TN), lambda j, i: (i, gstep + j)),
            pl.BlockSpec((tm, TN), lambda j, i: (i, 2 * gstep + j)),
        ],
        out_specs=pl.BlockSpec((tm, TN), lambda j, i: (i, j)),
        out_shape=jax.ShapeDtypeStruct((m, D_MODEL), BF16),
        scratch_shapes=[pltpu.VMEM((N_BRANCHES, w, TN), BF16)],
        compiler_params=_params("arbitrary", "arbitrary"),
        name="merge",
    )(ya, yb, yc, w_branch, p_gates, p_gates, p_gates)


def _outproj_body(a_ref, w_ref, t_ref, ssq_ref, wb_ref):
    j = pl.program_id(0)
    i = pl.program_id(1)

    @pl.when(i == 0)
    def _():
        _cast_weight(w_ref, wb_ref)

    acc = jnp.dot(a_ref[...], wb_ref[...], preferred_element_type=F32)
    t_ref[...] = acc.astype(t_ref.dtype)
    rows = pl.ds(pl.multiple_of(i * TM, TM), TM)
    part = jnp.broadcast_to(jnp.sum(acc * acc, axis=-1, keepdims=True), (TM, LANES))

    @pl.when(j == 0)
    def _():
        ssq_ref[rows, :] = part

    @pl.when(j > 0)
    def _():
        ssq_ref[rows, :] += part


def _outproj(a, w_out, layer):
    m, k = a.shape
    n = w_out.shape[-1]
    return pl.pallas_call(
        _outproj_body,
        grid=(n // TN, m // TM),
        in_specs=[
            pl.BlockSpec((TM, k), lambda j, i: (i, 0)),
            pl.BlockSpec((None, k, TN), lambda j, i: (layer, 0, j)),
        ],
        out_specs=[
            pl.BlockSpec((TM, TN), lambda j, i: (i, j)),
            pl.BlockSpec((m, LANES), lambda j, i: (0, 0)),
        ],
        out_shape=[
            jax.ShapeDtypeStruct((m, n), BF16),
            jax.ShapeDtypeStruct((m, LANES), F32),
        ],
        scratch_shapes=[pltpu.VMEM((k, TN), BF16)],
        compiler_params=_params("arbitrary", "arbitrary"),
        name="outproj",
    )(a, w_out)


def _post_residual(x, t, ssq, g):
    rs = lax.rsqrt(ssq * (1.0 / D_MODEL) + NORM_EPS)
    return x + t.astype(F32) * rs * g


def _postnorm_body(x_ref, t_ref, ssq_ref, g_ref, x1b_ref):
    x1b_ref[...] = _post_residual(x_ref[...], t_ref[...], ssq_ref[:, 0:1], g_ref[...]).astype(BF16)


def _postnorm(x2d, t, ssq, norm_post, layer, tm=256):
    m = x2d.shape[0]
    return pl.pallas_call(
        _postnorm_body,
        grid=(m // tm,),
        in_specs=[
            pl.BlockSpec((tm, D_MODEL), lambda i: (i, 0)),
            pl.BlockSpec((tm, D_MODEL), lambda i: (i, 0)),
            pl.BlockSpec((tm, LANES), lambda i: (i, 0)),
            pl.BlockSpec((None, 1, D_MODEL), lambda i: (layer, 0, 0)),
        ],
        out_specs=pl.BlockSpec((tm, D_MODEL), lambda i: (i, 0)),
        out_shape=jax.ShapeDtypeStruct((m, D_MODEL), BF16),
        compiler_params=_params("arbitrary"),
        name="postnorm",
    )(x2d, t, ssq, norm_post)


def _ple_body(p_ref, w_ref, g_ref, e_ref, wb_ref):
    @pl.when(pl.program_id(0) == 0)
    def _():
        wb_ref[...] = w_ref[...].astype(BF16)

    t = jnp.dot(p_ref[...].astype(BF16), wb_ref[...], preferred_element_type=F32)
    e = t * lax.rsqrt(jnp.mean(t * t, axis=-1, keepdims=True) + NORM_EPS) * g_ref[...]
    e_ref[...] = e.astype(e_ref.dtype)


def _ple(p3d, ple_proj, ple_norm, layer, tm=256):
    m = p3d.shape[1]
    return pl.pallas_call(
        _ple_body,
        grid=(m // tm,),
        in_specs=[
            pl.BlockSpec((None, tm, PLE_DIM), lambda i: (layer, i, 0)),
            pl.BlockSpec((None, PLE_DIM, D_MODEL), lambda i: (layer, 0, 0)),
            pl.BlockSpec((None, 1, D_MODEL), lambda i: (layer, 0, 0)),
        ],
        out_specs=pl.BlockSpec((tm, D_MODEL), lambda i: (i, 0)),
        out_shape=jax.ShapeDtypeStruct((m, D_MODEL), BF16),
        scratch_shapes=[pltpu.VMEM((PLE_DIM, D_MODEL), BF16)],
        compiler_params=_params("arbitrary"),
        name="ple_embed",
    )(p3d, ple_proj, ple_norm)


def _plegate_body(a_ref, w_ref, x_ref, t_ref, ssq_ref, g_ref, e_ref, o_ref, wb_ref):
    @pl.when(pl.program_id(1) == 0)
    def _():
        _cast_weight(w_ref, wb_ref)

    acc = jnp.dot(a_ref[...], wb_ref[...], preferred_element_type=F32)
    x1 = _post_residual(x_ref[...], t_ref[...], ssq_ref[:, 0:1], g_ref[...])
    o_ref[...] = x1 + _sigmoid(acc) * e_ref[...].astype(F32)


def _plegate(x1b, ple_gate, x2d, t, ssq, norm_post, e, layer):
    m, k = x1b.shape
    n = ple_gate.shape[-1]
    tile = pl.BlockSpec((TM, TN), lambda j, i: (i, j))
    return pl.pallas_call(
        _plegate_body,
        grid=(n // TN, m // TM),
        in_specs=[
            pl.BlockSpec((TM, k), lambda j, i: (i, 0)),
            pl.BlockSpec((None, k, TN), lambda j, i: (layer, 0, j)),
            tile, tile,
            pl.BlockSpec((TM, LANES), lambda j, i: (i, 0)),
            pl.BlockSpec((None, 1, TN), lambda j, i: (layer, 0, j)),
            tile,
        ],
        out_specs=tile,
        out_shape=jax.ShapeDtypeStruct((m, n), F32),
        scratch_shapes=[pltpu.VMEM((k, TN), BF16)],
        compiler_params=_params("arbitrary", "arbitrary"),
        name="ple_gate",
    )(x1b, ple_gate, x2d, t, ssq, norm_post, e)


def kernel(x, p, positions, norm_pre, w_in, gmlp_ln_g, gmlp_ln_b, gmlp_ws, gmlp_bs, attn_sinks,
           mlstm_ib, mlstm_fb, mlstm_norm_g, w_branch, w_out, norm_post, ple_proj, ple_norm,
           ple_gate):
    batch, seq, d = x.shape
    depth = w_in.shape[0]
    m = batch * seq
    assert d == D_MODEL and w_in.shape[-1] == N_IN and seq % 512 == 0 and m % TM == 0

    inv_freq = ROPE_THETA ** (-jnp.arange(0, ROT_DIM, 2, dtype=F32) / ROT_DIM)
    lane_d = jnp.arange(LANES) % SWA_HEAD_DIM
    inv_lane = jnp.where(lane_d < ROT_DIM, inv_freq[lane_d % (ROT_DIM // 2)], 0.0).reshape(1, LANES)
    posf = positions.astype(F32).reshape(m, 1)
    p3d = p.reshape(depth, m, PLE_DIM)
    bst = jnp.swapaxes(gmlp_bs, 1, 2)
    gate_bias = jnp.pad(jnp.concatenate([mlstm_ib, mlstm_fb], axis=1).astype(F32),
                        ((0, 0), (0, LANES - GATE_COLS)))

    def rows3(a):
        return a.reshape(depth, 1, a.shape[-1])

    norm_pre, norm_post, ple_norm = rows3(norm_pre), rows3(norm_post), rows3(ple_norm)
    gmlp_ln_g, gmlp_ln_b = rows3(gmlp_ln_g), rows3(gmlp_ln_b)
    attn_sinks, mlstm_norm_g = rows3(attn_sinks), rows3(mlstm_norm_g)

    xc = x.reshape(m, d)
    for l in range(depth):
        h, gcol, gtr = _prenorm(xc, norm_pre, w_in, gate_bias[l:l + 1], l)

        p_uv = _proj(h, w_in, l, ("a_u", "a_v"), "gelu")
        p_az = _proj(h, w_in, l, ("a_z",), "silu")
        p_bqkv = _proj(h, w_in, l, ("b_q", "b_k", "b_v"), "linear")
        p_bz = _proj(h, w_in, l, ("b_z",), "silu")
        p_cqkv = _proj(h, w_in, l, ("c_q", "c_k", "c_v"), "linear")
        p_co = _proj(h, w_in, l, ("c_o",), "sigmoid")
        p_cz = _proj(h, w_in, l, ("c_z",), "silu")
        p_gates = _proj(h, w_in, l, ("gates",), "sigmoid")

        ya = _gmlp(p_uv, p_az, gmlp_ln_g, gmlp_ln_b, gmlp_ws, bst, l)
        yb = _swa(p_bqkv, p_bz, posf, inv_lane, attn_sinks, l, batch, seq)
        yc = _mlstm(p_cqkv, p_co, p_cz, gcol, gtr, mlstm_norm_g, l, batch, seq)

        mixed = _merge(ya, yb, yc, w_branch, p_gates, l)
        t, ssq = _outproj(mixed, w_out, l)
        x1b = _postnorm(xc, t, ssq, norm_post, l)
        e = _ple(p3d, ple_proj, ple_norm, l)
        xc = _plegate(x1b, ple_gate, xc, t, ssq, norm_post, e, l)
    return xc.reshape(batch, seq, d)
```

```python
import functools

import jax
import jax.numpy as jnp
from jax import lax
from jax.experimental import pallas as pl
from jax.experimental.pallas import tpu as pltpu

F32 = jnp.float32
BF16 = jnp.bfloat16

D_MODEL = 4096
PLE_DIM = 256
N_BRANCHES = 3
BRANCH_WIDTH = D_MODEL // 2
GMLP_CHUNK = 128
GMLP_GROUPS = 8
GMLP_GROUP_WIDTH = BRANCH_WIDTH // GMLP_GROUPS
SWA_HEAD_DIM = 64
SWA_Q_HEADS = BRANCH_WIDTH // SWA_HEAD_DIM
SWA_KV_HEADS = 4
SWA_GROUP = SWA_Q_HEADS // SWA_KV_HEADS
SWA_BLOCK = 128
SWA_KV_WIDTH = SWA_KV_HEADS * SWA_HEAD_DIM
ROT_DIM = SWA_HEAD_DIM // 4
ROPE_THETA = 500000.0
MLSTM_HEADS = 4
MLSTM_V_DIM = BRANCH_WIDTH // MLSTM_HEADS
MLSTM_QK_DIM = MLSTM_V_DIM // 2
MLSTM_CHUNK = 64
GATE_SOFTCAP = 15.0
NORM_EPS = 1e-6
NEG_INF = -1e30

LANES = 128
GATE_ROWS = 16
VMEM_LIMIT = 56 * 1024 * 1024
CAST_ELEMS = 128 * 1024
TM = 1024
TN = 512

_SPLIT = (
    ("a_u", BRANCH_WIDTH), ("a_v", BRANCH_WIDTH), ("a_z", BRANCH_WIDTH),
    ("b_q", BRANCH_WIDTH), ("b_k", SWA_KV_WIDTH), ("b_v", SWA_KV_WIDTH), ("b_z", BRANCH_WIDTH),
    ("c_q", MLSTM_HEADS * MLSTM_QK_DIM), ("c_k", MLSTM_HEADS * MLSTM_QK_DIM),
    ("c_v", BRANCH_WIDTH), ("c_i", MLSTM_HEADS), ("c_f", MLSTM_HEADS),
    ("c_o", BRANCH_WIDTH), ("c_z", BRANCH_WIDTH), ("gates", N_BRANCHES * D_MODEL),
)
_SRC = {}
_off = 0
for _name, _size in _SPLIT:
    _SRC[_name] = (_off, _size)
    _off += _size
N_IN = _off
GATE_COLS = 2 * MLSTM_HEADS
GATE_START = _SRC["c_i"][0]


def _params(*sem):
    return pltpu.CompilerParams(dimension_semantics=sem, vmem_limit_bytes=VMEM_LIMIT)


def _gelu(x):
    return 0.5 * x * (1.0 + lax.erf(x * (0.5 ** 0.5)))


def _sigmoid(x):
    return jax.nn.sigmoid(x)


def _silu(x):
    return x * jax.nn.sigmoid(x)


def _identity(x):
    return x


_ACTS = {"gelu": _gelu, "silu": _silu, "sigmoid": _sigmoid, "linear": _identity}


def _softcap(z):
    return GATE_SOFTCAP * jnp.tanh(z / GATE_SOFTCAP)


def _log_sigmoid(x):
    return -(jnp.maximum(-x, 0.0) + jnp.log1p(jnp.exp(-jnp.abs(x))))


def _cast_weight(w_ref, wb_ref):
    nrows, ncols = wb_ref.shape
    chunk = CAST_ELEMS // ncols

    def body(r, carry):
        rs = pl.ds(pl.multiple_of(r * chunk, chunk), chunk)
        wb_ref[rs, :] = w_ref[rs, :].astype(BF16)
        return carry

    lax.fori_loop(0, nrows // chunk, body, 0)


def _prenorm_body(x_ref, g_ref, wif_ref, brow_ref, h_ref, gc_ref, gt_ref, wb_ref):
    @pl.when(pl.program_id(0) == 0)
    def _():
        wb_ref[...] = wif_ref[...].astype(BF16)

    x = x_ref[...]
    y = x * lax.rsqrt(jnp.mean(x * x, axis=-1, keepdims=True) + NORM_EPS) * g_ref[...]
    hb = y.astype(BF16)
    h_ref[...] = hb
    pre = lax.dot_general(hb, wb_ref[...], (((1,), (1,)), ((), ())),
                          preferred_element_type=F32) + brow_ref[...]
    sc = _softcap(pre)
    lane = lax.broadcasted_iota(jnp.int32, sc.shape, 1)
    gc = jnp.where(lane >= MLSTM_HEADS, _log_sigmoid(sc), sc)
    gc_ref[...] = gc
    gt_ref[...] = gc.T[:GATE_ROWS, :]


def _prenorm(x2d, norm_pre, w_t, brow, layer, tm=256):
    m = x2d.shape[0]
    assert GATE_START % LANES == 0
    return pl.pallas_call(
        _prenorm_body,
        grid=(m // tm,),
        in_specs=[
            pl.BlockSpec((tm, D_MODEL), lambda i: (i, 0)),
            pl.BlockSpec((None, 1, D_MODEL), lambda i: (layer, 0, 0)),
            pl.BlockSpec((None, LANES, D_MODEL), lambda i: (layer, GATE_START // LANES, 0)),
            pl.BlockSpec((1, LANES), lambda i: (0, 0)),
        ],
        out_specs=[
            pl.BlockSpec((tm, D_MODEL), lambda i: (i, 0)),
            pl.BlockSpec((tm, LANES), lambda i: (i, 0)),
            pl.BlockSpec((GATE_ROWS, tm), lambda i: (0, i)),
        ],
        out_shape=[
            jax.ShapeDtypeStruct((m, D_MODEL), BF16),
            jax.ShapeDtypeStruct((m, LANES), F32),
            jax.ShapeDtypeStruct((GATE_ROWS, m), F32),
        ],
        scratch_shapes=[pltpu.VMEM((LANES, D_MODEL), BF16)],
        compiler_params=_params("arbitrary"),
        name="prenorm",
    )(x2d, norm_pre, w_t, brow)


_NT = (((1,), (1,)), ((), ()))


def _proj_body(act, n_row_steps, h_ref, w0_ref, wn_ref, o_ref, wb_even, wb_odd):
    j = pl.program_id(0)
    i = pl.program_id(1)

    @pl.when((j == 0) & (i == 0))
    def _():
        _cast_weight(w0_ref, wb_even)

    piece = TN // n_row_steps
    rows = pl.ds(pl.multiple_of(i * piece, piece), piece)

    def step(wb_cur, wb_next):
        wb_next[rows, :] = wn_ref[rows, :].astype(BF16)
        acc = lax.dot_general(h_ref[...], wb_cur[...], _NT, preferred_element_type=F32)
        o_ref[...] = act(acc).astype(o_ref.dtype)

    @pl.when(j % 2 == 0)
    def _():
        step(wb_even, wb_odd)

    @pl.when(j % 2 == 1)
    def _():
        step(wb_odd, wb_even)


def _proj(h, w_t, layer, names, act_name):
    m, k = h.shape
    start = _SRC[names[0]][0]
    width = sum(_SRC[n][1] for n in names)
    n_col, n_row = width // TN, m // TM
    assert start % 8 == 0 and width % TN == 0 and TN % n_row == 0
    wblock = (None, pl.Element(TN), pl.Element(k))
    return pl.pallas_call(
        functools.partial(_proj_body, _ACTS[act_name], n_row),
        grid=(n_col, n_row),
        in_specs=[
            pl.BlockSpec((TM, k), lambda j, i: (i, 0)),
            pl.BlockSpec(wblock, lambda j, i: (layer, start, 0), pipeline_mode=pl.Buffered(1)),
            pl.BlockSpec(wblock, lambda j, i: (
                layer, pl.multiple_of(start + jnp.minimum(j + 1, n_col - 1) * TN, 8), 0)),
        ],
        out_specs=pl.BlockSpec((TM, TN), lambda j, i: (i, j)),
        out_shape=jax.ShapeDtypeStruct((m, width), BF16),
        scratch_shapes=[pltpu.VMEM((TN, k), BF16), pltpu.VMEM((TN, k), BF16)],
        compiler_params=_params("arbitrary", "arbitrary"),
        name="proj_" + names[0],
    )(h, w_t, w_t)


def _gmlp_body(chunks, u_ref, v_ref, z_ref, lg_ref, lb_ref, ws_ref, bst_ref, o_ref):
    r = lax.broadcasted_iota(jnp.int32, (GMLP_CHUNK, GMLP_CHUNK), 0)
    c = lax.broadcasted_iota(jnp.int32, (GMLP_CHUNK, GMLP_CHUNK), 1)
    tri = r >= c
    wmix = [jnp.where(tri, ws_ref[g], 0.0).astype(BF16) for g in range(GMLP_GROUPS)]
    for ch in range(chunks):
        rows = pl.ds(ch * GMLP_CHUNK, GMLP_CHUNK)
        v = v_ref[rows, :].astype(F32)
        mu = jnp.mean(v, axis=-1, keepdims=True)
        d = v - mu
        var = jnp.mean(d * d, axis=-1, keepdims=True)
        vn = (d * lax.rsqrt(var + NORM_EPS) * lg_ref[...] + lb_ref[...]).astype(BF16)
        for g in range(GMLP_GROUPS):
            cols = pl.ds(g * GMLP_GROUP_WIDTH, GMLP_GROUP_WIDTH)
            mixed = jnp.dot(wmix[g], vn[:, g * GMLP_GROUP_WIDTH:(g + 1) * GMLP_GROUP_WIDTH],
                            preferred_element_type=F32) + bst_ref[:, g:g + 1]
            y = u_ref[rows, cols].astype(F32) * mixed * z_ref[rows, cols].astype(F32)
            o_ref[rows, cols] = y.astype(o_ref.dtype)


def _gmlp(p_uv, p_z, ln_g, ln_b, ws, bst, layer, ts=512):
    m = p_uv.shape[0]
    w = BRANCH_WIDTH
    return pl.pallas_call(
        functools.partial(_gmlp_body, ts // GMLP_CHUNK),
        grid=(m // ts,),
        in_specs=[
            pl.BlockSpec((ts, w), lambda i: (i, 0)),
            pl.BlockSpec((ts, w), lambda i: (i, 1)),
            pl.BlockSpec((ts, w), lambda i: (i, 0)),
            pl.BlockSpec((None, 1, w), lambda i: (layer, 0, 0)),
            pl.BlockSpec((None, 1, w), lambda i: (layer, 0, 0)),
            pl.BlockSpec((None, GMLP_GROUPS, GMLP_CHUNK, GMLP_CHUNK), lambda i: (layer, 0, 0, 0)),
            pl.BlockSpec((None, GMLP_CHUNK, GMLP_GROUPS), lambda i: (layer, 0, 0)),
        ],
        out_specs=pl.BlockSpec((ts, w), lambda i: (i, 0)),
        out_shape=jax.ShapeDtypeStruct((m, w), BF16),
        compiler_params=_params("arbitrary"),
        name="gmlp",
    )(p_uv, p_uv, p_z, ln_g, ln_b, ws, bst)


def _rope_tables(pos_ref, inv_ref):
    ang = pos_ref[...] * inv_ref[...]
    d = lax.broadcasted_iota(jnp.int32, ang.shape, 1) % SWA_HEAD_DIM
    half = ROT_DIM // 2
    cosf = jnp.where(d < ROT_DIM, jnp.cos(ang), 1.0)
    sinf = jnp.sin(ang)
    s_lo = jnp.where(d < half, -sinf, 0.0)
    s_hi = jnp.where((d >= half) & (d < ROT_DIM), sinf, 0.0)
    return cosf, s_lo, s_hi


def _rope_tile(t, tables):
    cosf, s_lo, s_hi = tables
    half = ROT_DIM // 2
    up = pltpu.roll(t, LANES - half, 1)
    dn = pltpu.roll(t, half, 1)
    return t * cosf + up * s_lo + dn * s_hi


def _swa_body(q_ref, kc_ref, kp_ref, vc_ref, vp_ref, z_ref, pc_ref, pp_ref, inv_ref, sink_ref,
              o_ref):
    n = pl.program_id(1)
    tab_c = _rope_tables(pc_ref, inv_ref)
    tab_p = _rope_tables(pp_ref, inv_ref)
    heads_per_tile = LANES // SWA_HEAD_DIM
    tiles_per_group = SWA_GROUP // heads_per_tile
    rows = tiles_per_group * SWA_BLOCK

    lane_half = lax.broadcasted_iota(jnp.int32, (2 * SWA_BLOCK, LANES), 1) // SWA_HEAD_DIM
    qi = lax.broadcasted_iota(jnp.int32, (rows, 2 * SWA_BLOCK), 0) % SWA_BLOCK
    kj = lax.broadcasted_iota(jnp.int32, (rows, 2 * SWA_BLOCK), 1)
    diff = qi + SWA_BLOCK - kj
    mask = (diff >= 0) & (diff < SWA_BLOCK) & ((n > 0) | (kj >= SWA_BLOCK))

    scale = SWA_HEAD_DIM ** -0.5
    for kt in range(SWA_KV_WIDTH // LANES):
        cols = pl.ds(kt * LANES, LANES)
        k_band = jnp.concatenate(
            [_rope_tile(kp_ref[:, cols].astype(F32), tab_p),
             _rope_tile(kc_ref[:, cols].astype(F32), tab_c)], axis=0)
        v_band = jnp.concatenate([vp_ref[:, cols], vc_ref[:, cols]], axis=0).astype(F32)
        for hh in range(heads_per_tile):
            kvh = kt * heads_per_tile + hh
            k_keep = jnp.where(lane_half == hh, k_band, 0.0)
            v_keep = jnp.where(lane_half == hh, v_band, 0.0)
            k_swap = pltpu.roll(k_keep, SWA_HEAD_DIM, 1)
            v_swap = pltpu.roll(v_keep, SWA_HEAD_DIM, 1)
            k_at = [k_keep.astype(BF16), k_swap.astype(BF16)]
            v_at = [v_keep.astype(BF16), v_swap.astype(BF16)]
            if hh == 1:
                k_at.reverse()
                v_at.reverse()
            tile0 = kvh * tiles_per_group
            q_stack = jnp.concatenate(
                [(_rope_tile(q_ref[:, pl.ds((tile0 + t) * LANES, LANES)].astype(F32), tab_c)
                  * scale).astype(BF16) for t in range(tiles_per_group)], axis=0)
            o_stack = jnp.zeros((rows, LANES), F32)
            for qh in range(heads_per_tile):
                sink = jnp.concatenate(
                    [jnp.broadcast_to(
                        sink_ref[:, (tile0 + t) * heads_per_tile + qh:
                                 (tile0 + t) * heads_per_tile + qh + 1], (SWA_BLOCK, 1))
                     for t in range(tiles_per_group)], axis=0)
                s = lax.dot_general(q_stack, k_at[qh], (((1,), (1,)), ((), ())),
                                    preferred_element_type=F32)
                s = jnp.where(mask, s, NEG_INF)
                mx = jnp.maximum(jnp.max(s, axis=-1, keepdims=True), sink)
                p = jnp.exp(s - mx)
                den = jnp.sum(p, axis=-1, keepdims=True) + jnp.exp(sink - mx)
                pv = jnp.dot(p.astype(BF16), v_at[qh], preferred_element_type=F32)
                o_stack = o_stack + pv / den
            for t in range(tiles_per_group):
                qcols = pl.ds((tile0 + t) * LANES, LANES)
                o_t = o_stack[t * SWA_BLOCK:(t + 1) * SWA_BLOCK, :]
                o_ref[:, qcols] = (o_t * z_ref[:, qcols].astype(F32)).astype(o_ref.dtype)


def _swa(p_qkv, p_z, posf, inv_lane, sinks, layer, batch, seq):
    m = p_qkv.shape[0]
    nb = seq // SWA_BLOCK
    w = BRANCH_WIDTH
    kblk = w // SWA_KV_WIDTH
    vblk = kblk + 1

    def cur(b, n):
        return b * nb + n

    def prev(b, n):
        return b * nb + jnp.maximum(n - 1, 0)

    return pl.pallas_call(
        _swa_body,
        grid=(batch, nb),
        in_specs=[
            pl.BlockSpec((SWA_BLOCK, w), lambda b, n: (cur(b, n), 0)),
            pl.BlockSpec((SWA_BLOCK, SWA_KV_WIDTH), lambda b, n: (cur(b, n), kblk)),
            pl.BlockSpec((SWA_BLOCK, SWA_KV_WIDTH), lambda b, n: (prev(b, n), kblk)),
            pl.BlockSpec((SWA_BLOCK, SWA_KV_WIDTH), lambda b, n: (cur(b, n), vblk)),
            pl.BlockSpec((SWA_BLOCK, SWA_KV_WIDTH), lambda b, n: (prev(b, n), vblk)),
            pl.BlockSpec((SWA_BLOCK, w), lambda b, n: (cur(b, n), 0)),
            pl.BlockSpec((SWA_BLOCK, 1), lambda b, n: (cur(b, n), 0)),
            pl.BlockSpec((SWA_BLOCK, 1), lambda b, n: (prev(b, n), 0)),
            pl.BlockSpec((1, LANES), lambda b, n: (0, 0)),
            pl.BlockSpec((None, 1, SWA_Q_HEADS), lambda b, n: (layer, 0, 0)),
        ],
        out_specs=pl.BlockSpec((SWA_BLOCK, w), lambda b, n: (cur(b, n), 0)),
        out_shape=jax.ShapeDtypeStruct((m, w), BF16),
        compiler_params=_params("arbitrary", "arbitrary"),
        name="swa",
    )(p_qkv, p_qkv, p_qkv, p_qkv, p_qkv, p_z, posf, posf, inv_lane, sinks)


def _mlstm_body(chunks, q_ref, k_ref, v_ref, o_ref, z_ref, gc_ref, gt_ref, ng_ref, y_ref,
                c_ref, n_ref, m_ref):
    head = pl.program_id(1)
    step = pl.program_id(2)
    L = MLSTM_CHUNK

    @pl.when(step == 0)
    def _():
        c_ref[...] = jnp.zeros_like(c_ref)
        n_ref[...] = jnp.zeros_like(n_ref)
        m_ref[...] = jnp.zeros_like(m_ref)

    r = lax.broadcasted_iota(jnp.int32, (L, L), 0)
    c = lax.broadcasted_iota(jnp.int32, (L, L), 1)
    causal = r >= c
    lane_g = lax.broadcasted_iota(jnp.int32, (L, LANES), 1)
    row_g = lax.broadcasted_iota(jnp.int32, (GATE_ROWS, L), 0)

    for ch in range(chunks):
        rows = pl.ds(ch * L, L)
        gc = gc_ref[rows, :]
        gt = gt_ref[:, ch * L:(ch + 1) * L]
        i_col = jnp.sum(jnp.where(lane_g == head, gc, 0.0), axis=1, keepdims=True)
        f_col = jnp.sum(jnp.where(lane_g == head + MLSTM_HEADS, gc, 0.0), axis=1, keepdims=True)
        i_row = jnp.sum(jnp.where(row_g == head, gt, 0.0), axis=0, keepdims=True)
        f_row = jnp.sum(jnp.where(row_g == head + MLSTM_HEADS, gt, 0.0), axis=0, keepdims=True)

        b_col = jnp.sum(jnp.where(causal, f_row, 0.0), axis=1, keepdims=True)
        b_row = jnp.sum(jnp.where(r <= c, f_col, 0.0), axis=0, keepdims=True)
        g_tot = jnp.sum(f_row, axis=1, keepdims=True)
        m_prev = m_ref[...]

        log_d = jnp.where(causal, b_col - b_row + i_row, NEG_INF)
        m_inter = b_col + m_prev
        m_t = jnp.maximum(m_inter, jnp.max(log_d, axis=1, keepdims=True))
        dmat = jnp.exp(log_d - m_t)
        a = jnp.exp(m_inter - m_t)

        qf = q_ref[rows, :].astype(F32) * (MLSTM_QK_DIM ** -0.5)
        qs = qf.astype(BF16)
        k = k_ref[rows, :]
        kf = k.astype(F32)
        v = v_ref[rows, :]

        qk = lax.dot_general(qs, k, (((1,), (1,)), ((), ())), preferred_element_type=F32)
        s = qk * dmat
        c_prev = c_ref[...]
        num = jnp.dot(s.astype(BF16), v, preferred_element_type=F32) + \
            a * jnp.dot(qs, c_prev.astype(BF16), preferred_element_type=F32)
        qn = jnp.sum(qf * n_ref[...], axis=1, keepdims=True)
        den = jnp.sum(s, axis=1, keepdims=True) + a * qn
        hout = num / jnp.maximum(jnp.abs(den), jnp.exp(-m_t))

        w_col = g_tot - b_col + i_col
        w_row = g_tot - b_row + i_row
        m_new = jnp.maximum(g_tot + m_prev, jnp.max(w_row, axis=1, keepdims=True))
        wgt = jnp.exp(w_col - m_new)
        decay = jnp.exp(g_tot + m_prev - m_new)
        kw = kf * wgt
        c_ref[...] = decay * c_prev + lax.dot_general(
            kw.astype(BF16), v, (((0,), (0,)), ((), ())), preferred_element_type=F32)
        n_ref[...] = decay * n_ref[...] + jnp.sum(kw, axis=0, keepdims=True)
        m_ref[...] = m_new

        hn = hout * lax.rsqrt(jnp.mean(hout * hout, axis=1, keepdims=True) + NORM_EPS)
        hn = hn * ng_ref[...]
        y = hn * o_ref[rows, :].astype(F32) * z_ref[rows, :].astype(F32)
        y_ref[rows, :] = y.astype(y_ref.dtype)


def _mlstm(p_qkv, p_o, p_z, gcol, gtr, norm_g, layer, batch, seq, ts=256):
    m = p_qkv.shape[0]
    nt = seq // ts
    kblk = MLSTM_HEADS
    vblk = 2 * MLSTM_HEADS * MLSTM_QK_DIM // MLSTM_V_DIM

    def row(b, t):
        return b * nt + t

    return pl.pallas_call(
        functools.partial(_mlstm_body, ts // MLSTM_CHUNK),
        grid=(batch, MLSTM_HEADS, nt),
        in_specs=[
            pl.BlockSpec((ts, MLSTM_QK_DIM), lambda b, h, t: (row(b, t), h)),
            pl.BlockSpec((ts, MLSTM_QK_DIM), lambda b, h, t: (row(b, t), kblk + h)),
            pl.BlockSpec((ts, MLSTM_V_DIM), lambda b, h, t: (row(b, t), vblk + h)),
            pl.BlockSpec((ts, MLSTM_V_DIM), lambda b, h, t: (row(b, t), h)),
            pl.BlockSpec((ts, MLSTM_V_DIM), lambda b, h, t: (row(b, t), h)),
            pl.BlockSpec((ts, LANES), lambda b, h, t: (row(b, t), 0)),
            pl.BlockSpec((GATE_ROWS, ts), lambda b, h, t: (0, row(b, t))),
            pl.BlockSpec((None, 1, MLSTM_V_DIM), lambda b, h, t: (layer, 0, h)),
        ],
        out_specs=pl.BlockSpec((ts, MLSTM_V_DIM), lambda b, h, t: (row(b, t), h)),
        out_shape=jax.ShapeDtypeStruct((m, BRANCH_WIDTH), BF16),
        scratch_shapes=[
            pltpu.VMEM((MLSTM_QK_DIM, MLSTM_V_DIM), F32),
            pltpu.VMEM((1, MLSTM_QK_DIM), F32),
            pltpu.VMEM((1, 1), F32),
        ],
        compiler_params=_params("arbitrary", "arbitrary", "arbitrary"),
        name="mlstm",
    )(p_qkv, p_qkv, p_qkv, p_o, p_z, gcol, gtr, norm_g)


def _merge_body(ya_ref, yb_ref, yc_ref, w_ref, g0_ref, g1_ref, g2_ref, o_ref, wb_ref):
    @pl.when(pl.program_id(1) == 0)
    def _():
        for b in range(N_BRANCHES):
            _cast_weight(w_ref.at[b], wb_ref.at[b])

    acc = g0_ref[...].astype(F32) * jnp.dot(ya_ref[...], wb_ref[0], preferred_element_type=F32)
    acc = acc + g1_ref[...].astype(F32) * jnp.dot(yb_ref[...], wb_ref[1],
                                                  preferred_element_type=F32)
    acc = acc + g2_ref[...].astype(F32) * jnp.dot(yc_ref[...], wb_ref[2],
                                                  preferred_element_type=F32)
    o_ref[...] = acc.astype(o_ref.dtype)


def _merge(ya, yb, yc, w_branch, p_gates, layer, tm=512):
    m = ya.shape[0]
    w = BRANCH_WIDTH
    gstep = D_MODEL // TN
    yspec = pl.BlockSpec((tm, w), lambda j, i: (i, 0))
    return pl.pallas_call(
        _merge_body,
        grid=(D_MODEL // TN, m // tm),
        in_specs=[
            yspec, yspec, yspec,
            pl.BlockSpec((None, N_BRANCHES, w, TN), lambda j, i: (layer, 0, 0, j)),
            pl.BlockSpec((tm, TN), lambda j, i: (i, j)),
            pl.BlockSpec((tm, TN), lambda j, i: (i, gstep + j)),
            pl.BlockSpec((tm, TN), lambda j, i: (i, 2 * gstep + j)),
        ],
        out_specs=pl.BlockSpec((tm, TN), lambda j, i: (i, j)),
        out_shape=jax.ShapeDtypeStruct((m, D_MODEL), BF16),
        scratch_shapes=[pltpu.VMEM((N_BRANCHES, w, TN), BF16)],
        compiler_params=_params("arbitrary", "arbitrary"),
        name="merge",
    )(ya, yb, yc, w_branch, p_gates, p_gates, p_gates)


def _outproj_body(a_ref, w_ref, t_ref, ssq_ref, wb_ref):
    j = pl.program_id(0)
    i = pl.program_id(1)

    @pl.when(i == 0)
    def _():
        _cast_weight(w_ref, wb_ref)

    acc = jnp.dot(a_ref[...], wb_ref[...], preferred_element_type=F32)
    t_ref[...] = acc.astype(t_ref.dtype)
    rows = pl.ds(pl.multiple_of(i * TM, TM), TM)
    part = jnp.broadcast_to(jnp.sum(acc * acc, axis=-1, keepdims=True), (TM, LANES))

    @pl.when(j == 0)
    def _():
        ssq_ref[rows, :] = part

    @pl.when(j > 0)
    def _():
        ssq_ref[rows, :] += part


def _outproj(a, w_out, layer):
    m, k = a.shape
    n = w_out.shape[-1]
    return pl.pallas_call(
        _outproj_body,
        grid=(n // TN, m // TM),
        in_specs=[
            pl.BlockSpec((TM, k), lambda j, i: (i, 0)),
            pl.BlockSpec((None, k, TN), lambda j, i: (layer, 0, j)),
        ],
        out_specs=[
            pl.BlockSpec((TM, TN), lambda j, i: (i, j)),
            pl.BlockSpec((m, LANES), lambda j, i: (0, 0)),
        ],
        out_shape=[
            jax.ShapeDtypeStruct((m, n), BF16),
            jax.ShapeDtypeStruct((m, LANES), F32),
        ],
        scratch_shapes=[pltpu.VMEM((k, TN), BF16)],
        compiler_params=_params("arbitrary", "arbitrary"),
        name="outproj",
    )(a, w_out)


def _post_residual(x, t, ssq, g):
    rs = lax.rsqrt(ssq * (1.0 / D_MODEL) + NORM_EPS)
    return x + t.astype(F32) * rs * g


def _postnorm_body(x_ref, t_ref, ssq_ref, g_ref, x1b_ref):
    x1b_ref[...] = _post_residual(x_ref[...], t_ref[...], ssq_ref[:, 0:1], g_ref[...]).astype(BF16)


def _postnorm(x2d, t, ssq, norm_post, layer, tm=256):
    m = x2d.shape[0]
    return pl.pallas_call(
        _postnorm_body,
        grid=(m // tm,),
        in_specs=[
            pl.BlockSpec((tm, D_MODEL), lambda i: (i, 0)),
            pl.BlockSpec((tm, D_MODEL), lambda i: (i, 0)),
            pl.BlockSpec((tm, LANES), lambda i: (i, 0)),
            pl.BlockSpec((None, 1, D_MODEL), lambda i: (layer, 0, 0)),
        ],
        out_specs=pl.BlockSpec((tm, D_MODEL), lambda i: (i, 0)),
        out_shape=jax.ShapeDtypeStruct((m, D_MODEL), BF16),
        compiler_params=_params("arbitrary"),
        name="postnorm",
    )(x2d, t, ssq, norm_post)


def _ple_body(p_ref, w_ref, g_ref, e_ref, wb_ref):
    @pl.when(pl.program_id(0) == 0)
    def _():
        wb_ref[...] = w_ref[...].astype(BF16)

    t = jnp.dot(p_ref[...].astype(BF16), wb_ref[...], preferred_element_type=F32)
    e = t * lax.rsqrt(jnp.mean(t * t, axis=-1, keepdims=True) + NORM_EPS) * g_ref[...]
    e_ref[...] = e.astype(e_ref.dtype)


def _ple(p3d, ple_proj, ple_norm, layer, tm=256):
    m = p3d.shape[1]
    return pl.pallas_call(
        _ple_body,
        grid=(m // tm,),
        in_specs=[
            pl.BlockSpec((None, tm, PLE_DIM), lambda i: (layer, i, 0)),
            pl.BlockSpec((None, PLE_DIM, D_MODEL), lambda i: (layer, 0, 0)),
            pl.BlockSpec((None, 1, D_MODEL), lambda i: (layer, 0, 0)),
        ],
        out_specs=pl.BlockSpec((tm, D_MODEL), lambda i: (i, 0)),
        out_shape=jax.ShapeDtypeStruct((m, D_MODEL), BF16),
        scratch_shapes=[pltpu.VMEM((PLE_DIM, D_MODEL), BF16)],
        compiler_params=_params("arbitrary"),
        name="ple_embed",
    )(p3d, ple_proj, ple_norm)


def _plegate_body(a_ref, w_ref, x_ref, t_ref, ssq_ref, g_ref, e_ref, o_ref, wb_ref):
    @pl.when(pl.program_id(1) == 0)
    def _():
        _cast_weight(w_ref, wb_ref)

    acc = jnp.dot(a_ref[...], wb_ref[...], preferred_element_type=F32)
    x1 = _post_residual(x_ref[...], t_ref[...], ssq_ref[:, 0:1], g_ref[...])
    o_ref[...] = x1 + _sigmoid(acc) * e_ref[...].astype(F32)


def _plegate(x1b, ple_gate, x2d, t, ssq, norm_post, e, layer):
    m, k = x1b.shape
    n = ple_gate.shape[-1]
    tile = pl.BlockSpec((TM, TN), lambda j, i: (i, j))
    return pl.pallas_call(
        _plegate_body,
        grid=(n // TN, m // TM),
        in_specs=[
            pl.BlockSpec((TM, k), lambda j, i: (i, 0)),
            pl.BlockSpec((None, k, TN), lambda j, i: (layer, 0, j)),
            tile, tile,
            pl.BlockSpec((TM, LANES), lambda j, i: (i, 0)),
            pl.BlockSpec((None, 1, TN), lambda j, i: (layer, 0, j)),
            tile,
        ],
        out_specs=tile,
        out_shape=jax.ShapeDtypeStruct((m, n), F32),
        scratch_shapes=[pltpu.VMEM((k, TN), BF16)],
        compiler_params=_params("arbitrary", "arbitrary"),
        name="ple_gate",
    )(x1b, ple_gate, x2d, t, ssq, norm_post, e)


def kernel(x, p, positions, norm_pre, w_in, gmlp_ln_g, gmlp_ln_b, gmlp_ws, gmlp_bs, attn_sinks,
           mlstm_ib, mlstm_fb, mlstm_norm_g, w_branch, w_out, norm_post, ple_proj, ple_norm,
           ple_gate):
    batch, seq, d = x.shape
    depth = w_in.shape[0]
    m = batch * seq
    assert d == D_MODEL and w_in.shape[-1] == N_IN and seq % 512 == 0 and m % TM == 0

    inv_freq = ROPE_THETA ** (-jnp.arange(0, ROT_DIM, 2, dtype=F32) / ROT_DIM)
    lane_d = jnp.arange(LANES) % SWA_HEAD_DIM
    inv_lane = jnp.where(lane_d < ROT_DIM, inv_freq[lane_d % (ROT_DIM // 2)], 0.0).reshape(1, LANES)
    posf = positions.astype(F32).reshape(m, 1)
    p3d = p.reshape(depth, m, PLE_DIM)
    bst = jnp.swapaxes(gmlp_bs, 1, 2)
    gate_bias = jnp.pad(jnp.concatenate([mlstm_ib, mlstm_fb], axis=1).astype(F32),
                        ((0, 0), (0, LANES - GATE_COLS)))

    def rows3(a):
        return a.reshape(depth, 1, a.shape[-1])

    norm_pre, norm_post, ple_norm = rows3(norm_pre), rows3(norm_post), rows3(ple_norm)
    gmlp_ln_g, gmlp_ln_b = rows3(gmlp_ln_g), rows3(gmlp_ln_b)
    attn_sinks, mlstm_norm_g = rows3(attn_sinks), rows3(mlstm_norm_g)

    w_t = jnp.swapaxes(w_in, 1, 2)

    xc = x.reshape(m, d)
    for l in range(depth):
        h, gcol, gtr = _prenorm(xc, norm_pre, w_t, gate_bias[l:l + 1], l)

        p_uv = _proj(h, w_t, l, ("a_u", "a_v"), "gelu")
        p_az = _proj(h, w_t, l, ("a_z",), "silu")
        p_bqkv = _proj(h, w_t, l, ("b_q", "b_k", "b_v"), "linear")
        p_bz = _proj(h, w_t, l, ("b_z",), "silu")
        p_cqkv = _proj(h, w_t, l, ("c_q", "c_k", "c_v"), "linear")
        p_co = _proj(h, w_t, l, ("c_o",), "sigmoid")
        p_cz = _proj(h, w_t, l, ("c_z",), "silu")
        p_gates = _proj(h, w_t, l, ("gates",), "sigmoid")

        ya = _gmlp(p_uv, p_az, gmlp_ln_g, gmlp_ln_b, gmlp_ws, bst, l)
        yb = _swa(p_bqkv, p_bz, posf, inv_lane, attn_sinks, l, batch, seq)
        yc = _mlstm(p_cqkv, p_co, p_cz, gcol, gtr, mlstm_norm_g, l, batch, seq)

        mixed = _merge(ya, yb, yc, w_branch, p_gates, l)
        t, ssq = _outproj(mixed, w_out, l)
        x1b = _postnorm(xc, t, ssq, norm_post, l)
        e = _ple(p3d, ple_proj, ple_norm, l)
        xc = _plegate(x1b, ple_gate, xc, t, ssq, norm_post, e, l)
    return xc.reshape(batch, seq, d)
```

```python
import functools

import jax
import jax.numpy as jnp
from jax import lax
from jax.experimental import pallas as pl
from jax.experimental.pallas import tpu as pltpu

F32 = jnp.float32
BF16 = jnp.bfloat16

D_MODEL = 4096
PLE_DIM = 256
N_BRANCHES = 3
BRANCH_WIDTH = D_MODEL // 2
GMLP_CHUNK = 128
GMLP_GROUPS = 8
GMLP_GROUP_WIDTH = BRANCH_WIDTH // GMLP_GROUPS
SWA_HEAD_DIM = 64
SWA_Q_HEADS = BRANCH_WIDTH // SWA_HEAD_DIM
SWA_KV_HEADS = 4
SWA_GROUP = SWA_Q_HEADS // SWA_KV_HEADS
SWA_BLOCK = 128
SWA_KV_WIDTH = SWA_KV_HEADS * SWA_HEAD_DIM
ROT_DIM = SWA_HEAD_DIM // 4
ROPE_THETA = 500000.0
MLSTM_HEADS = 4
MLSTM_V_DIM = BRANCH_WIDTH // MLSTM_HEADS
MLSTM_QK_DIM = MLSTM_V_DIM // 2
MLSTM_CHUNK = 64
GATE_SOFTCAP = 15.0
NORM_EPS = 1e-6
NEG_INF = -1e30

LANES = 128
GATE_ROWS = 16
VMEM_LIMIT = 56 * 1024 * 1024
TM = 1024
TN = 1024

_SPLIT = (
    ("a_u", BRANCH_WIDTH), ("a_v", BRANCH_WIDTH), ("a_z", BRANCH_WIDTH),
    ("b_q", BRANCH_WIDTH), ("b_k", SWA_KV_WIDTH), ("b_v", SWA_KV_WIDTH), ("b_z", BRANCH_WIDTH),
    ("c_q", MLSTM_HEADS * MLSTM_QK_DIM), ("c_k", MLSTM_HEADS * MLSTM_QK_DIM),
    ("c_v", BRANCH_WIDTH), ("c_i", MLSTM_HEADS), ("c_f", MLSTM_HEADS),
    ("c_o", BRANCH_WIDTH), ("c_z", BRANCH_WIDTH), ("gates", N_BRANCHES * D_MODEL),
)
_SRC = {}
_off = 0
for _name, _size in _SPLIT:
    _SRC[_name] = (_off, _size)
    _off += _size
N_IN = _off
GATE_COLS = 2 * MLSTM_HEADS
GATE_START = _SRC["c_i"][0]


def _params(*sem):
    return pltpu.CompilerParams(dimension_semantics=sem, vmem_limit_bytes=VMEM_LIMIT)


def _gelu(x):
    return 0.5 * x * (1.0 + lax.erf(x * (0.5 ** 0.5)))


def _sigmoid(x):
    return jax.nn.sigmoid(x)


def _silu(x):
    return x * jax.nn.sigmoid(x)


def _identity(x):
    return x


_ACTS = {"gelu": _gelu, "silu": _silu, "sigmoid": _sigmoid, "linear": _identity}


def _softcap(z):
    return GATE_SOFTCAP * jnp.tanh(z / GATE_SOFTCAP)


def _log_sigmoid(x):
    return -(jnp.maximum(-x, 0.0) + jnp.log1p(jnp.exp(-jnp.abs(x))))


def _stream_steps(cast, compute, wb_even, wb_odd):
    jj = pl.program_id(0)

    @pl.when(jj == 0)
    def _():
        cast(wb_even)

    @pl.when((jj > 0) & (jj % 2 == 1))
    def _():
        cast(wb_odd)
        compute(wb_even)

    @pl.when((jj > 0) & (jj % 2 == 0))
    def _():
        cast(wb_even)
        compute(wb_odd)


def _row_of(jj, i):
    return jnp.where(jj > 0, i, 0)


def _col_of(jj):
    return jnp.maximum(jj - 1, 0)


def _cast_col_of(jj, n_col):
    return jnp.minimum(jj, n_col - 1)


def _prenorm_body(x_ref, g_ref, wif_ref, brow_ref, h_ref, gc_ref, gt_ref, wb_ref):
    @pl.when(pl.program_id(0) == 0)
    def _():
        wb_ref[...] = wif_ref[...].astype(BF16)

    x = x_ref[...]
    y = x * lax.rsqrt(jnp.mean(x * x, axis=-1, keepdims=True) + NORM_EPS) * g_ref[...]
    hb = y.astype(BF16)
    h_ref[...] = hb
    pre = lax.dot_general(hb, wb_ref[...], (((1,), (1,)), ((), ())),
                          preferred_element_type=F32) + brow_ref[...]
    sc = _softcap(pre)
    lane = lax.broadcasted_iota(jnp.int32, sc.shape, 1)
    gc = jnp.where(lane >= MLSTM_HEADS, _log_sigmoid(sc), sc)
    gc_ref[...] = gc
    gt_ref[...] = gc.T[:GATE_ROWS, :]


def _prenorm(x2d, norm_pre, w_t, brow, layer, tm=256):
    m = x2d.shape[0]
    assert GATE_START % LANES == 0
    return pl.pallas_call(
        _prenorm_body,
        grid=(m // tm,),
        in_specs=[
            pl.BlockSpec((tm, D_MODEL), lambda i: (i, 0)),
            pl.BlockSpec((None, 1, D_MODEL), lambda i: (layer, 0, 0)),
            pl.BlockSpec((None, LANES, D_MODEL), lambda i: (layer, GATE_START // LANES, 0)),
            pl.BlockSpec((1, LANES), lambda i: (0, 0)),
        ],
        out_specs=[
            pl.BlockSpec((tm, D_MODEL), lambda i: (i, 0)),
            pl.BlockSpec((tm, LANES), lambda i: (i, 0)),
            pl.BlockSpec((GATE_ROWS, tm), lambda i: (0, i)),
        ],
        out_shape=[
            jax.ShapeDtypeStruct((m, D_MODEL), BF16),
            jax.ShapeDtypeStruct((m, LANES), F32),
            jax.ShapeDtypeStruct((GATE_ROWS, m), F32),
        ],
        scratch_shapes=[pltpu.VMEM((LANES, D_MODEL), BF16)],
        compiler_params=_params("arbitrary"),
        name="prenorm",
    )(x2d, norm_pre, w_t, brow)


_NT = (((1,), (1,)), ((), ()))


def _proj_body(act, h_ref, wp_ref, o_ref, wb_even, wb_odd):
    piece = wp_ref.shape[0]
    rows = pl.ds(pl.multiple_of(pl.program_id(1) * piece, piece), piece)

    def cast(dst):
        dst[rows, :] = wp_ref[...].astype(BF16)

    def compute(src):
        acc = lax.dot_general(h_ref[...], src[...], _NT, preferred_element_type=F32)
        o_ref[...] = act(acc).astype(o_ref.dtype)

    _stream_steps(cast, compute, wb_even, wb_odd)


def _proj(h, w_t, layer, names, act_name, tn):
    m, k = h.shape
    start = _SRC[names[0]][0]
    width = sum(_SRC[n][1] for n in names)
    n_col, n_row = width // tn, m // TM
    piece = tn // n_row
    assert start % 8 == 0 and width % tn == 0 and tn % n_row == 0 and piece % 16 == 0
    return pl.pallas_call(
        functools.partial(_proj_body, _ACTS[act_name]),
        grid=(n_col + 1, n_row),
        in_specs=[
            pl.BlockSpec((TM, k), lambda jj, i: (_row_of(jj, i), 0)),
            pl.BlockSpec((None, pl.Element(piece), pl.Element(k)), lambda jj, i: (
                layer,
                pl.multiple_of(start + _cast_col_of(jj, n_col) * tn + i * piece, 8), 0)),
        ],
        out_specs=pl.BlockSpec((TM, tn), lambda jj, i: (_row_of(jj, i), _col_of(jj))),
        out_shape=jax.ShapeDtypeStruct((m, width), BF16),
        scratch_shapes=[pltpu.VMEM((tn, k), BF16), pltpu.VMEM((tn, k), BF16)],
        compiler_params=_params("arbitrary", "arbitrary"),
        name="proj_" + names[0],
    )(h, w_t)


def _gmlp_body(chunks, u_ref, v_ref, z_ref, lg_ref, lb_ref, ws_ref, bst_ref, o_ref):
    r = lax.broadcasted_iota(jnp.int32, (GMLP_CHUNK, GMLP_CHUNK), 0)
    c = lax.broadcasted_iota(jnp.int32, (GMLP_CHUNK, GMLP_CHUNK), 1)
    tri = r >= c
    wmix = [jnp.where(tri, ws_ref[g], 0.0).astype(BF16) for g in range(GMLP_GROUPS)]
    for ch in range(chunks):
        rows = pl.ds(ch * GMLP_CHUNK, GMLP_CHUNK)
        v = v_ref[rows, :].astype(F32)
        mu = jnp.mean(v, axis=-1, keepdims=True)
        d = v - mu
        var = jnp.mean(d * d, axis=-1, keepdims=True)
        vn = (d * lax.rsqrt(var + NORM_EPS) * lg_ref[...] + lb_ref[...]).astype(BF16)
        for g in range(GMLP_GROUPS):
            cols = pl.ds(g * GMLP_GROUP_WIDTH, GMLP_GROUP_WIDTH)
            mixed = jnp.dot(wmix[g], vn[:, g * GMLP_GROUP_WIDTH:(g + 1) * GMLP_GROUP_WIDTH],
                            preferred_element_type=F32) + bst_ref[:, g:g + 1]
            y = u_ref[rows, cols].astype(F32) * mixed * z_ref[rows, cols].astype(F32)
            o_ref[rows, cols] = y.astype(o_ref.dtype)


def _gmlp(p_uv, p_z, ln_g, ln_b, ws, bst, layer, ts=512):
    m = p_uv.shape[0]
    w = BRANCH_WIDTH
    return pl.pallas_call(
        functools.partial(_gmlp_body, ts // GMLP_CHUNK),
        grid=(m // ts,),
        in_specs=[
            pl.BlockSpec((ts, w), lambda i: (i, 0)),
            pl.BlockSpec((ts, w), lambda i: (i, 1)),
            pl.BlockSpec((ts, w), lambda i: (i, 0)),
            pl.BlockSpec((None, 1, w), lambda i: (layer, 0, 0)),
            pl.BlockSpec((None, 1, w), lambda i: (layer, 0, 0)),
            pl.BlockSpec((None, GMLP_GROUPS, GMLP_CHUNK, GMLP_CHUNK), lambda i: (layer, 0, 0, 0)),
            pl.BlockSpec((None, GMLP_CHUNK, GMLP_GROUPS), lambda i: (layer, 0, 0)),
        ],
        out_specs=pl.BlockSpec((ts, w), lambda i: (i, 0)),
        out_shape=jax.ShapeDtypeStruct((m, w), BF16),
        compiler_params=_params("arbitrary"),
        name="gmlp",
    )(p_uv, p_uv, p_z, ln_g, ln_b, ws, bst)


def _rope_tables(pos_ref, inv_ref):
    ang = pos_ref[...] * inv_ref[...]
    d = lax.broadcasted_iota(jnp.int32, ang.shape, 1) % SWA_HEAD_DIM
    half = ROT_DIM // 2
    cosf = jnp.where(d < ROT_DIM, jnp.cos(ang), 1.0)
    sinf = jnp.sin(ang)
    s_lo = jnp.where(d < half, -sinf, 0.0)
    s_hi = jnp.where((d >= half) & (d < ROT_DIM), sinf, 0.0)
    return cosf, s_lo, s_hi


def _rope_tile(t, tables):
    cosf, s_lo, s_hi = tables
    half = ROT_DIM // 2
    up = pltpu.roll(t, LANES - half, 1)
    dn = pltpu.roll(t, half, 1)
    return t * cosf + up * s_lo + dn * s_hi


def _swa_body(q_ref, kc_ref, kp_ref, vc_ref, vp_ref, z_ref, pc_ref, pp_ref, inv_ref, sink_ref,
              o_ref):
    n = pl.program_id(1)
    tab_c = _rope_tables(pc_ref, inv_ref)
    tab_p = _rope_tables(pp_ref, inv_ref)
    heads_per_tile = LANES // SWA_HEAD_DIM
    tiles_per_group = SWA_GROUP // heads_per_tile
    rows = tiles_per_group * SWA_BLOCK

    lane_half = lax.broadcasted_iota(jnp.int32, (2 * SWA_BLOCK, LANES), 1) // SWA_HEAD_DIM
    qi = lax.broadcasted_iota(jnp.int32, (rows, 2 * SWA_BLOCK), 0) % SWA_BLOCK
    kj = lax.broadcasted_iota(jnp.int32, (rows, 2 * SWA_BLOCK), 1)
    diff = qi + SWA_BLOCK - kj
    mask = (diff >= 0) & (diff < SWA_BLOCK) & ((n > 0) | (kj >= SWA_BLOCK))

    scale = SWA_HEAD_DIM ** -0.5
    for kt in range(SWA_KV_WIDTH // LANES):
        cols = pl.ds(kt * LANES, LANES)
        k_band = jnp.concatenate(
            [_rope_tile(kp_ref[:, cols].astype(F32), tab_p),
             _rope_tile(kc_ref[:, cols].astype(F32), tab_c)], axis=0)
        v_band = jnp.concatenate([vp_ref[:, cols], vc_ref[:, cols]], axis=0).astype(F32)
        for hh in range(heads_per_tile):
            kvh = kt * heads_per_tile + hh
            k_keep = jnp.where(lane_half == hh, k_band, 0.0)
            v_keep = jnp.where(lane_half == hh, v_band, 0.0)
            k_swap = pltpu.roll(k_keep, SWA_HEAD_DIM, 1)
            v_swap = pltpu.roll(v_keep, SWA_HEAD_DIM, 1)
            k_at = [k_keep.astype(BF16), k_swap.astype(BF16)]
            v_at = [v_keep.astype(BF16), v_swap.astype(BF16)]
            if hh == 1:
                k_at.reverse()
                v_at.reverse()
            tile0 = kvh * tiles_per_group
            q_stack = jnp.concatenate(
                [(_rope_tile(q_ref[:, pl.ds((tile0 + t) * LANES, LANES)].astype(F32), tab_c)
                  * scale).astype(BF16) for t in range(tiles_per_group)], axis=0)
            o_stack = jnp.zeros((rows, LANES), F32)
            for qh in range(heads_per_tile):
                sink = jnp.concatenate(
                    [jnp.broadcast_to(
                        sink_ref[:, (tile0 + t) * heads_per_tile + qh:
                                 (tile0 + t) * heads_per_tile + qh + 1], (SWA_BLOCK, 1))
                     for t in range(tiles_per_group)], axis=0)
                s = lax.dot_general(q_stack, k_at[qh], (((1,), (1,)), ((), ())),
                                    preferred_element_type=F32)
                s = jnp.where(mask, s, NEG_INF)
                mx = jnp.maximum(jnp.max(s, axis=-1, keepdims=True), sink)
                p = jnp.exp(s - mx)
                den = jnp.sum(p, axis=-1, keepdims=True) + jnp.exp(sink - mx)
                pv = jnp.dot(p.astype(BF16), v_at[qh], preferred_element_type=F32)
                o_stack = o_stack + pv / den
            for t in range(tiles_per_group):
                qcols = pl.ds((tile0 + t) * LANES, LANES)
                o_t = o_stack[t * SWA_BLOCK:(t + 1) * SWA_BLOCK, :]
                o_ref[:, qcols] = (o_t * z_ref[:, qcols].astype(F32)).astype(o_ref.dtype)


def _swa(p_qkv, p_z, posf, inv_lane, sinks, layer, batch, seq):
    m = p_qkv.shape[0]
    nb = seq // SWA_BLOCK
    w = BRANCH_WIDTH
    kblk = w // SWA_KV_WIDTH
    vblk = kblk + 1

    def cur(b, n):
        return b * nb + n

    def prev(b, n):
        return b * nb + jnp.maximum(n - 1, 0)

    return pl.pallas_call(
        _swa_body,
        grid=(batch, nb),
        in_specs=[
            pl.BlockSpec((SWA_BLOCK, w), lambda b, n: (cur(b, n), 0)),
            pl.BlockSpec((SWA_BLOCK, SWA_KV_WIDTH), lambda b, n: (cur(b, n), kblk)),
            pl.BlockSpec((SWA_BLOCK, SWA_KV_WIDTH), lambda b, n: (prev(b, n), kblk)),
            pl.BlockSpec((SWA_BLOCK, SWA_KV_WIDTH), lambda b, n: (cur(b, n), vblk)),
            pl.BlockSpec((SWA_BLOCK, SWA_KV_WIDTH), lambda b, n: (prev(b, n), vblk)),
            pl.BlockSpec((SWA_BLOCK, w), lambda b, n: (cur(b, n), 0)),
            pl.BlockSpec((SWA_BLOCK, 1), lambda b, n: (cur(b, n), 0)),
            pl.BlockSpec((SWA_BLOCK, 1), lambda b, n: (prev(b, n), 0)),
            pl.BlockSpec((1, LANES), lambda b, n: (0, 0)),
            pl.BlockSpec((None, 1, SWA_Q_HEADS), lambda b, n: (layer, 0, 0)),
        ],
        out_specs=pl.BlockSpec((SWA_BLOCK, w), lambda b, n: (cur(b, n), 0)),
        out_shape=jax.ShapeDtypeStruct((m, w), BF16),
        compiler_params=_params("arbitrary", "arbitrary"),
        name="swa",
    )(p_qkv, p_qkv, p_qkv, p_qkv, p_qkv, p_z, posf, posf, inv_lane, sinks)


def _mlstm_body(chunks, q_ref, k_ref, v_ref, o_ref, z_ref, gc_ref, gt_ref, ng_ref, y_ref,
                c_ref, n_ref, m_ref):
    head = pl.program_id(1)
    step = pl.program_id(2)
    L = MLSTM_CHUNK

    @pl.when(step == 0)
    def _():
        c_ref[...] = jnp.zeros_like(c_ref)
        n_ref[...] = jnp.zeros_like(n_ref)
        m_ref[...] = jnp.zeros_like(m_ref)

    r = lax.broadcasted_iota(jnp.int32, (L, L), 0)
    c = lax.broadcasted_iota(jnp.int32, (L, L), 1)
    causal = r >= c
    lane_g = lax.broadcasted_iota(jnp.int32, (L, LANES), 1)
    row_g = lax.broadcasted_iota(jnp.int32, (GATE_ROWS, L), 0)

    for ch in range(chunks):
        rows = pl.ds(ch * L, L)
        gc = gc_ref[rows, :]
        gt = gt_ref[:, ch * L:(ch + 1) * L]
        i_col = jnp.sum(jnp.where(lane_g == head, gc, 0.0), axis=1, keepdims=True)
        f_col = jnp.sum(jnp.where(lane_g == head + MLSTM_HEADS, gc, 0.0), axis=1, keepdims=True)
        i_row = jnp.sum(jnp.where(row_g == head, gt, 0.0), axis=0, keepdims=True)
        f_row = jnp.sum(jnp.where(row_g == head + MLSTM_HEADS, gt, 0.0), axis=0, keepdims=True)

        b_col = jnp.sum(jnp.where(causal, f_row, 0.0), axis=1, keepdims=True)
        b_row = jnp.sum(jnp.where(r <= c, f_col, 0.0), axis=0, keepdims=True)
        g_tot = jnp.sum(f_row, axis=1, keepdims=True)
        m_prev = m_ref[...]

        log_d = jnp.where(causal, b_col - b_row + i_row, NEG_INF)
        m_inter = b_col + m_prev
        m_t = jnp.maximum(m_inter, jnp.max(log_d, axis=1, keepdims=True))
        dmat = jnp.exp(log_d - m_t)
        a = jnp.exp(m_inter - m_t)

        qf = q_ref[rows, :].astype(F32) * (MLSTM_QK_DIM ** -0.5)
        qs = qf.astype(BF16)
        k = k_ref[rows, :]
        kf = k.astype(F32)
        v = v_ref[rows, :]

        qk = lax.dot_general(qs, k, (((1,), (1,)), ((), ())), preferred_element_type=F32)
        s = qk * dmat
        c_prev = c_ref[...]
        num = jnp.dot(s.astype(BF16), v, preferred_element_type=F32) + \
            a * jnp.dot(qs, c_prev.astype(BF16), preferred_element_type=F32)
        qn = jnp.sum(qf * n_ref[...], axis=1, keepdims=True)
        den = jnp.sum(s, axis=1, keepdims=True) + a * qn
        hout = num / jnp.maximum(jnp.abs(den), jnp.exp(-m_t))

        w_col = g_tot - b_col + i_col
        w_row = g_tot - b_row + i_row
        m_new = jnp.maximum(g_tot + m_prev, jnp.max(w_row, axis=1, keepdims=True))
        wgt = jnp.exp(w_col - m_new)
        decay = jnp.exp(g_tot + m_prev - m_new)
        kw = kf * wgt
        c_ref[...] = decay * c_prev + lax.dot_general(
            kw.astype(BF16), v, (((0,), (0,)), ((), ())), preferred_element_type=F32)
        n_ref[...] = decay * n_ref[...] + jnp.sum(kw, axis=0, keepdims=True)
        m_ref[...] = m_new

        hn = hout * lax.rsqrt(jnp.mean(hout * hout, axis=1, keepdims=True) + NORM_EPS)
        hn = hn * ng_ref[...]
        y = hn * o_ref[rows, :].astype(F32) * z_ref[rows, :].astype(F32)
        y_ref[rows, :] = y.astype(y_ref.dtype)


def _mlstm(p_qkv, p_o, p_z, gcol, gtr, norm_g, layer, batch, seq, ts=256):
    m = p_qkv.shape[0]
    nt = seq // ts
    kblk = MLSTM_HEADS
    vblk = 2 * MLSTM_HEADS * MLSTM_QK_DIM // MLSTM_V_DIM

    def row(b, t):
        return b * nt + t

    return pl.pallas_call(
        functools.partial(_mlstm_body, ts // MLSTM_CHUNK),
        grid=(batch, MLSTM_HEADS, nt),
        in_specs=[
            pl.BlockSpec((ts, MLSTM_QK_DIM), lambda b, h, t: (row(b, t), h)),
            pl.BlockSpec((ts, MLSTM_QK_DIM), lambda b, h, t: (row(b, t), kblk + h)),
            pl.BlockSpec((ts, MLSTM_V_DIM), lambda b, h, t: (row(b, t), vblk + h)),
            pl.BlockSpec((ts, MLSTM_V_DIM), lambda b, h, t: (row(b, t), h)),
            pl.BlockSpec((ts, MLSTM_V_DIM), lambda b, h, t: (row(b, t), h)),
            pl.BlockSpec((ts, LANES), lambda b, h, t: (row(b, t), 0)),
            pl.BlockSpec((GATE_ROWS, ts), lambda b, h, t: (0, row(b, t))),
            pl.BlockSpec((None, 1, MLSTM_V_DIM), lambda b, h, t: (layer, 0, h)),
        ],
        out_specs=pl.BlockSpec((ts, MLSTM_V_DIM), lambda b, h, t: (row(b, t), h)),
        out_shape=jax.ShapeDtypeStruct((m, BRANCH_WIDTH), BF16),
        scratch_shapes=[
            pltpu.VMEM((MLSTM_QK_DIM, MLSTM_V_DIM), F32),
            pltpu.VMEM((1, MLSTM_QK_DIM), F32),
            pltpu.VMEM((1, 1), F32),
        ],
        compiler_params=_params("arbitrary", "arbitrary", "arbitrary"),
        name="mlstm",
    )(p_qkv, p_qkv, p_qkv, p_o, p_z, gcol, gtr, norm_g)


def _merge_body(ya_ref, yb_ref, yc_ref, wp_ref, g0_ref, g1_ref, g2_ref, o_ref, wb_even, wb_odd):
    piece = wp_ref.shape[1]
    rows = pl.ds(pl.multiple_of(pl.program_id(1) * piece, piece), piece)

    def cast(dst):
        dst[:, rows, :] = wp_ref[...].astype(BF16)

    def compute(src):
        acc = g0_ref[...].astype(F32) * jnp.dot(ya_ref[...], src[0], preferred_element_type=F32)
        acc = acc + g1_ref[...].astype(F32) * jnp.dot(yb_ref[...], src[1],
                                                      preferred_element_type=F32)
        acc = acc + g2_ref[...].astype(F32) * jnp.dot(yc_ref[...], src[2],
                                                      preferred_element_type=F32)
        o_ref[...] = acc.astype(o_ref.dtype)

    _stream_steps(cast, compute, wb_even, wb_odd)


def _merge(ya, yb, yc, w_branch, p_gates, layer, tm=512, tn=1024):
    m = ya.shape[0]
    w = BRANCH_WIDTH
    n_col, n_row = D_MODEL // tn, m // tm
    piece = w // n_row
    yspec = pl.BlockSpec((tm, w), lambda jj, i: (_row_of(jj, i), 0))

    def gate_spec(branch):
        return pl.BlockSpec(
            (tm, tn), lambda jj, i: (_row_of(jj, i), branch * n_col + _col_of(jj)))

    wb = pltpu.VMEM((N_BRANCHES, w, tn), BF16)
    return pl.pallas_call(
        _merge_body,
        grid=(n_col + 1, n_row),
        in_specs=[
            yspec, yspec, yspec,
            pl.BlockSpec((None, N_BRANCHES, piece, tn),
                         lambda jj, i: (layer, 0, i, _cast_col_of(jj, n_col))),
            gate_spec(0), gate_spec(1), gate_spec(2),
        ],
        out_specs=pl.BlockSpec((tm, tn), lambda jj, i: (_row_of(jj, i), _col_of(jj))),
        out_shape=jax.ShapeDtypeStruct((m, D_MODEL), BF16),
        scratch_shapes=[wb, wb],
        compiler_params=_params("arbitrary", "arbitrary"),
        name="merge",
    )(ya, yb, yc, w_branch, p_gates, p_gates, p_gates)


def _outproj_body(a_ref, wp_ref, t_ref, ssq_ref, wb_even, wb_odd):
    jj = pl.program_id(0)
    i = pl.program_id(1)
    piece = wp_ref.shape[0]
    rows = pl.ds(pl.multiple_of(i * piece, piece), piece)

    def cast(dst):
        dst[rows, :] = wp_ref[...].astype(BF16)

    def compute(src):
        acc = jnp.dot(a_ref[...], src[...], preferred_element_type=F32)
        t_ref[...] = acc.astype(t_ref.dtype)
        out_rows = pl.ds(pl.multiple_of(i * TM, TM), TM)
        part = jnp.broadcast_to(jnp.sum(acc * acc, axis=-1, keepdims=True), (TM, LANES))

        @pl.when(jj == 1)
        def _():
            ssq_ref[out_rows, :] = part

        @pl.when(jj > 1)
        def _():
            ssq_ref[out_rows, :] += part

    _stream_steps(cast, compute, wb_even, wb_odd)


def _outproj(a, w_out, layer, tn=1024):
    m, k = a.shape
    n = w_out.shape[-1]
    n_col, n_row = n // tn, m // TM
    piece = k // n_row
    return pl.pallas_call(
        _outproj_body,
        grid=(n_col + 1, n_row),
        in_specs=[
            pl.BlockSpec((TM, k), lambda jj, i: (_row_of(jj, i), 0)),
            pl.BlockSpec((None, piece, tn), lambda jj, i: (layer, i, _cast_col_of(jj, n_col))),
        ],
        out_specs=[
            pl.BlockSpec((TM, tn), lambda jj, i: (_row_of(jj, i), _col_of(jj))),
            pl.BlockSpec((m, LANES), lambda jj, i: (0, 0)),
        ],
        out_shape=[
            jax.ShapeDtypeStruct((m, n), BF16),
            jax.ShapeDtypeStruct((m, LANES), F32),
        ],
        scratch_shapes=[pltpu.VMEM((k, tn), BF16), pltpu.VMEM((k, tn), BF16)],
        compiler_params=_params("arbitrary", "arbitrary"),
        name="outproj",
    )(a, w_out)


def _post_residual(x, t, ssq, g):
    rs = lax.rsqrt(ssq * (1.0 / D_MODEL) + NORM_EPS)
    return x + t.astype(F32) * rs * g


def _postnorm_body(x_ref, t_ref, ssq_ref, g_ref, x1b_ref):
    x1b_ref[...] = _post_residual(x_ref[...], t_ref[...], ssq_ref[:, 0:1], g_ref[...]).astype(BF16)


def _postnorm(x2d, t, ssq, norm_post, layer, tm=256):
    m = x2d.shape[0]
    return pl.pallas_call(
        _postnorm_body,
        grid=(m // tm,),
        in_specs=[
            pl.BlockSpec((tm, D_MODEL), lambda i: (i, 0)),
            pl.BlockSpec((tm, D_MODEL), lambda i: (i, 0)),
            pl.BlockSpec((tm, LANES), lambda i: (i, 0)),
            pl.BlockSpec((None, 1, D_MODEL), lambda i: (layer, 0, 0)),
        ],
        out_specs=pl.BlockSpec((tm, D_MODEL), lambda i: (i, 0)),
        out_shape=jax.ShapeDtypeStruct((m, D_MODEL), BF16),
        compiler_params=_params("arbitrary"),
        name="postnorm",
    )(x2d, t, ssq, norm_post)


def _ple_body(p_ref, w_ref, g_ref, e_ref, wb_ref):
    @pl.when(pl.program_id(0) == 0)
    def _():
        wb_ref[...] = w_ref[...].astype(BF16)

    t = jnp.dot(p_ref[...].astype(BF16), wb_ref[...], preferred_element_type=F32)
    e = t * lax.rsqrt(jnp.mean(t * t, axis=-1, keepdims=True) + NORM_EPS) * g_ref[...]
    e_ref[...] = e.astype(e_ref.dtype)


def _ple(p3d, ple_proj, ple_norm, layer, tm=256):
    m = p3d.shape[1]
    return pl.pallas_call(
        _ple_body,
        grid=(m // tm,),
        in_specs=[
            pl.BlockSpec((None, tm, PLE_DIM), lambda i: (layer, i, 0)),
            pl.BlockSpec((None, PLE_DIM, D_MODEL), lambda i: (layer, 0, 0)),
            pl.BlockSpec((None, 1, D_MODEL), lambda i: (layer, 0, 0)),
        ],
        out_specs=pl.BlockSpec((tm, D_MODEL), lambda i: (i, 0)),
        out_shape=jax.ShapeDtypeStruct((m, D_MODEL), BF16),
        scratch_shapes=[pltpu.VMEM((PLE_DIM, D_MODEL), BF16)],
        compiler_params=_params("arbitrary"),
        name="ple_embed",
    )(p3d, ple_proj, ple_norm)


def _plegate_body(a_ref, wp_ref, x_ref, t_ref, ssq_ref, g_ref, e_ref, o_ref, wb_even, wb_odd):
    piece = wp_ref.shape[0]
    rows = pl.ds(pl.multiple_of(pl.program_id(1) * piece, piece), piece)

    def cast(dst):
        dst[rows, :] = wp_ref[...].astype(BF16)

    def compute(src):
        acc = jnp.dot(a_ref[...], src[...], preferred_element_type=F32)
        x1 = _post_residual(x_ref[...], t_ref[...], ssq_ref[:, 0:1], g_ref[...])
        o_ref[...] = x1 + _sigmoid(acc) * e_ref[...].astype(F32)

    _stream_steps(cast, compute, wb_even, wb_odd)


def _plegate(x1b, ple_gate, x2d, t, ssq, norm_post, e, layer, tn=512):
    m, k = x1b.shape
    n = ple_gate.shape[-1]
    n_col, n_row = n // tn, m // TM
    piece = k // n_row
    tile = pl.BlockSpec((TM, tn), lambda jj, i: (_row_of(jj, i), _col_of(jj)))
    return pl.pallas_call(
        _plegate_body,
        grid=(n_col + 1, n_row),
        in_specs=[
            pl.BlockSpec((TM, k), lambda jj, i: (_row_of(jj, i), 0)),
            pl.BlockSpec((None, piece, tn), lambda jj, i: (layer, i, _cast_col_of(jj, n_col))),
            tile, tile,
            pl.BlockSpec((TM, LANES), lambda jj, i: (_row_of(jj, i), 0)),
            pl.BlockSpec((None, 1, tn), lambda jj, i: (layer, 0, _col_of(jj))),
            tile,
        ],
        out_specs=tile,
        out_shape=jax.ShapeDtypeStruct((m, n), F32),
        scratch_shapes=[pltpu.VMEM((k, tn), BF16), pltpu.VMEM((k, tn), BF16)],
        compiler_params=_params("arbitrary", "arbitrary"),
        name="ple_gate",
    )(x1b, ple_gate, x2d, t, ssq, norm_post, e)


def kernel(x, p, positions, norm_pre, w_in, gmlp_ln_g, gmlp_ln_b, gmlp_ws, gmlp_bs, attn_sinks,
           mlstm_ib, mlstm_fb, mlstm_norm_g, w_branch, w_out, norm_post, ple_proj, ple_norm,
           ple_gate):
    batch, seq, d = x.shape
    depth = w_in.shape[0]
    m = batch * seq
    assert d == D_MODEL and w_in.shape[-1] == N_IN and seq % 512 == 0 and m % TM == 0

    inv_freq = ROPE_THETA ** (-jnp.arange(0, ROT_DIM, 2, dtype=F32) / ROT_DIM)
    lane_d = jnp.arange(LANES) % SWA_HEAD_DIM
    inv_lane = jnp.where(lane_d < ROT_DIM, inv_freq[lane_d % (ROT_DIM // 2)], 0.0).reshape(1, LANES)
    posf = positions.astype(F32).reshape(m, 1)
    p3d = p.reshape(depth, m, PLE_DIM)
    bst = jnp.swapaxes(gmlp_bs, 1, 2)
    gate_bias = jnp.pad(jnp.concatenate([mlstm_ib, mlstm_fb], axis=1).astype(F32),
                        ((0, 0), (0, LANES - GATE_COLS)))

    def rows3(a):
        return a.reshape(depth, 1, a.shape[-1])

    norm_pre, norm_post, ple_norm = rows3(norm_pre), rows3(norm_post), rows3(ple_norm)
    gmlp_ln_g, gmlp_ln_b = rows3(gmlp_ln_g), rows3(gmlp_ln_b)
    attn_sinks, mlstm_norm_g = rows3(attn_sinks), rows3(mlstm_norm_g)

    w_t = jnp.swapaxes(w_in, 1, 2)

    xc = x.reshape(m, d)
    for l in range(depth):
        h, gcol, gtr = _prenorm(xc, norm_pre, w_t, gate_bias[l:l + 1], l)

        p_uv = _proj(h, w_t, l, ("a_u", "a_v"), "gelu", TN)
        p_az = _proj(h, w_t, l, ("a_z",), "silu", TN)
        p_bqkv = _proj(h, w_t, l, ("b_q", "b_k", "b_v"), "linear", TN // 2)
        p_bz = _proj(h, w_t, l, ("b_z",), "silu", TN)
        p_cqkv = _proj(h, w_t, l, ("c_q", "c_k", "c_v"), "linear", TN)
        p_co = _proj(h, w_t, l, ("c_o",), "sigmoid", TN)
        p_cz = _proj(h, w_t, l, ("c_z",), "silu", TN)
        p_gates = _proj(h, w_t, l, ("gates",), "sigmoid", TN)

        ya = _gmlp(p_uv, p_az, gmlp_ln_g, gmlp_ln_b, gmlp_ws, bst, l)
        yb = _swa(p_bqkv, p_bz, posf, inv_lane, attn_sinks, l, batch, seq)
        yc = _mlstm(p_cqkv, p_co, p_cz, gcol, gtr, mlstm_norm_g, l, batch, seq)

        mixed = _merge(ya, yb, yc, w_branch, p_gates, l)
        t, ssq = _outproj(mixed, w_out, l)
        x1b = _postnorm(xc, t, ssq, norm_post, l)
        e = _ple(p3d, ple_proj, ple_norm, l)
        xc = _plegate(x1b, ple_gate, xc, t, ssq, norm_post, e, l)
    return xc.reshape(batch, seq, d)
```

```python
import functools

import jax
import jax.numpy as jnp
from jax import lax
from jax.experimental import pallas as pl
from jax.experimental.pallas import tpu as pltpu

F32 = jnp.float32
BF16 = jnp.bfloat16

D_MODEL = 4096
PLE_DIM = 256
N_BRANCHES = 3
BRANCH_WIDTH = D_MODEL // 2
GMLP_CHUNK = 128
GMLP_GROUPS = 8
GMLP_GROUP_WIDTH = BRANCH_WIDTH // GMLP_GROUPS
SWA_HEAD_DIM = 64
SWA_Q_HEADS = BRANCH_WIDTH // SWA_HEAD_DIM
SWA_KV_HEADS = 4
SWA_GROUP = SWA_Q_HEADS // SWA_KV_HEADS
SWA_BLOCK = 128
SWA_KV_WIDTH = SWA_KV_HEADS * SWA_HEAD_DIM
ROT_DIM = SWA_HEAD_DIM // 4
ROPE_THETA = 500000.0
MLSTM_HEADS = 4
MLSTM_V_DIM = BRANCH_WIDTH // MLSTM_HEADS
MLSTM_QK_DIM = MLSTM_V_DIM // 2
MLSTM_CHUNK = 64
GATE_SOFTCAP = 15.0
NORM_EPS = 1e-6
NEG_INF = -1e30

LANES = 128
GATE_ROWS = 16
VMEM_LIMIT = 56 * 1024 * 1024
TM = 1024
TN = 1024

_SPLIT = (
    ("a_u", BRANCH_WIDTH), ("a_v", BRANCH_WIDTH), ("a_z", BRANCH_WIDTH),
    ("b_q", BRANCH_WIDTH), ("b_k", SWA_KV_WIDTH), ("b_v", SWA_KV_WIDTH), ("b_z", BRANCH_WIDTH),
    ("c_q", MLSTM_HEADS * MLSTM_QK_DIM), ("c_k", MLSTM_HEADS * MLSTM_QK_DIM),
    ("c_v", BRANCH_WIDTH), ("c_i", MLSTM_HEADS), ("c_f", MLSTM_HEADS),
    ("c_o", BRANCH_WIDTH), ("c_z", BRANCH_WIDTH), ("gates", N_BRANCHES * D_MODEL),
)
_SRC = {}
_off = 0
for _name, _size in _SPLIT:
    _SRC[_name] = (_off, _size)
    _off += _size
N_IN = _off
GATE_COLS = 2 * MLSTM_HEADS
GATE_START = _SRC["c_i"][0]


def _params(*sem):
    return pltpu.CompilerParams(dimension_semantics=sem, vmem_limit_bytes=VMEM_LIMIT)


def _gelu(x):
    return 0.5 * x * (1.0 + lax.erf(x * (0.5 ** 0.5)))


def _sigmoid(x):
    return jax.nn.sigmoid(x)


def _silu(x):
    return x * jax.nn.sigmoid(x)


def _identity(x):
    return x


_ACTS = {"gelu": _gelu, "silu": _silu, "sigmoid": _sigmoid, "linear": _identity}


def _softcap(z):
    return GATE_SOFTCAP * jnp.tanh(z / GATE_SOFTCAP)


def _log_sigmoid(x):
    return -(jnp.maximum(-x, 0.0) + jnp.log1p(jnp.exp(-jnp.abs(x))))


def _stream_steps(cast, compute, wb_even, wb_odd):
    jj = pl.program_id(0)

    @pl.when(jj == 0)
    def _():
        cast(wb_even)

    @pl.when((jj > 0) & (jj % 2 == 1))
    def _():
        cast(wb_odd)
        compute(wb_even)

    @pl.when((jj > 0) & (jj % 2 == 0))
    def _():
        cast(wb_even)
        compute(wb_odd)


def _row_of(jj, i):
    return jnp.where(jj > 0, i, 0)


def _col_of(jj):
    return jnp.maximum(jj - 1, 0)


def _cast_col_of(jj, n_col):
    return jnp.minimum(jj, n_col - 1)


def _prenorm_body(x_ref, g_ref, wif_ref, brow_ref, h_ref, gc_ref, gt_ref, wb_ref):
    @pl.when(pl.program_id(0) == 0)
    def _():
        wb_ref[...] = wif_ref[...].astype(BF16)

    x = x_ref[...]
    y = x * lax.rsqrt(jnp.mean(x * x, axis=-1, keepdims=True) + NORM_EPS) * g_ref[...]
    hb = y.astype(BF16)
    h_ref[...] = hb
    pre = lax.dot_general(hb, wb_ref[...], (((1,), (1,)), ((), ())),
                          preferred_element_type=F32) + brow_ref[...]
    sc = _softcap(pre)
    lane = lax.broadcasted_iota(jnp.int32, sc.shape, 1)
    gc = jnp.where(lane >= MLSTM_HEADS, _log_sigmoid(sc), sc)
    gc_ref[...] = gc
    gt_ref[...] = gc.T[:GATE_ROWS, :]


def _prenorm(x2d, norm_pre, w_t, brow, layer, batch, seq, tm=256):
    m = x2d.shape[0]
    per_seq = seq // tm
    assert GATE_START % LANES == 0 and seq % tm == 0
    return pl.pallas_call(
        _prenorm_body,
        grid=(m // tm,),
        in_specs=[
            pl.BlockSpec((tm, D_MODEL), lambda i: (i, 0)),
            pl.BlockSpec((None, 1, D_MODEL), lambda i: (layer, 0, 0)),
            pl.BlockSpec((None, LANES, D_MODEL), lambda i: (layer, GATE_START // LANES, 0)),
            pl.BlockSpec((1, LANES), lambda i: (0, 0)),
        ],
        out_specs=[
            pl.BlockSpec((tm, D_MODEL), lambda i: (i, 0)),
            pl.BlockSpec((tm, LANES), lambda i: (i, 0)),
            pl.BlockSpec((None, GATE_ROWS, tm), lambda i: (i // per_seq, 0, i % per_seq)),
        ],
        out_shape=[
            jax.ShapeDtypeStruct((m, D_MODEL), BF16),
            jax.ShapeDtypeStruct((m, LANES), F32),
            jax.ShapeDtypeStruct((batch, GATE_ROWS, seq), F32),
        ],
        scratch_shapes=[pltpu.VMEM((LANES, D_MODEL), BF16)],
        compiler_params=_params("arbitrary"),
        name="prenorm",
    )(x2d, norm_pre, w_t, brow)


_NT = (((1,), (1,)), ((), ()))


def _proj_body(act, h_ref, wp_ref, o_ref, wb_even, wb_odd):
    piece = wp_ref.shape[0]
    rows = pl.ds(pl.multiple_of(pl.program_id(1) * piece, piece), piece)

    def cast(dst):
        dst[rows, :] = wp_ref[...].astype(BF16)

    def compute(src):
        acc = lax.dot_general(h_ref[...], src[...], _NT, preferred_element_type=F32)
        o_ref[...] = act(acc).astype(o_ref.dtype)

    _stream_steps(cast, compute, wb_even, wb_odd)


def _proj(h, w_t, layer, names, act_name, tn):
    m, k = h.shape
    start = _SRC[names[0]][0]
    width = sum(_SRC[n][1] for n in names)
    n_col, n_row = width // tn, m // TM
    piece = tn // n_row
    assert start % 8 == 0 and width % tn == 0 and tn % n_row == 0 and piece % 16 == 0
    return pl.pallas_call(
        functools.partial(_proj_body, _ACTS[act_name]),
        grid=(n_col + 1, n_row),
        in_specs=[
            pl.BlockSpec((TM, k), lambda jj, i: (_row_of(jj, i), 0)),
            pl.BlockSpec((None, pl.Element(piece), pl.Element(k)), lambda jj, i: (
                layer,
                pl.multiple_of(start + _cast_col_of(jj, n_col) * tn + i * piece, 8), 0)),
        ],
        out_specs=pl.BlockSpec((TM, tn), lambda jj, i: (_row_of(jj, i), _col_of(jj))),
        out_shape=jax.ShapeDtypeStruct((m, width), BF16),
        scratch_shapes=[pltpu.VMEM((tn, k), BF16), pltpu.VMEM((tn, k), BF16)],
        compiler_params=_params("arbitrary", "arbitrary"),
        name="proj_" + names[0],
    )(h, w_t)


def _gmlp_body(chunks, u_ref, v_ref, z_ref, lg_ref, lb_ref, ws_ref, bst_ref, o_ref):
    r = lax.broadcasted_iota(jnp.int32, (GMLP_CHUNK, GMLP_CHUNK), 0)
    c = lax.broadcasted_iota(jnp.int32, (GMLP_CHUNK, GMLP_CHUNK), 1)
    tri = r >= c
    wmix = [jnp.where(tri, ws_ref[g], 0.0).astype(BF16) for g in range(GMLP_GROUPS)]
    for ch in range(chunks):
        rows = pl.ds(ch * GMLP_CHUNK, GMLP_CHUNK)
        v = v_ref[rows, :].astype(F32)
        mu = jnp.mean(v, axis=-1, keepdims=True)
        d = v - mu
        var = jnp.mean(d * d, axis=-1, keepdims=True)
        vn = (d * lax.rsqrt(var + NORM_EPS) * lg_ref[...] + lb_ref[...]).astype(BF16)
        for g in range(GMLP_GROUPS):
            cols = pl.ds(g * GMLP_GROUP_WIDTH, GMLP_GROUP_WIDTH)
            mixed = jnp.dot(wmix[g], vn[:, g * GMLP_GROUP_WIDTH:(g + 1) * GMLP_GROUP_WIDTH],
                            preferred_element_type=F32) + bst_ref[:, g:g + 1]
            y = u_ref[rows, cols].astype(F32) * mixed * z_ref[rows, cols].astype(F32)
            o_ref[rows, cols] = y.astype(o_ref.dtype)


def _gmlp(p_uv, p_z, ln_g, ln_b, ws, bst, layer, ts=512):
    m = p_uv.shape[0]
    w = BRANCH_WIDTH
    return pl.pallas_call(
        functools.partial(_gmlp_body, ts // GMLP_CHUNK),
        grid=(m // ts,),
        in_specs=[
            pl.BlockSpec((ts, w), lambda i: (i, 0)),
            pl.BlockSpec((ts, w), lambda i: (i, 1)),
            pl.BlockSpec((ts, w), lambda i: (i, 0)),
            pl.BlockSpec((None, 1, w), lambda i: (layer, 0, 0)),
            pl.BlockSpec((None, 1, w), lambda i: (layer, 0, 0)),
            pl.BlockSpec((None, GMLP_GROUPS, GMLP_CHUNK, GMLP_CHUNK), lambda i: (layer, 0, 0, 0)),
            pl.BlockSpec((None, GMLP_CHUNK, GMLP_GROUPS), lambda i: (layer, 0, 0)),
        ],
        out_specs=pl.BlockSpec((ts, w), lambda i: (i, 0)),
        out_shape=jax.ShapeDtypeStruct((m, w), BF16),
        compiler_params=_params("arbitrary"),
        name="gmlp",
    )(p_uv, p_uv, p_z, ln_g, ln_b, ws, bst)


def _rope_tables(pos_ref, inv_ref):
    ang = pos_ref[...] * inv_ref[...]
    d = lax.broadcasted_iota(jnp.int32, ang.shape, 1) % SWA_HEAD_DIM
    half = ROT_DIM // 2
    cosf = jnp.where(d < ROT_DIM, jnp.cos(ang), 1.0)
    sinf = jnp.sin(ang)
    s_lo = jnp.where(d < half, -sinf, 0.0)
    s_hi = jnp.where((d >= half) & (d < ROT_DIM), sinf, 0.0)
    return cosf, s_lo, s_hi


def _rope_tile(t, tables):
    cosf, s_lo, s_hi = tables
    half = ROT_DIM // 2
    up = pltpu.roll(t, LANES - half, 1)
    dn = pltpu.roll(t, half, 1)
    return t * cosf + up * s_lo + dn * s_hi


def _rope_body(pos_ref, inv_ref, o_ref):
    cosf, s_lo, s_hi = _rope_tables(pos_ref, inv_ref)
    o_ref[0] = cosf
    o_ref[1] = s_lo
    o_ref[2] = s_hi


def _rope_call(posf, inv_lane, tm=512):
    m = posf.shape[0]
    return pl.pallas_call(
        _rope_body,
        grid=(m // tm,),
        in_specs=[pl.BlockSpec((tm, 1), lambda i: (i, 0)),
                  pl.BlockSpec((1, LANES), lambda i: (0, 0))],
        out_specs=pl.BlockSpec((3, tm, LANES), lambda i: (0, i, 0)),
        out_shape=jax.ShapeDtypeStruct((3, m, LANES), F32),
        compiler_params=_params("arbitrary"),
        name="rope_tables",
    )(posf, inv_lane)


def _swa_body(layer, q_ref, kc_ref, kp_ref, vc_ref, vp_ref, z_ref, tc_ref, tp_ref, sink_ref, o_ref):
    n = pl.program_id(1)
    tab_c = (tc_ref[0], tc_ref[1], tc_ref[2])
    tab_p = (tp_ref[0], tp_ref[1], tp_ref[2])
    heads_per_tile = LANES // SWA_HEAD_DIM
    tiles_per_group = SWA_GROUP // heads_per_tile
    rows = tiles_per_group * SWA_BLOCK

    lane_half = lax.broadcasted_iota(jnp.int32, (2 * SWA_BLOCK, LANES), 1) // SWA_HEAD_DIM
    key_row = lax.broadcasted_iota(jnp.int32, (2 * SWA_BLOCK, LANES), 0)
    qi = lax.broadcasted_iota(jnp.int32, (rows, LANES), 0) % SWA_BLOCK
    kj = lax.broadcasted_iota(jnp.int32, (rows, LANES), 1)
    mask_prev = (kj > qi) & (n > 0)
    mask_cur = kj <= qi
    sink_slot = lax.broadcasted_iota(jnp.int32, (SWA_BLOCK, LANES), 1) == 0

    scale = SWA_HEAD_DIM ** -0.5
    batches = []
    for kt in range(SWA_KV_WIDTH // LANES):
        cols = pl.ds(kt * LANES, LANES)
        k_band = jnp.concatenate(
            [_rope_tile(kp_ref[:, cols].astype(F32), tab_p),
             _rope_tile(kc_ref[:, cols].astype(F32), tab_c)], axis=0)
        v_band = jnp.concatenate([vp_ref[:, cols], vc_ref[:, cols]], axis=0).astype(F32)
        v_band = jnp.where(key_row == 0, 0.0, v_band)
        for hh in range(heads_per_tile):
            kvh = kt * heads_per_tile + hh
            k_keep = jnp.where(lane_half == hh, k_band, 0.0)
            v_keep = jnp.where(lane_half == hh, v_band, 0.0)
            k_swap = pltpu.roll(k_keep, SWA_HEAD_DIM, 1)
            v_swap = pltpu.roll(v_keep, SWA_HEAD_DIM, 1)
            k_at = [k_keep.astype(BF16), k_swap.astype(BF16)]
            v_at = [v_keep.astype(BF16), v_swap.astype(BF16)]
            if hh == 1:
                k_at.reverse()
                v_at.reverse()
            tile0 = kvh * tiles_per_group
            q_stack = jnp.concatenate(
                [(_rope_tile(q_ref[:, pl.ds((tile0 + t) * LANES, LANES)].astype(F32), tab_c)
                  * scale).astype(BF16) for t in range(tiles_per_group)], axis=0)
            for qh in range(heads_per_tile):
                fill = jnp.concatenate(
                    [jnp.where(sink_slot,
                               sink_ref[layer, (tile0 + t) * heads_per_tile + qh], NEG_INF)
                     for t in range(tiles_per_group)], axis=0)
                batches.append((tile0, q_stack, k_at[qh], v_at[qh], fill))

    scores = []
    for _, q_stack, k_rows, _, fill in batches:
        qk = lax.dot_general(q_stack, k_rows, _NT, preferred_element_type=F32)
        scores.append(jnp.concatenate([jnp.where(mask_prev, qk[:, :LANES], fill),
                                       jnp.where(mask_cur, qk[:, LANES:], NEG_INF)], axis=1))
    maxes = [jnp.max(s, axis=-1, keepdims=True) for s in scores]
    probs = [jnp.exp(s - mx) for s, mx in zip(scores, maxes)]
    sums = [jnp.sum(p, axis=-1, keepdims=True) for p in probs]
    outs = [jnp.dot(p.astype(BF16), bt[3], preferred_element_type=F32) / den
            for p, bt, den in zip(probs, batches, sums)]
    for i in range(0, len(batches), heads_per_tile):
        tile0 = batches[i][0]
        o_stack = sum(outs[i + 1:i + heads_per_tile], outs[i])
        for t in range(tiles_per_group):
            qcols = pl.ds((tile0 + t) * LANES, LANES)
            o_t = o_stack[t * SWA_BLOCK:(t + 1) * SWA_BLOCK, :]
            o_ref[:, qcols] = (o_t * z_ref[:, qcols].astype(F32)).astype(o_ref.dtype)


def _swa(p_qkv, p_z, rope_tab, sinks, layer, batch, seq):
    m = p_qkv.shape[0]
    nb = seq // SWA_BLOCK
    w = BRANCH_WIDTH
    kblk = w // SWA_KV_WIDTH
    vblk = kblk + 1

    def cur(b, n):
        return b * nb + n

    def prev(b, n):
        return b * nb + jnp.maximum(n - 1, 0)

    return pl.pallas_call(
        functools.partial(_swa_body, layer),
        grid=(batch, nb),
        in_specs=[
            pl.BlockSpec((SWA_BLOCK, w), lambda b, n: (cur(b, n), 0)),
            pl.BlockSpec((SWA_BLOCK, SWA_KV_WIDTH), lambda b, n: (cur(b, n), kblk)),
            pl.BlockSpec((SWA_BLOCK, SWA_KV_WIDTH), lambda b, n: (prev(b, n), kblk)),
            pl.BlockSpec((SWA_BLOCK, SWA_KV_WIDTH), lambda b, n: (cur(b, n), vblk)),
            pl.BlockSpec((SWA_BLOCK, SWA_KV_WIDTH), lambda b, n: (prev(b, n), vblk)),
            pl.BlockSpec((SWA_BLOCK, w), lambda b, n: (cur(b, n), 0)),
            pl.BlockSpec((3, SWA_BLOCK, LANES), lambda b, n: (0, cur(b, n), 0)),
            pl.BlockSpec((3, SWA_BLOCK, LANES), lambda b, n: (0, prev(b, n), 0)),
            pl.BlockSpec(memory_space=pltpu.SMEM),
        ],
        out_specs=pl.BlockSpec((SWA_BLOCK, w), lambda b, n: (cur(b, n), 0)),
        out_shape=jax.ShapeDtypeStruct((m, w), BF16),
        compiler_params=_params("arbitrary", "arbitrary"),
        name="swa",
    )(p_qkv, p_qkv, p_qkv, p_qkv, p_qkv, p_z, rope_tab, rope_tab, sinks)


def _mlstm_body(chunks, nb, q_ref, k_ref, v_ref, o_ref, z_ref, gc_ref, gt_ref, ng_ref, y_ref,
                c_ref, n_ref, m_ref):
    L, DK, DV = MLSTM_CHUNK, MLSTM_QK_DIM, MLSTM_V_DIM

    @pl.when(pl.program_id(1) == 0)
    def _():
        c_ref[...] = jnp.zeros_like(c_ref)
        n_ref[...] = jnp.zeros_like(n_ref)
        m_ref[...] = jnp.zeros_like(m_ref)

    r = lax.broadcasted_iota(jnp.int32, (L, L), 0)
    c = lax.broadcasted_iota(jnp.int32, (L, L), 1)
    causal = r >= c
    lane_g = lax.broadcasted_iota(jnp.int32, (L, LANES), 1)
    row_g = lax.broadcasted_iota(jnp.int32, (GATE_ROWS, L), 0)
    chains = [(bi, h) for bi in range(nb) for h in range(MLSTM_HEADS)]

    for ch in range(chunks):
        rows = pl.ds(ch * L, L)
        gates, cums, stab, qkv, inter, houts = [], [], [], [], [], []
        for bi, h in chains:
            gc = gc_ref[bi, rows, :]
            gt = gt_ref[bi, :, ch * L:(ch + 1) * L]
            i_col = jnp.sum(jnp.where(lane_g == h, gc, 0.0), axis=1, keepdims=True)
            f_col = jnp.sum(jnp.where(lane_g == h + MLSTM_HEADS, gc, 0.0), axis=1, keepdims=True)
            i_row = jnp.sum(jnp.where(row_g == h, gt, 0.0), axis=0, keepdims=True)
            f_row = jnp.sum(jnp.where(row_g == h + MLSTM_HEADS, gt, 0.0), axis=0, keepdims=True)
            gates.append((i_col, f_col, i_row, f_row))
        for (bi, h), (i_col, f_col, i_row, f_row) in zip(chains, gates):
            b_col = jnp.sum(jnp.where(causal, f_row, 0.0), axis=1, keepdims=True)
            b_row = jnp.sum(jnp.where(r <= c, f_col, 0.0), axis=0, keepdims=True)
            g_tot = jnp.sum(f_row, axis=1, keepdims=True)
            cums.append((b_col, b_row, g_tot, m_ref[bi * MLSTM_HEADS + h]))
        for (i_col, f_col, i_row, f_row), (b_col, b_row, g_tot, m_prev) in zip(gates, cums):
            log_d = jnp.where(causal, b_col - b_row + i_row, NEG_INF)
            m_inter = b_col + m_prev
            m_t = jnp.maximum(m_inter, jnp.max(log_d, axis=1, keepdims=True))
            stab.append((m_t, jnp.exp(log_d - m_t), jnp.exp(m_inter - m_t)))
        for bi, h in chains:
            qf = q_ref[bi, rows, h * DK:(h + 1) * DK].astype(F32) * (DK ** -0.5)
            qs = qf.astype(BF16)
            k = k_ref[bi, rows, h * DK:(h + 1) * DK]
            v = v_ref[bi, rows, h * DV:(h + 1) * DV]
            qk = lax.dot_general(qs, k, _NT, preferred_element_type=F32)
            qkv.append((qf, qs, k, v, qk))
        for (bi, h), (qf, qs, k, v, qk), (m_t, dmat, a) in zip(chains, qkv, stab):
            c_prev = c_ref[bi * MLSTM_HEADS + h]
            inter.append((qk * dmat, c_prev,
                          jnp.dot(qs, c_prev.astype(BF16), preferred_element_type=F32)))
        for (bi, h), (qf, qs, k, v, qk), (m_t, dmat, a), (s, c_prev, qc) in zip(
                chains, qkv, stab, inter):
            num = jnp.dot(s.astype(BF16), v, preferred_element_type=F32) + a * qc
            qn = jnp.sum(qf * n_ref[bi * MLSTM_HEADS + h], axis=1, keepdims=True)
            den = jnp.sum(s, axis=1, keepdims=True) + a * qn
            houts.append(num / jnp.maximum(jnp.abs(den), jnp.exp(-m_t)))
        for (bi, h), (i_col, f_col, i_row, f_row), (b_col, b_row, g_tot, m_prev), \
                (qf, qs, k, v, qk), (s, c_prev, qc) in zip(chains, gates, cums, qkv, inter):
            idx = bi * MLSTM_HEADS + h
            w_col = g_tot - b_col + i_col
            w_row = g_tot - b_row + i_row
            m_new = jnp.maximum(g_tot + m_prev, jnp.max(w_row, axis=1, keepdims=True))
            wgt = jnp.exp(w_col - m_new)
            decay = jnp.exp(g_tot + m_prev - m_new)
            kw = k.astype(F32) * wgt
            c_ref[idx] = decay * c_prev + lax.dot_general(
                kw.astype(BF16), v, (((0,), (0,)), ((), ())), preferred_element_type=F32)
            n_ref[idx] = decay * n_ref[idx] + jnp.sum(kw, axis=0, keepdims=True)
            m_ref[idx] = m_new
        for (bi, h), hout in zip(chains, houts):
            vcols = pl.ds(h * DV, DV)
            hn = hout * lax.rsqrt(jnp.mean(hout * hout, axis=1, keepdims=True) + NORM_EPS)
            hn = hn * ng_ref[:, vcols]
            y = hn * o_ref[bi, rows, vcols].astype(F32) * z_ref[bi, rows, vcols].astype(F32)
            y_ref[bi, rows, vcols] = y.astype(y_ref.dtype)


def _mlstm(p_qkv, p_o, p_z, gcol, gtr, norm_g, layer, batch, seq, ts=128, nb=2):
    m = p_qkv.shape[0]
    nt = seq // ts
    qw, vw = MLSTM_HEADS * MLSTM_QK_DIM, BRANCH_WIDTH
    assert batch % nb == 0

    def tok3(a):
        return a.reshape(batch, seq, a.shape[-1])

    chains = nb * MLSTM_HEADS
    out = pl.pallas_call(
        functools.partial(_mlstm_body, ts // MLSTM_CHUNK, nb),
        grid=(batch // nb, nt),
        in_specs=[
            pl.BlockSpec((nb, ts, qw), lambda b, t: (b, t, 0)),
            pl.BlockSpec((nb, ts, qw), lambda b, t: (b, t, 1)),
            pl.BlockSpec((nb, ts, vw), lambda b, t: (b, t, 2 * qw // vw)),
            pl.BlockSpec((nb, ts, vw), lambda b, t: (b, t, 0)),
            pl.BlockSpec((nb, ts, vw), lambda b, t: (b, t, 0)),
            pl.BlockSpec((nb, ts, LANES), lambda b, t: (b, t, 0)),
            pl.BlockSpec((nb, GATE_ROWS, ts), lambda b, t: (b, 0, t)),
            pl.BlockSpec((None, 1, vw), lambda b, t: (layer, 0, 0)),
        ],
        out_specs=pl.BlockSpec((nb, ts, vw), lambda b, t: (b, t, 0)),
        out_shape=jax.ShapeDtypeStruct((batch, seq, BRANCH_WIDTH), BF16),
        scratch_shapes=[
            pltpu.VMEM((chains, MLSTM_QK_DIM, MLSTM_V_DIM), F32),
            pltpu.VMEM((chains, 1, MLSTM_QK_DIM), F32),
            pltpu.VMEM((chains, 1, 1), F32),
        ],
        compiler_params=_params("arbitrary", "arbitrary"),
        name="mlstm",
    )(tok3(p_qkv), tok3(p_qkv), tok3(p_qkv), tok3(p_o), tok3(p_z), tok3(gcol), gtr, norm_g)
    return out.reshape(m, BRANCH_WIDTH)


def _merge_body(ya_ref, yb_ref, yc_ref, wp_ref, g0_ref, g1_ref, g2_ref, o_ref, wb_even, wb_odd):
    piece = wp_ref.shape[1]
    rows = pl.ds(pl.multiple_of(pl.program_id(1) * piece, piece), piece)

    def cast(dst):
        dst[:, rows, :] = wp_ref[...].astype(BF16)

    def compute(src):
        acc = g0_ref[...].astype(F32) * jnp.dot(ya_ref[...], src[0], preferred_element_type=F32)
        acc = acc + g1_ref[...].astype(F32) * jnp.dot(yb_ref[...], src[1],
                                                      preferred_element_type=F32)
        acc = acc + g2_ref[...].astype(F32) * jnp.dot(yc_ref[...], src[2],
                                                      preferred_element_type=F32)
        o_ref[...] = acc.astype(o_ref.dtype)

    _stream_steps(cast, compute, wb_even, wb_odd)


def _merge(ya, yb, yc, w_branch, p_gates, layer, tm=512, tn=1024):
    m = ya.shape[0]
    w = BRANCH_WIDTH
    n_col, n_row = D_MODEL // tn, m // tm
    piece = w // n_row
    yspec = pl.BlockSpec((tm, w), lambda jj, i: (_row_of(jj, i), 0))

    def gate_spec(branch):
        return pl.BlockSpec(
            (tm, tn), lambda jj, i: (_row_of(jj, i), branch * n_col + _col_of(jj)))

    wb = pltpu.VMEM((N_BRANCHES, w, tn), BF16)
    return pl.pallas_call(
        _merge_body,
        grid=(n_col + 1, n_row),
        in_specs=[
            yspec, yspec, yspec,
            pl.BlockSpec((None, N_BRANCHES, piece, tn),
                         lambda jj, i: (layer, 0, i, _cast_col_of(jj, n_col))),
            gate_spec(0), gate_spec(1), gate_spec(2),
        ],
        out_specs=pl.BlockSpec((tm, tn), lambda jj, i: (_row_of(jj, i), _col_of(jj))),
        out_shape=jax.ShapeDtypeStruct((m, D_MODEL), BF16),
        scratch_shapes=[wb, wb],
        compiler_params=_params("arbitrary", "arbitrary"),
        name="merge",
    )(ya, yb, yc, w_branch, p_gates, p_gates, p_gates)


def _outproj_body(a_ref, wp_ref, t_ref, ssq_ref, wb_even, wb_odd):
    jj = pl.program_id(0)
    i = pl.program_id(1)
    piece = wp_ref.shape[0]
    rows = pl.ds(pl.multiple_of(i * piece, piece), piece)

    def cast(dst):
        dst[rows, :] = wp_ref[...].astype(BF16)

    def compute(src):
        acc = jnp.dot(a_ref[...], src[...], preferred_element_type=F32)
        t_ref[...] = acc.astype(t_ref.dtype)
        out_rows = pl.ds(pl.multiple_of(i * TM, TM), TM)
        part = jnp.broadcast_to(jnp.sum(acc * acc, axis=-1, keepdims=True), (TM, LANES))

        @pl.when(jj == 1)
        def _():
            ssq_ref[out_rows, :] = part

        @pl.when(jj > 1)
        def _():
            ssq_ref[out_rows, :] += part

    _stream_steps(cast, compute, wb_even, wb_odd)


def _outproj(a, w_out, layer, tn=1024):
    m, k = a.shape
    n = w_out.shape[-1]
    n_col, n_row = n // tn, m // TM
    piece = k // n_row
    return pl.pallas_call(
        _outproj_body,
        grid=(n_col + 1, n_row),
        in_specs=[
            pl.BlockSpec((TM, k), lambda jj, i: (_row_of(jj, i), 0)),
            pl.BlockSpec((None, piece, tn), lambda jj, i: (layer, i, _cast_col_of(jj, n_col))),
        ],
        out_specs=[
            pl.BlockSpec((TM, tn), lambda jj, i: (_row_of(jj, i), _col_of(jj))),
            pl.BlockSpec((m, LANES), lambda jj, i: (0, 0)),
        ],
        out_shape=[
            jax.ShapeDtypeStruct((m, n), BF16),
            jax.ShapeDtypeStruct((m, LANES), F32),
        ],
        scratch_shapes=[pltpu.VMEM((k, tn), BF16), pltpu.VMEM((k, tn), BF16)],
        compiler_params=_params("arbitrary", "arbitrary"),
        name="outproj",
    )(a, w_out)


def _post_residual(x, t, ssq, g):
    rs = lax.rsqrt(ssq * (1.0 / D_MODEL) + NORM_EPS)
    return x + t.astype(F32) * rs * g


def _postnorm_body(x_ref, t_ref, ssq_ref, g_ref, x1b_ref):
    x1b_ref[...] = _post_residual(x_ref[...], t_ref[...], ssq_ref[:, 0:1], g_ref[...]).astype(BF16)


def _postnorm(x2d, t, ssq, norm_post, layer, tm=256):
    m = x2d.shape[0]
    return pl.pallas_call(
        _postnorm_body,
        grid=(m // tm,),
        in_specs=[
            pl.BlockSpec((tm, D_MODEL), lambda i: (i, 0)),
            pl.BlockSpec((tm, D_MODEL), lambda i: (i, 0)),
            pl.BlockSpec((tm, LANES), lambda i: (i, 0)),
            pl.BlockSpec((None, 1, D_MODEL), lambda i: (layer, 0, 0)),
        ],
        out_specs=pl.BlockSpec((tm, D_MODEL), lambda i: (i, 0)),
        out_shape=jax.ShapeDtypeStruct((m, D_MODEL), BF16),
        compiler_params=_params("arbitrary"),
        name="postnorm",
    )(x2d, t, ssq, norm_post)


def _ple_body(p_ref, w_ref, g_ref, e_ref, wb_ref):
    @pl.when(pl.program_id(0) == 0)
    def _():
        wb_ref[...] = w_ref[...].astype(BF16)

    t = jnp.dot(p_ref[...].astype(BF16), wb_ref[...], preferred_element_type=F32)
    e = t * lax.rsqrt(jnp.mean(t * t, axis=-1, keepdims=True) + NORM_EPS) * g_ref[...]
    e_ref[...] = e.astype(e_ref.dtype)


def _ple(p3d, ple_proj, ple_norm, layer, tm=256):
    m = p3d.shape[1]
    return pl.pallas_call(
        _ple_body,
        grid=(m // tm,),
        in_specs=[
            pl.BlockSpec((None, tm, PLE_DIM), lambda i: (layer, i, 0)),
            pl.BlockSpec((None, PLE_DIM, D_MODEL), lambda i: (layer, 0, 0)),
            pl.BlockSpec((None, 1, D_MODEL), lambda i: (layer, 0, 0)),
        ],
        out_specs=pl.BlockSpec((tm, D_MODEL), lambda i: (i, 0)),
        out_shape=jax.ShapeDtypeStruct((m, D_MODEL), BF16),
        scratch_shapes=[pltpu.VMEM((PLE_DIM, D_MODEL), BF16)],
        compiler_params=_params("arbitrary"),
        name="ple_embed",
    )(p3d, ple_proj, ple_norm)


def _plegate_body(a_ref, wp_ref, x_ref, t_ref, ssq_ref, g_ref, e_ref, o_ref, wb_even, wb_odd):
    piece = wp_ref.shape[0]
    rows = pl.ds(pl.multiple_of(pl.program_id(1) * piece, piece), piece)

    def cast(dst):
        dst[rows, :] = wp_ref[...].astype(BF16)

    def compute(src):
        acc = jnp.dot(a_ref[...], src[...], preferred_element_type=F32)
        x1 = _post_residual(x_ref[...], t_ref[...], ssq_ref[:, 0:1], g_ref[...])
        o_ref[...] = x1 + _sigmoid(acc) * e_ref[...].astype(F32)

    _stream_steps(cast, compute, wb_even, wb_odd)


def _plegate(x1b, ple_gate, x2d, t, ssq, norm_post, e, layer, tn=512):
    m, k = x1b.shape
    n = ple_gate.shape[-1]
    n_col, n_row = n // tn, m // TM
    piece = k // n_row
    tile = pl.BlockSpec((TM, tn), lambda jj, i: (_row_of(jj, i), _col_of(jj)))
    return pl.pallas_call(
        _plegate_body,
        grid=(n_col + 1, n_row),
        in_specs=[
            pl.BlockSpec((TM, k), lambda jj, i: (_row_of(jj, i), 0)),
            pl.BlockSpec((None, piece, tn), lambda jj, i: (layer, i, _cast_col_of(jj, n_col))),
            tile, tile,
            pl.BlockSpec((TM, LANES), lambda jj, i: (_row_of(jj, i), 0)),
            pl.BlockSpec((None, 1, tn), lambda jj, i: (layer, 0, _col_of(jj))),
            tile,
        ],
        out_specs=tile,
        out_shape=jax.ShapeDtypeStruct((m, n), F32),
        scratch_shapes=[pltpu.VMEM((k, tn), BF16), pltpu.VMEM((k, tn), BF16)],
        compiler_params=_params("arbitrary", "arbitrary"),
        name="ple_gate",
    )(x1b, ple_gate, x2d, t, ssq, norm_post, e)


def kernel(x, p, positions, norm_pre, w_in, gmlp_ln_g, gmlp_ln_b, gmlp_ws, gmlp_bs, attn_sinks,
           mlstm_ib, mlstm_fb, mlstm_norm_g, w_branch, w_out, norm_post, ple_proj, ple_norm,
           ple_gate):
    batch, seq, d = x.shape
    depth = w_in.shape[0]
    m = batch * seq
    assert d == D_MODEL and w_in.shape[-1] == N_IN and seq % 512 == 0 and m % TM == 0

    inv_freq = ROPE_THETA ** (-jnp.arange(0, ROT_DIM, 2, dtype=F32) / ROT_DIM)
    lane_d = jnp.arange(LANES) % SWA_HEAD_DIM
    inv_lane = jnp.where(lane_d < ROT_DIM, inv_freq[lane_d % (ROT_DIM // 2)], 0.0).reshape(1, LANES)
    posf = positions.astype(F32).reshape(m, 1)
    p3d = p.reshape(depth, m, PLE_DIM)
    bst = jnp.swapaxes(gmlp_bs, 1, 2)
    gate_bias = jnp.pad(jnp.concatenate([mlstm_ib, mlstm_fb], axis=1).astype(F32),
                        ((0, 0), (0, LANES - GATE_COLS)))

    def rows3(a):
        return a.reshape(depth, 1, a.shape[-1])

    norm_pre, norm_post, ple_norm = rows3(norm_pre), rows3(norm_post), rows3(ple_norm)
    gmlp_ln_g, gmlp_ln_b = rows3(gmlp_ln_g), rows3(gmlp_ln_b)
    mlstm_norm_g = rows3(mlstm_norm_g)
    rope_tab = _rope_call(posf, inv_lane)

    w_t = jnp.swapaxes(w_in, 1, 2)

    xc = x.reshape(m, d)
    for l in range(depth):
        h, gcol, gtr = _prenorm(xc, norm_pre, w_t, gate_bias[l:l + 1], l, batch, seq)

        p_uv = _proj(h, w_t, l, ("a_u", "a_v"), "gelu", TN)
        p_az = _proj(h, w_t, l, ("a_z",), "silu", TN)
        p_bqkv = _proj(h, w_t, l, ("b_q", "b_k", "b_v"), "linear", TN // 2)
        p_bz = _proj(h, w_t, l, ("b_z",), "silu", TN)
        p_cqkv = _proj(h, w_t, l, ("c_q", "c_k", "c_v"), "linear", TN)
        p_co = _proj(h, w_t, l, ("c_o",), "sigmoid", TN)
        p_cz = _proj(h, w_t, l, ("c_z",), "silu", TN)
        p_gates = _proj(h, w_t, l, ("gates",), "sigmoid", TN)

        ya = _gmlp(p_uv, p_az, gmlp_ln_g, gmlp_ln_b, gmlp_ws, bst, l)
        yb = _swa(p_bqkv, p_bz, rope_tab, attn_sinks, l, batch, seq)
        yc = _mlstm(p_cqkv, p_co, p_cz, gcol, gtr, mlstm_norm_g, l, batch, seq)

        mixed = _merge(ya, yb, yc, w_branch, p_gates, l)
        t, ssq = _outproj(mixed, w_out, l)
        x1b = _postnorm(xc, t, ssq, norm_post, l)
        e = _ple(p3d, ple_proj, ple_norm, l)
        xc = _plegate(x1b, ple_gate, xc, t, ssq, norm_post, e, l)
    return xc.reshape(batch, seq, d)
```

```python
import functools

import jax
import jax.numpy as jnp
from jax import lax
from jax.experimental import pallas as pl
from jax.experimental.pallas import tpu as pltpu

F32 = jnp.float32
BF16 = jnp.bfloat16

D_MODEL = 4096
PLE_DIM = 256
N_BRANCHES = 3
BRANCH_WIDTH = D_MODEL // 2
GMLP_CHUNK = 128
GMLP_GROUPS = 8
GMLP_GROUP_WIDTH = BRANCH_WIDTH // GMLP_GROUPS
SWA_HEAD_DIM = 64
SWA_Q_HEADS = BRANCH_WIDTH // SWA_HEAD_DIM
SWA_KV_HEADS = 4
SWA_GROUP = SWA_Q_HEADS // SWA_KV_HEADS
SWA_BLOCK = 128
SWA_KV_WIDTH = SWA_KV_HEADS * SWA_HEAD_DIM
ROT_DIM = SWA_HEAD_DIM // 4
ROPE_THETA = 500000.0
MLSTM_HEADS = 4
MLSTM_V_DIM = BRANCH_WIDTH // MLSTM_HEADS
MLSTM_QK_DIM = MLSTM_V_DIM // 2
MLSTM_CHUNK = 64
GATE_SOFTCAP = 15.0
NORM_EPS = 1e-6
NEG_INF = -1e30

LANES = 128
GATE_ROWS = 16
VMEM_LIMIT = 56 * 1024 * 1024
TM = 1024
TN = 1024

_SPLIT = (
    ("a_u", BRANCH_WIDTH), ("a_v", BRANCH_WIDTH), ("a_z", BRANCH_WIDTH),
    ("b_q", BRANCH_WIDTH), ("b_k", SWA_KV_WIDTH), ("b_v", SWA_KV_WIDTH), ("b_z", BRANCH_WIDTH),
    ("c_q", MLSTM_HEADS * MLSTM_QK_DIM), ("c_k", MLSTM_HEADS * MLSTM_QK_DIM),
    ("c_v", BRANCH_WIDTH), ("c_i", MLSTM_HEADS), ("c_f", MLSTM_HEADS),
    ("c_o", BRANCH_WIDTH), ("c_z", BRANCH_WIDTH), ("gates", N_BRANCHES * D_MODEL),
)
_SRC = {}
_off = 0
for _name, _size in _SPLIT:
    _SRC[_name] = (_off, _size)
    _off += _size
N_IN = _off
GATE_COLS = 2 * MLSTM_HEADS
GATE_START = _SRC["c_i"][0]


def _params(*sem):
    return pltpu.CompilerParams(dimension_semantics=sem, vmem_limit_bytes=VMEM_LIMIT)


def _gelu(x):
    return 0.5 * x * (1.0 + lax.erf(x * (0.5 ** 0.5)))


def _sigmoid(x):
    return jax.nn.sigmoid(x)


def _softcap(z):
    return GATE_SOFTCAP * jnp.tanh(z / GATE_SOFTCAP)


def _log_sigmoid(x):
    return -(jnp.maximum(-x, 0.0) + jnp.log1p(jnp.exp(-jnp.abs(x))))


def _stream_step(wb_ref, cast, compute):
    jj = pl.program_id(0)
    cast(wb_ref.at[jj % 2])

    @pl.when(jj > 0)
    def _():
        compute(wb_ref.at[(jj + 1) % 2])


def _row_of(jj, i):
    return jnp.where(jj > 0, i, 0)


def _col_of(jj):
    return jnp.maximum(jj - 1, 0)


def _cast_col_of(jj, n_col):
    return jnp.minimum(jj, n_col - 1)


def _prenorm_body(x_ref, g_ref, wif_ref, brow_ref, h_ref, gc_ref, gt_ref, wb_ref):
    @pl.when(pl.program_id(0) == 0)
    def _():
        wb_ref[...] = wif_ref[...].astype(BF16)

    x = x_ref[...]
    y = x * lax.rsqrt(jnp.mean(x * x, axis=-1, keepdims=True) + NORM_EPS) * g_ref[...]
    hb = y.astype(BF16)
    h_ref[...] = hb
    pre = lax.dot_general(hb, wb_ref[...], (((1,), (1,)), ((), ())),
                          preferred_element_type=F32) + brow_ref[...]
    sc = _softcap(pre)
    lane = lax.broadcasted_iota(jnp.int32, sc.shape, 1)
    gc = jnp.where(lane >= MLSTM_HEADS, _log_sigmoid(sc), sc)
    gc_ref[...] = gc
    gt_ref[...] = gc.T[:GATE_ROWS, :]


def _prenorm(x2d, norm_pre, w_t, brow, layer, batch, seq, tm=256):
    m = x2d.shape[0]
    per_seq = seq // tm
    assert GATE_START % LANES == 0 and seq % tm == 0
    return pl.pallas_call(
        _prenorm_body,
        grid=(m // tm,),
        in_specs=[
            pl.BlockSpec((tm, D_MODEL), lambda i: (i, 0)),
            pl.BlockSpec((None, 1, D_MODEL), lambda i: (layer, 0, 0)),
            pl.BlockSpec((None, LANES, D_MODEL), lambda i: (layer, GATE_START // LANES, 0)),
            pl.BlockSpec((1, LANES), lambda i: (0, 0)),
        ],
        out_specs=[
            pl.BlockSpec((tm, D_MODEL), lambda i: (i, 0)),
            pl.BlockSpec((tm, LANES), lambda i: (i, 0)),
            pl.BlockSpec((None, GATE_ROWS, tm), lambda i: (i // per_seq, 0, i % per_seq)),
        ],
        out_shape=[
            jax.ShapeDtypeStruct((m, D_MODEL), BF16),
            jax.ShapeDtypeStruct((m, LANES), F32),
            jax.ShapeDtypeStruct((batch, GATE_ROWS, seq), F32),
        ],
        scratch_shapes=[pltpu.VMEM((LANES, D_MODEL), BF16)],
        compiler_params=_params("arbitrary"),
        name="prenorm",
    )(x2d, norm_pre, w_t, brow)


_NT = (((1,), (1,)), ((), ()))

_PROJ_ORDER = (
    ("a_u", "gelu"), ("a_v", "gelu"), ("a_z", "silu"), ("b_q", "linear"), ("b_z", "silu"),
    ("c_q", "linear"), ("c_k", "linear"), ("c_v", "linear"), ("c_o", "sigmoid"), ("c_z", "silu"),
    ("gates", "sigmoid"), ("b_k", "linear"),
)


def _proj_layout():
    tiles, cols, col = [], {}, 0
    for name, kind in _PROJ_ORDER:
        start, width = _SRC[name]
        cols[name] = col
        n_tiles = -(-width // TN)
        tiles += [(start + i * TN, kind) for i in range(n_tiles)]
        col += n_tiles * TN
    cols["b_v"] = cols["b_k"] + _SRC["b_k"][1]
    assert all(s % 8 == 0 and s + TN <= N_IN for s, _ in tiles)
    return tuple(tiles), cols, col


_PROJ_TILES, PROJ_COL, PROJ_WIDTH = _proj_layout()


def _proj_block(name, width):
    assert PROJ_COL[name] % width == 0
    return PROJ_COL[name] // width


_N_GELU = sum(kind == "gelu" for _, kind in _PROJ_TILES)
assert all(kind == "gelu" for _, kind in _PROJ_TILES[:_N_GELU])


def _tile_is(t, kind):
    hit = False
    for idx, (_, k) in enumerate(_PROJ_TILES):
        if k == kind:
            hit = (t == idx) | hit
    return hit


def _tile_src_start(t):
    start = _PROJ_TILES[0][0] + t * TN
    for idx in range(1, len(_PROJ_TILES)):
        if _PROJ_TILES[idx][0] != _PROJ_TILES[idx - 1][0] + TN:
            start = jnp.where(t >= idx, _PROJ_TILES[idx][0] + (t - idx) * TN, start)
    return start


def _proj_body(h_ref, wp_ref, o_ref, wb_ref):
    jj = pl.program_id(0)
    t = jj - 1
    piece = wp_ref.shape[0]
    rows = pl.ds(pl.multiple_of(pl.program_id(1) * piece, piece), piece)
    wb_ref[jj % 2, rows, :] = wp_ref[...].astype(BF16)

    def matmul():
        return lax.dot_general(h_ref[...], wb_ref[(jj + 1) % 2], _NT, preferred_element_type=F32)

    @pl.when((jj > 0) & (t < _N_GELU))
    def _():
        o_ref[...] = _gelu(matmul()).astype(o_ref.dtype)

    @pl.when(t >= _N_GELU)
    def _():
        acc = matmul()
        sig = _sigmoid(acc)
        out = jnp.where(_tile_is(t, "linear"), acc, jnp.where(_tile_is(t, "silu"), acc * sig, sig))
        o_ref[...] = out.astype(o_ref.dtype)


def _proj(h, w_t, layer):
    m, k = h.shape
    n_col, n_row = len(_PROJ_TILES), m // TM
    piece = TN // n_row
    assert TN % n_row == 0 and piece % 16 == 0

    def w_index(jj, i):
        src = _tile_src_start(_cast_col_of(jj, n_col))
        return layer, pl.multiple_of(src + i * piece, 8), 0

    return pl.pallas_call(
        _proj_body,
        grid=(n_col + 1, n_row),
        in_specs=[
            pl.BlockSpec((TM, k), lambda jj, i: (_row_of(jj, i), 0)),
            pl.BlockSpec((None, pl.Element(piece), pl.Element(k)), w_index),
        ],
        out_specs=pl.BlockSpec((TM, TN), lambda jj, i: (_row_of(jj, i), _col_of(jj))),
        out_shape=jax.ShapeDtypeStruct((m, PROJ_WIDTH), BF16),
        scratch_shapes=[pltpu.VMEM((2, TN, k), BF16)],
        compiler_params=_params("arbitrary", "arbitrary"),
        name="proj",
    )(h, w_t)


def _gmlp_body(chunks, u_ref, v_ref, z_ref, lg_ref, lb_ref, ws_ref, bst_ref, o_ref):
    r = lax.broadcasted_iota(jnp.int32, (GMLP_CHUNK, GMLP_CHUNK), 0)
    c = lax.broadcasted_iota(jnp.int32, (GMLP_CHUNK, GMLP_CHUNK), 1)
    tri = r >= c
    wmix = [jnp.where(tri, ws_ref[g], 0.0).astype(BF16) for g in range(GMLP_GROUPS)]
    for ch in range(chunks):
        rows = pl.ds(ch * GMLP_CHUNK, GMLP_CHUNK)
        v = v_ref[rows, :].astype(F32)
        mu = jnp.mean(v, axis=-1, keepdims=True)
        d = v - mu
        var = jnp.mean(d * d, axis=-1, keepdims=True)
        vn = (d * lax.rsqrt(var + NORM_EPS) * lg_ref[...] + lb_ref[...]).astype(BF16)
        for g in range(GMLP_GROUPS):
            cols = pl.ds(g * GMLP_GROUP_WIDTH, GMLP_GROUP_WIDTH)
            mixed = jnp.dot(wmix[g], vn[:, g * GMLP_GROUP_WIDTH:(g + 1) * GMLP_GROUP_WIDTH],
                            preferred_element_type=F32) + bst_ref[:, g:g + 1]
            y = u_ref[rows, cols].astype(F32) * mixed * z_ref[rows, cols].astype(F32)
            o_ref[rows, cols] = y.astype(o_ref.dtype)


def _gmlp(proj, ln_g, ln_b, ws, bst, layer, ts=512):
    m = proj.shape[0]
    w = BRANCH_WIDTH
    ublk, vblk, zblk = (_proj_block(n, w) for n in ("a_u", "a_v", "a_z"))
    return pl.pallas_call(
        functools.partial(_gmlp_body, ts // GMLP_CHUNK),
        grid=(m // ts,),
        in_specs=[
            pl.BlockSpec((ts, w), lambda i: (i, ublk)),
            pl.BlockSpec((ts, w), lambda i: (i, vblk)),
            pl.BlockSpec((ts, w), lambda i: (i, zblk)),
            pl.BlockSpec((None, 1, w), lambda i: (layer, 0, 0)),
            pl.BlockSpec((None, 1, w), lambda i: (layer, 0, 0)),
            pl.BlockSpec((None, GMLP_GROUPS, GMLP_CHUNK, GMLP_CHUNK), lambda i: (layer, 0, 0, 0)),
            pl.BlockSpec((None, GMLP_CHUNK, GMLP_GROUPS), lambda i: (layer, 0, 0)),
        ],
        out_specs=pl.BlockSpec((ts, w), lambda i: (i, 0)),
        out_shape=jax.ShapeDtypeStruct((m, w), BF16),
        compiler_params=_params("arbitrary"),
        name="gmlp",
    )(proj, proj, proj, ln_g, ln_b, ws, bst)


def _rope_tables(pos_ref, inv_ref):
    ang = pos_ref[...] * inv_ref[...]
    d = lax.broadcasted_iota(jnp.int32, ang.shape, 1) % SWA_HEAD_DIM
    half = ROT_DIM // 2
    cosf = jnp.where(d < ROT_DIM, jnp.cos(ang), 1.0)
    sinf = jnp.sin(ang)
    s_lo = jnp.where(d < half, -sinf, 0.0)
    s_hi = jnp.where((d >= half) & (d < ROT_DIM), sinf, 0.0)
    return cosf, s_lo, s_hi


def _rope_tile(t, tables):
    cosf, s_lo, s_hi = tables
    half = ROT_DIM // 2
    up = pltpu.roll(t, LANES - half, 1)
    dn = pltpu.roll(t, half, 1)
    return t * cosf + up * s_lo + dn * s_hi


def _rope_body(pos_ref, inv_ref, o_ref):
    cosf, s_lo, s_hi = _rope_tables(pos_ref, inv_ref)
    o_ref[0] = cosf
    o_ref[1] = s_lo
    o_ref[2] = s_hi


def _rope_call(posf, inv_lane, tm=512):
    m = posf.shape[0]
    return pl.pallas_call(
        _rope_body,
        grid=(m // tm,),
        in_specs=[pl.BlockSpec((tm, 1), lambda i: (i, 0)),
                  pl.BlockSpec((1, LANES), lambda i: (0, 0))],
        out_specs=pl.BlockSpec((3, tm, LANES), lambda i: (0, i, 0)),
        out_shape=jax.ShapeDtypeStruct((3, m, LANES), F32),
        compiler_params=_params("arbitrary"),
        name="rope_tables",
    )(posf, inv_lane)


def _swa_body(layer, q_ref, kc_ref, kp_ref, vc_ref, vp_ref, z_ref, tc_ref, tp_ref, sink_ref, o_ref):
    n = pl.program_id(1)
    tab_c = (tc_ref[0], tc_ref[1], tc_ref[2])
    tab_p = (tp_ref[0], tp_ref[1], tp_ref[2])
    heads_per_tile = LANES // SWA_HEAD_DIM
    tiles_per_group = SWA_GROUP // heads_per_tile
    rows = tiles_per_group * SWA_BLOCK

    lane_half = lax.broadcasted_iota(jnp.int32, (2 * SWA_BLOCK, LANES), 1) // SWA_HEAD_DIM
    key_row = lax.broadcasted_iota(jnp.int32, (2 * SWA_BLOCK, LANES), 0)
    qi = lax.broadcasted_iota(jnp.int32, (rows, LANES), 0) % SWA_BLOCK
    kj = lax.broadcasted_iota(jnp.int32, (rows, LANES), 1)
    mask_prev = (kj > qi) & (n > 0)
    mask_cur = kj <= qi
    sink_slot = lax.broadcasted_iota(jnp.int32, (SWA_BLOCK, LANES), 1) == 0

    scale = SWA_HEAD_DIM ** -0.5
    batches = []
    for kt in range(SWA_KV_WIDTH // LANES):
        cols = pl.ds(kt * LANES, LANES)
        k_band = jnp.concatenate(
            [_rope_tile(kp_ref[:, cols].astype(F32), tab_p),
             _rope_tile(kc_ref[:, cols].astype(F32), tab_c)], axis=0)
        v_band = jnp.concatenate([vp_ref[:, cols], vc_ref[:, cols]], axis=0).astype(F32)
        v_band = jnp.where(key_row == 0, 0.0, v_band)
        for hh in range(heads_per_tile):
            kvh = kt * heads_per_tile + hh
            k_keep = jnp.where(lane_half == hh, k_band, 0.0)
            v_keep = jnp.where(lane_half == hh, v_band, 0.0)
            k_swap = pltpu.roll(k_keep, SWA_HEAD_DIM, 1)
            v_swap = pltpu.roll(v_keep, SWA_HEAD_DIM, 1)
            k_at = [k_keep.astype(BF16), k_swap.astype(BF16)]
            v_at = [v_keep.astype(BF16), v_swap.astype(BF16)]
            if hh == 1:
                k_at.reverse()
                v_at.reverse()
            tile0 = kvh * tiles_per_group
            q_stack = jnp.concatenate(
                [(_rope_tile(q_ref[:, pl.ds((tile0 + t) * LANES, LANES)].astype(F32), tab_c)
                  * scale).astype(BF16) for t in range(tiles_per_group)], axis=0)
            for qh in range(heads_per_tile):
                fill = jnp.concatenate(
                    [jnp.where(sink_slot,
                               sink_ref[layer, (tile0 + t) * heads_per_tile + qh], NEG_INF)
                     for t in range(tiles_per_group)], axis=0)
                batches.append((tile0, q_stack, k_at[qh], v_at[qh], fill))

    scores = []
    for _, q_stack, k_rows, _, fill in batches:
        qk = lax.dot_general(q_stack, k_rows, _NT, preferred_element_type=F32)
        scores.append(jnp.concatenate([jnp.where(mask_prev, qk[:, :LANES], fill),
                                       jnp.where(mask_cur, qk[:, LANES:], NEG_INF)], axis=1))
    maxes = [jnp.max(s, axis=-1, keepdims=True) for s in scores]
    probs = [jnp.exp(s - mx) for s, mx in zip(scores, maxes)]
    sums = [jnp.sum(p, axis=-1, keepdims=True) for p in probs]
    outs = [jnp.dot(p.astype(BF16), bt[3], preferred_element_type=F32) / den
            for p, bt, den in zip(probs, batches, sums)]
    for i in range(0, len(batches), heads_per_tile):
        tile0 = batches[i][0]
        o_stack = sum(outs[i + 1:i + heads_per_tile], outs[i])
        for t in range(tiles_per_group):
            qcols = pl.ds((tile0 + t) * LANES, LANES)
            o_t = o_stack[t * SWA_BLOCK:(t + 1) * SWA_BLOCK, :]
            o_ref[:, qcols] = (o_t * z_ref[:, qcols].astype(F32)).astype(o_ref.dtype)


def _swa(proj, rope_tab, sinks, layer, batch, seq):
    m = proj.shape[0]
    nb = seq // SWA_BLOCK
    w = BRANCH_WIDTH
    qblk, zblk = _proj_block("b_q", w), _proj_block("b_z", w)
    kblk, vblk = _proj_block("b_k", SWA_KV_WIDTH), _proj_block("b_v", SWA_KV_WIDTH)

    def cur(b, n):
        return b * nb + n

    def prev(b, n):
        return b * nb + jnp.maximum(n - 1, 0)

    return pl.pallas_call(
        functools.partial(_swa_body, layer),
        grid=(batch, nb),
        in_specs=[
            pl.BlockSpec((SWA_BLOCK, w), lambda b, n: (cur(b, n), qblk)),
            pl.BlockSpec((SWA_BLOCK, SWA_KV_WIDTH), lambda b, n: (cur(b, n), kblk)),
            pl.BlockSpec((SWA_BLOCK, SWA_KV_WIDTH), lambda b, n: (prev(b, n), kblk)),
            pl.BlockSpec((SWA_BLOCK, SWA_KV_WIDTH), lambda b, n: (cur(b, n), vblk)),
            pl.BlockSpec((SWA_BLOCK, SWA_KV_WIDTH), lambda b, n: (prev(b, n), vblk)),
            pl.BlockSpec((SWA_BLOCK, w), lambda b, n: (cur(b, n), zblk)),
            pl.BlockSpec((3, SWA_BLOCK, LANES), lambda b, n: (0, cur(b, n), 0)),
            pl.BlockSpec((3, SWA_BLOCK, LANES), lambda b, n: (0, prev(b, n), 0)),
            pl.BlockSpec(memory_space=pltpu.SMEM),
        ],
        out_specs=pl.BlockSpec((SWA_BLOCK, w), lambda b, n: (cur(b, n), 0)),
        out_shape=jax.ShapeDtypeStruct((m, w), BF16),
        compiler_params=_params("arbitrary", "arbitrary"),
        name="swa",
    )(proj, proj, proj, proj, proj, proj, rope_tab, rope_tab, sinks)


def _mlstm_body(chunks, nb, q_ref, k_ref, v_ref, o_ref, z_ref, gc_ref, gt_ref, ng_ref, y_ref,
                c_ref, n_ref, m_ref):
    L, DK, DV = MLSTM_CHUNK, MLSTM_QK_DIM, MLSTM_V_DIM

    @pl.when(pl.program_id(1) == 0)
    def _():
        c_ref[...] = jnp.zeros_like(c_ref)
        n_ref[...] = jnp.zeros_like(n_ref)
        m_ref[...] = jnp.zeros_like(m_ref)

    r = lax.broadcasted_iota(jnp.int32, (L, L), 0)
    c = lax.broadcasted_iota(jnp.int32, (L, L), 1)
    causal = r >= c
    lane_g = lax.broadcasted_iota(jnp.int32, (L, LANES), 1)
    row_g = lax.broadcasted_iota(jnp.int32, (GATE_ROWS, L), 0)
    chains = [(bi, h) for bi in range(nb) for h in range(MLSTM_HEADS)]

    for ch in range(chunks):
        rows = pl.ds(ch * L, L)
        gates, cums, stab, qkv, inter, houts = [], [], [], [], [], []
        for bi, h in chains:
            gc = gc_ref[bi, rows, :]
            gt = gt_ref[bi, :, ch * L:(ch + 1) * L]
            i_col = jnp.sum(jnp.where(lane_g == h, gc, 0.0), axis=1, keepdims=True)
            f_col = jnp.sum(jnp.where(lane_g == h + MLSTM_HEADS, gc, 0.0), axis=1, keepdims=True)
            i_row = jnp.sum(jnp.where(row_g == h, gt, 0.0), axis=0, keepdims=True)
            f_row = jnp.sum(jnp.where(row_g == h + MLSTM_HEADS, gt, 0.0), axis=0, keepdims=True)
            gates.append((i_col, f_col, i_row, f_row))
        for (bi, h), (i_col, f_col, i_row, f_row) in zip(chains, gates):
            b_col = jnp.sum(jnp.where(causal, f_row, 0.0), axis=1, keepdims=True)
            b_row = jnp.sum(jnp.where(r <= c, f_col, 0.0), axis=0, keepdims=True)
            g_tot = jnp.sum(f_row, axis=1, keepdims=True)
            cums.append((b_col, b_row, g_tot, m_ref[bi * MLSTM_HEADS + h]))
        for (i_col, f_col, i_row, f_row), (b_col, b_row, g_tot, m_prev) in zip(gates, cums):
            log_d = jnp.where(causal, b_col - b_row + i_row, NEG_INF)
            m_inter = b_col + m_prev
            m_t = jnp.maximum(m_inter, jnp.max(log_d, axis=1, keepdims=True))
            stab.append((m_t, jnp.exp(log_d - m_t), jnp.exp(m_inter - m_t)))
        for bi, h in chains:
            qf = q_ref[bi, rows, h * DK:(h + 1) * DK].astype(F32) * (DK ** -0.5)
            qs = qf.astype(BF16)
            k = k_ref[bi, rows, h * DK:(h + 1) * DK]
            v = v_ref[bi, rows, h * DV:(h + 1) * DV]
            qk = lax.dot_general(qs, k, _NT, preferred_element_type=F32)
            qkv.append((qf, qs, k, v, qk))
        for (bi, h), (qf, qs, k, v, qk), (m_t, dmat, a) in zip(chains, qkv, stab):
            c_prev = c_ref[bi * MLSTM_HEADS + h]
            inter.append((qk * dmat, c_prev,
                          jnp.dot(qs, c_prev.astype(BF16), preferred_element_type=F32)))
        for (bi, h), (qf, qs, k, v, qk), (m_t, dmat, a), (s, c_prev, qc) in zip(
                chains, qkv, stab, inter):
            num = jnp.dot(s.astype(BF16), v, preferred_element_type=F32) + a * qc
            qn = jnp.sum(qf * n_ref[bi * MLSTM_HEADS + h], axis=1, keepdims=True)
            den = jnp.sum(s, axis=1, keepdims=True) + a * qn
            houts.append(num / jnp.maximum(jnp.abs(den), jnp.exp(-m_t)))
        for (bi, h), (i_col, f_col, i_row, f_row), (b_col, b_row, g_tot, m_prev), \
                (qf, qs, k, v, qk), (s, c_prev, qc) in zip(chains, gates, cums, qkv, inter):
            idx = bi * MLSTM_HEADS + h
            w_col = g_tot - b_col + i_col
            w_row = g_tot - b_row + i_row
            m_new = jnp.maximum(g_tot + m_prev, jnp.max(w_row, axis=1, keepdims=True))
            wgt = jnp.exp(w_col - m_new)
            decay = jnp.exp(g_tot + m_prev - m_new)
            kw = k.astype(F32) * wgt
            c_ref[idx] = decay * c_prev + lax.dot_general(
                kw.astype(BF16), v, (((0,), (0,)), ((), ())), preferred_element_type=F32)
            n_ref[idx] = decay * n_ref[idx] + jnp.sum(kw, axis=0, keepdims=True)
            m_ref[idx] = m_new
        for (bi, h), hout in zip(chains, houts):
            vcols = pl.ds(h * DV, DV)
            hn = hout * lax.rsqrt(jnp.mean(hout * hout, axis=1, keepdims=True) + NORM_EPS)
            hn = hn * ng_ref[:, vcols]
            y = hn * o_ref[bi, rows, vcols].astype(F32) * z_ref[bi, rows, vcols].astype(F32)
            y_ref[bi, rows, vcols] = y.astype(y_ref.dtype)


def _mlstm(proj, gcol, gtr, norm_g, layer, batch, seq, ts=128, nb=2):
    m = proj.shape[0]
    nt = seq // ts
    qw, vw = MLSTM_HEADS * MLSTM_QK_DIM, BRANCH_WIDTH
    qblk, kblk = _proj_block("c_q", qw), _proj_block("c_k", qw)
    vblk, oblk, zblk = (_proj_block(n, vw) for n in ("c_v", "c_o", "c_z"))
    proj3 = proj.reshape(batch, seq, proj.shape[-1])
    assert batch % nb == 0

    def tok3(a):
        return a.reshape(batch, seq, a.shape[-1])

    chains = nb * MLSTM_HEADS
    out = pl.pallas_call(
        functools.partial(_mlstm_body, ts // MLSTM_CHUNK, nb),
        grid=(batch // nb, nt),
        in_specs=[
            pl.BlockSpec((nb, ts, qw), lambda b, t: (b, t, qblk)),
            pl.BlockSpec((nb, ts, qw), lambda b, t: (b, t, kblk)),
            pl.BlockSpec((nb, ts, vw), lambda b, t: (b, t, vblk)),
            pl.BlockSpec((nb, ts, vw), lambda b, t: (b, t, oblk)),
            pl.BlockSpec((nb, ts, vw), lambda b, t: (b, t, zblk)),
            pl.BlockSpec((nb, ts, LANES), lambda b, t: (b, t, 0)),
            pl.BlockSpec((nb, GATE_ROWS, ts), lambda b, t: (b, 0, t)),
            pl.BlockSpec((None, 1, vw), lambda b, t: (layer, 0, 0)),
        ],
        out_specs=pl.BlockSpec((nb, ts, vw), lambda b, t: (b, t, 0)),
        out_shape=jax.ShapeDtypeStruct((batch, seq, BRANCH_WIDTH), BF16),
        scratch_shapes=[
            pltpu.VMEM((chains, MLSTM_QK_DIM, MLSTM_V_DIM), F32),
            pltpu.VMEM((chains, 1, MLSTM_QK_DIM), F32),
            pltpu.VMEM((chains, 1, 1), F32),
        ],
        compiler_params=_params("arbitrary", "arbitrary"),
        name="mlstm",
    )(proj3, proj3, proj3, proj3, proj3, tok3(gcol), gtr, norm_g)
    return out.reshape(m, BRANCH_WIDTH)


def _merge_body(ya_ref, yb_ref, yc_ref, wp_ref, g0_ref, g1_ref, g2_ref, o_ref, wb_ref):
    piece = wp_ref.shape[1]
    rows = pl.ds(pl.multiple_of(pl.program_id(1) * piece, piece), piece)

    def cast(dst):
        dst[:, rows, :] = wp_ref[...].astype(BF16)

    def compute(src):
        acc = g0_ref[...].astype(F32) * jnp.dot(ya_ref[...], src[0], preferred_element_type=F32)
        acc = acc + g1_ref[...].astype(F32) * jnp.dot(yb_ref[...], src[1],
                                                      preferred_element_type=F32)
        acc = acc + g2_ref[...].astype(F32) * jnp.dot(yc_ref[...], src[2],
                                                      preferred_element_type=F32)
        o_ref[...] = acc.astype(o_ref.dtype)

    _stream_step(wb_ref, cast, compute)


def _merge(ya, yb, yc, w_branch, proj, layer, tm=512, tn=1024):
    m = ya.shape[0]
    w = BRANCH_WIDTH
    n_col, n_row = D_MODEL // tn, m // tm
    piece = w // n_row
    gate0 = PROJ_COL["gates"] // tn
    assert PROJ_COL["gates"] % tn == 0
    yspec = pl.BlockSpec((tm, w), lambda jj, i: (_row_of(jj, i), 0))

    def gate_spec(branch):
        return pl.BlockSpec(
            (tm, tn), lambda jj, i: (_row_of(jj, i), gate0 + branch * n_col + _col_of(jj)))

    return pl.pallas_call(
        _merge_body,
        grid=(n_col + 1, n_row),
        in_specs=[
            yspec, yspec, yspec,
            pl.BlockSpec((None, N_BRANCHES, piece, tn),
                         lambda jj, i: (layer, 0, i, _cast_col_of(jj, n_col))),
            gate_spec(0), gate_spec(1), gate_spec(2),
        ],
        out_specs=pl.BlockSpec((tm, tn), lambda jj, i: (_row_of(jj, i), _col_of(jj))),
        out_shape=jax.ShapeDtypeStruct((m, D_MODEL), BF16),
        scratch_shapes=[pltpu.VMEM((2, N_BRANCHES, w, tn), BF16)],
        compiler_params=_params("arbitrary", "arbitrary"),
        name="merge",
    )(ya, yb, yc, w_branch, proj, proj, proj)


def _outproj_body(a_ref, wp_ref, t_ref, ssq_ref, wb_ref):
    jj = pl.program_id(0)
    i = pl.program_id(1)
    piece = wp_ref.shape[0]
    rows = pl.ds(pl.multiple_of(i * piece, piece), piece)

    def cast(dst):
        dst[rows, :] = wp_ref[...].astype(BF16)

    def compute(src):
        acc = jnp.dot(a_ref[...], src[...], preferred_element_type=F32)
        t_ref[...] = acc.astype(t_ref.dtype)
        out_rows = pl.ds(pl.multiple_of(i * TM, TM), TM)
        part = jnp.broadcast_to(jnp.sum(acc * acc, axis=-1, keepdims=True), (TM, LANES))

        @pl.when(jj == 1)
        def _():
            ssq_ref[out_rows, :] = part

        @pl.when(jj > 1)
        def _():
            ssq_ref[out_rows, :] += part

    _stream_step(wb_ref, cast, compute)


def _outproj(a, w_out, layer, tn=1024):
    m, k = a.shape
    n = w_out.shape[-1]
    n_col, n_row = n // tn, m // TM
    piece = k // n_row
    return pl.pallas_call(
        _outproj_body,
        grid=(n_col + 1, n_row),
        in_specs=[
            pl.BlockSpec((TM, k), lambda jj, i: (_row_of(jj, i), 0)),
            pl.BlockSpec((None, piece, tn), lambda jj, i: (layer, i, _cast_col_of(jj, n_col))),
        ],
        out_specs=[
            pl.BlockSpec((TM, tn), lambda jj, i: (_row_of(jj, i), _col_of(jj))),
            pl.BlockSpec((m, LANES), lambda jj, i: (0, 0)),
        ],
        out_shape=[
            jax.ShapeDtypeStruct((m, n), BF16),
            jax.ShapeDtypeStruct((m, LANES), F32),
        ],
        scratch_shapes=[pltpu.VMEM((2, k, tn), BF16)],
        compiler_params=_params("arbitrary", "arbitrary"),
        name="outproj",
    )(a, w_out)


def _post_residual(x, t, ssq, g):
    rs = lax.rsqrt(ssq * (1.0 / D_MODEL) + NORM_EPS)
    return x + t.astype(F32) * rs * g


def _postnorm_body(x_ref, t_ref, ssq_ref, g_ref, x1b_ref):
    x1b_ref[...] = _post_residual(x_ref[...], t_ref[...], ssq_ref[:, 0:1], g_ref[...]).astype(BF16)


def _postnorm(x2d, t, ssq, norm_post, layer, tm=256):
    m = x2d.shape[0]
    return pl.pallas_call(
        _postnorm_body,
        grid=(m // tm,),
        in_specs=[
            pl.BlockSpec((tm, D_MODEL), lambda i: (i, 0)),
            pl.BlockSpec((tm, D_MODEL), lambda i: (i, 0)),
            pl.BlockSpec((tm, LANES), lambda i: (i, 0)),
            pl.BlockSpec((None, 1, D_MODEL), lambda i: (layer, 0, 0)),
        ],
        out_specs=pl.BlockSpec((tm, D_MODEL), lambda i: (i, 0)),
        out_shape=jax.ShapeDtypeStruct((m, D_MODEL), BF16),
        compiler_params=_params("arbitrary"),
        name="postnorm",
    )(x2d, t, ssq, norm_post)


def _ple_body(p_ref, w_ref, g_ref, e_ref, wb_ref):
    @pl.when(pl.program_id(0) == 0)
    def _():
        wb_ref[...] = w_ref[...].astype(BF16)

    t = jnp.dot(p_ref[...].astype(BF16), wb_ref[...], preferred_element_type=F32)
    e = t * lax.rsqrt(jnp.mean(t * t, axis=-1, keepdims=True) + NORM_EPS) * g_ref[...]
    e_ref[...] = e.astype(e_ref.dtype)


def _ple(p3d, ple_proj, ple_norm, layer, tm=256):
    m = p3d.shape[1]
    return pl.pallas_call(
        _ple_body,
        grid=(m // tm,),
        in_specs=[
            pl.BlockSpec((None, tm, PLE_DIM), lambda i: (layer, i, 0)),
            pl.BlockSpec((None, PLE_DIM, D_MODEL), lambda i: (layer, 0, 0)),
            pl.BlockSpec((None, 1, D_MODEL), lambda i: (layer, 0, 0)),
        ],
        out_specs=pl.BlockSpec((tm, D_MODEL), lambda i: (i, 0)),
        out_shape=jax.ShapeDtypeStruct((m, D_MODEL), BF16),
        scratch_shapes=[pltpu.VMEM((PLE_DIM, D_MODEL), BF16)],
        compiler_params=_params("arbitrary"),
        name="ple_embed",
    )(p3d, ple_proj, ple_norm)


def _plegate_body(a_ref, wp_ref, x_ref, t_ref, ssq_ref, g_ref, e_ref, o_ref, wb_ref):
    piece = wp_ref.shape[0]
    rows = pl.ds(pl.multiple_of(pl.program_id(1) * piece, piece), piece)

    def cast(dst):
        dst[rows, :] = wp_ref[...].astype(BF16)

    def compute(src):
        acc = jnp.dot(a_ref[...], src[...], preferred_element_type=F32)
        x1 = _post_residual(x_ref[...], t_ref[...], ssq_ref[:, 0:1], g_ref[...])
        o_ref[...] = x1 + _sigmoid(acc) * e_ref[...].astype(F32)

    _stream_step(wb_ref, cast, compute)


def _plegate(x1b, ple_gate, x2d, t, ssq, norm_post, e, layer, tm=512, tn=1024):
    m, k = x1b.shape
    n = ple_gate.shape[-1]
    n_col, n_row = n // tn, m // tm
    piece = k // n_row
    tile = pl.BlockSpec((tm, tn), lambda jj, i: (_row_of(jj, i), _col_of(jj)))
    return pl.pallas_call(
        _plegate_body,
        grid=(n_col + 1, n_row),
        in_specs=[
            pl.BlockSpec((tm, k), lambda jj, i: (_row_of(jj, i), 0)),
            pl.BlockSpec((None, piece, tn), lambda jj, i: (layer, i, _cast_col_of(jj, n_col))),
            tile, tile,
            pl.BlockSpec((tm, LANES), lambda jj, i: (_row_of(jj, i), 0)),
            pl.BlockSpec((None, 1, tn), lambda jj, i: (layer, 0, _col_of(jj))),
            tile,
        ],
        out_specs=tile,
        out_shape=jax.ShapeDtypeStruct((m, n), F32),
        scratch_shapes=[pltpu.VMEM((2, k, tn), BF16)],
        compiler_params=_params("arbitrary", "arbitrary"),
        name="ple_gate",
    )(x1b, ple_gate, x2d, t, ssq, norm_post, e)


def kernel(x, p, positions, norm_pre, w_in, gmlp_ln_g, gmlp_ln_b, gmlp_ws, gmlp_bs, attn_sinks,
           mlstm_ib, mlstm_fb, mlstm_norm_g, w_branch, w_out, norm_post, ple_proj, ple_norm,
           ple_gate):
    batch, seq, d = x.shape
    depth = w_in.shape[0]
    m = batch * seq
    assert d == D_MODEL and w_in.shape[-1] == N_IN and seq % 512 == 0 and m % TM == 0

    inv_freq = ROPE_THETA ** (-jnp.arange(0, ROT_DIM, 2, dtype=F32) / ROT_DIM)
    lane_d = jnp.arange(LANES) % SWA_HEAD_DIM
    inv_lane = jnp.where(lane_d < ROT_DIM, inv_freq[lane_d % (ROT_DIM // 2)], 0.0).reshape(1, LANES)
    posf = positions.astype(F32).reshape(m, 1)
    p3d = p.reshape(depth, m, PLE_DIM)
    bst = jnp.swapaxes(gmlp_bs, 1, 2)
    gate_bias = jnp.pad(jnp.concatenate([mlstm_ib, mlstm_fb], axis=1).astype(F32),
                        ((0, 0), (0, LANES - GATE_COLS)))

    def rows3(a):
        return a.reshape(depth, 1, a.shape[-1])

    norm_pre, norm_post, ple_norm = rows3(norm_pre), rows3(norm_post), rows3(ple_norm)
    gmlp_ln_g, gmlp_ln_b = rows3(gmlp_ln_g), rows3(gmlp_ln_b)
    mlstm_norm_g = rows3(mlstm_norm_g)
    rope_tab = _rope_call(posf, inv_lane)

    w_t = jnp.swapaxes(w_in, 1, 2)

    xc = x.reshape(m, d)
    for l in range(depth):
        h, gcol, gtr = _prenorm(xc, norm_pre, w_t, gate_bias[l:l + 1], l, batch, seq)
        proj = _proj(h, w_t, l)

        ya = _gmlp(proj, gmlp_ln_g, gmlp_ln_b, gmlp_ws, bst, l)
        yb = _swa(proj, rope_tab, attn_sinks, l, batch, seq)
        yc = _mlstm(proj, gcol, gtr, mlstm_norm_g, l, batch, seq)

        mixed = _merge(ya, yb, yc, w_branch, proj, l)
        t, ssq = _outproj(mixed, w_out, l)
        x1b = _postnorm(xc, t, ssq, norm_post, l)
        e = _ple(p3d, ple_proj, ple_norm, l)
        xc = _plegate(x1b, ple_gate, xc, t, ssq, norm_post, e, l)
    return xc.reshape(batch, seq, d)
```

```python
import functools

import jax
import jax.numpy as jnp
from jax import lax
from jax.experimental import pallas as pl
from jax.experimental.pallas import tpu as pltpu

F32 = jnp.float32
BF16 = jnp.bfloat16

D_MODEL = 4096
PLE_DIM = 256
N_BRANCHES = 3
BRANCH_WIDTH = D_MODEL // 2
GMLP_CHUNK = 128
GMLP_GROUPS = 8
GMLP_GROUP_WIDTH = BRANCH_WIDTH // GMLP_GROUPS
SWA_HEAD_DIM = 64
SWA_Q_HEADS = BRANCH_WIDTH // SWA_HEAD_DIM
SWA_KV_HEADS = 4
SWA_GROUP = SWA_Q_HEADS // SWA_KV_HEADS
SWA_BLOCK = 128
SWA_KV_WIDTH = SWA_KV_HEADS * SWA_HEAD_DIM
ROT_DIM = SWA_HEAD_DIM // 4
ROPE_THETA = 500000.0
MLSTM_HEADS = 4
MLSTM_V_DIM = BRANCH_WIDTH // MLSTM_HEADS
MLSTM_QK_DIM = MLSTM_V_DIM // 2
MLSTM_CHUNK = 64
GATE_SOFTCAP = 15.0
NORM_EPS = 1e-6
NEG_INF = -1e30

LANES = 128
SUBLANES = 8
BF16_ROWS = 16
MXU_DIM = 256
GATE_ROWS = BF16_ROWS
VMEM_LIMIT = 56 * 1024 * 1024
TM = 1024
TN = 1024
TM_ROWWISE = 512

_SPLIT = (
    ("a_u", BRANCH_WIDTH), ("a_v", BRANCH_WIDTH), ("a_z", BRANCH_WIDTH),
    ("b_q", BRANCH_WIDTH), ("b_k", SWA_KV_WIDTH), ("b_v", SWA_KV_WIDTH), ("b_z", BRANCH_WIDTH),
    ("c_q", MLSTM_HEADS * MLSTM_QK_DIM), ("c_k", MLSTM_HEADS * MLSTM_QK_DIM),
    ("c_v", BRANCH_WIDTH), ("c_i", MLSTM_HEADS), ("c_f", MLSTM_HEADS),
    ("c_o", BRANCH_WIDTH), ("c_z", BRANCH_WIDTH), ("gates", N_BRANCHES * D_MODEL),
)
_SRC = {}
_off = 0
for _name, _size in _SPLIT:
    _SRC[_name] = (_off, _size)
    _off += _size
N_IN = _off
GATE_COLS = 2 * MLSTM_HEADS
GATE_START = _SRC["c_i"][0]


def _params(*sem):
    return pltpu.CompilerParams(dimension_semantics=sem, vmem_limit_bytes=VMEM_LIMIT)


def _gelu(x):
    return 0.5 * x * (1.0 + lax.erf(x * (0.5 ** 0.5)))


def _sigmoid(x):
    return jax.nn.sigmoid(x)


def _softcap(z):
    return GATE_SOFTCAP * jnp.tanh(z / GATE_SOFTCAP)


def _log_sigmoid(x):
    return -(jnp.maximum(-x, 0.0) + jnp.log1p(jnp.exp(-jnp.abs(x))))


def _stream_step(wb_ref, cast, compute):
    jj = pl.program_id(0)
    cast(wb_ref.at[jj % 2])

    @pl.when(jj > 0)
    def _():
        compute(wb_ref.at[(jj + 1) % 2])


def _row_of(jj, i):
    return jnp.where(jj > 0, i, 0)


def _col_of(jj):
    return jnp.maximum(jj - 1, 0)


def _cast_col_of(jj, n_col):
    return jnp.minimum(jj, n_col - 1)


def _prenorm_body(x_ref, g_ref, wif_ref, brow_ref, h_ref, gc_ref, gt_ref, wb_ref):
    @pl.when(pl.program_id(0) == 0)
    def _():
        wb_ref[...] = wif_ref[...].astype(BF16)

    x = x_ref[...]
    y = x * lax.rsqrt(jnp.mean(x * x, axis=-1, keepdims=True) + NORM_EPS) * g_ref[...]
    hb = y.astype(BF16)
    h_ref[...] = hb
    pre = lax.dot_general(hb, wb_ref[...], (((1,), (1,)), ((), ())),
                          preferred_element_type=F32) + brow_ref[...]
    sc = _softcap(pre)
    lane = lax.broadcasted_iota(jnp.int32, sc.shape, 1)
    gc = jnp.where(lane >= MLSTM_HEADS, _log_sigmoid(sc), sc)
    gc_ref[...] = gc
    gt_ref[...] = gc.T[:GATE_ROWS, :]


def _prenorm(x2d, norm_pre, w_t, brow, layer, batch, seq, tm=TM_ROWWISE):
    m = x2d.shape[0]
    per_seq = seq // tm
    assert GATE_START % LANES == 0 and seq % tm == 0
    return pl.pallas_call(
        _prenorm_body,
        grid=(m // tm,),
        in_specs=[
            pl.BlockSpec((tm, D_MODEL), lambda i: (i, 0)),
            pl.BlockSpec((None, 1, D_MODEL), lambda i: (layer, 0, 0)),
            pl.BlockSpec((None, LANES, D_MODEL), lambda i: (layer, GATE_START // LANES, 0)),
            pl.BlockSpec((1, LANES), lambda i: (0, 0)),
        ],
        out_specs=[
            pl.BlockSpec((tm, D_MODEL), lambda i: (i, 0)),
            pl.BlockSpec((tm, LANES), lambda i: (i, 0)),
            pl.BlockSpec((None, GATE_ROWS, tm), lambda i: (i // per_seq, 0, i % per_seq)),
        ],
        out_shape=[
            jax.ShapeDtypeStruct((m, D_MODEL), BF16),
            jax.ShapeDtypeStruct((m, LANES), F32),
            jax.ShapeDtypeStruct((batch, GATE_ROWS, seq), F32),
        ],
        scratch_shapes=[pltpu.VMEM((LANES, D_MODEL), BF16)],
        compiler_params=_params("arbitrary"),
        name="prenorm",
    )(x2d, norm_pre, w_t, brow)


_NT = (((1,), (1,)), ((), ()))

_PROJ_ORDER = (
    ("a_u", "gelu"), ("a_v", "gelu"), ("a_z", "silu"), ("b_q", "linear"), ("b_z", "silu"),
    ("c_q", "linear"), ("c_k", "linear"), ("c_v", "linear"), ("c_o", "sigmoid"), ("c_z", "silu"),
    ("gates", "sigmoid"), ("b_k", "linear"),
)


def _proj_layout():
    tiles, cols, col = [], {}, 0
    for name, kind in _PROJ_ORDER:
        start, width = _SRC[name]
        cols[name] = col
        n_tiles = -(-width // TN)
        tiles += [(start + i * TN, kind) for i in range(n_tiles)]
        col += n_tiles * TN
    cols["b_v"] = cols["b_k"] + _SRC["b_k"][1]
    assert all(s % SUBLANES == 0 and s + TN <= N_IN for s, _ in tiles)
    return tuple(tiles), cols, col


_PROJ_TILES, PROJ_COL, PROJ_WIDTH = _proj_layout()


def _proj_block(name, width):
    assert PROJ_COL[name] % width == 0
    return PROJ_COL[name] // width


_N_GELU = sum(kind == "gelu" for _, kind in _PROJ_TILES)
assert all(kind == "gelu" for _, kind in _PROJ_TILES[:_N_GELU])


def _tile_is(t, kind):
    hit = False
    for idx, (_, k) in enumerate(_PROJ_TILES):
        if k == kind:
            hit = (t == idx) | hit
    return hit


def _tile_src_start(t):
    start = _PROJ_TILES[0][0] + t * TN
    for idx in range(1, len(_PROJ_TILES)):
        if _PROJ_TILES[idx][0] != _PROJ_TILES[idx - 1][0] + TN:
            start = jnp.where(t >= idx, _PROJ_TILES[idx][0] + (t - idx) * TN, start)
    return start


def _proj_body(h_ref, wp_ref, o_ref, wb_ref):
    jj = pl.program_id(0)
    t = jj - 1
    piece = wp_ref.shape[0]
    rows = pl.ds(pl.multiple_of(pl.program_id(1) * piece, piece), piece)
    wb_ref[jj % 2, rows, :] = wp_ref[...].astype(BF16)

    def slabs(act):
        for r0 in range(0, h_ref.shape[0], MXU_DIM):
            acc = lax.dot_general(h_ref[r0:r0 + MXU_DIM, :], wb_ref[(jj + 1) % 2], _NT,
                                  preferred_element_type=F32)
            o_ref[r0:r0 + MXU_DIM, :] = act(acc).astype(o_ref.dtype)

    @pl.when((jj > 0) & (t < _N_GELU))
    def _():
        slabs(_gelu)

    @pl.when(t >= _N_GELU)
    def _():
        is_linear, is_silu = _tile_is(t, "linear"), _tile_is(t, "silu")

        def blended(acc):
            sig = _sigmoid(acc)
            return jnp.where(is_linear, acc, jnp.where(is_silu, acc * sig, sig))

        slabs(blended)


def _proj(h, w_t, layer):
    m, k = h.shape
    n_col, n_row = len(_PROJ_TILES), m // TM
    piece = TN // n_row
    assert TN % n_row == 0 and piece % BF16_ROWS == 0 and TM % MXU_DIM == 0

    def w_index(jj, i):
        src = _tile_src_start(_cast_col_of(jj, n_col))
        return layer, pl.multiple_of(src + i * piece, SUBLANES), 0

    return pl.pallas_call(
        _proj_body,
        grid=(n_col + 1, n_row),
        in_specs=[
            pl.BlockSpec((TM, k), lambda jj, i: (_row_of(jj, i), 0)),
            pl.BlockSpec((None, pl.Element(piece), pl.Element(k)), w_index),
        ],
        out_specs=pl.BlockSpec((TM, TN), lambda jj, i: (_row_of(jj, i), _col_of(jj))),
        out_shape=jax.ShapeDtypeStruct((m, PROJ_WIDTH), BF16),
        scratch_shapes=[pltpu.VMEM((2, TN, k), BF16)],
        compiler_params=_params("arbitrary", "arbitrary"),
        name="proj",
    )(h, w_t)


def _gmlp_body(chunks, u_ref, v_ref, z_ref, lg_ref, lb_ref, ws_ref, bst_ref, o_ref):
    r = lax.broadcasted_iota(jnp.int32, (GMLP_CHUNK, GMLP_CHUNK), 0)
    c = lax.broadcasted_iota(jnp.int32, (GMLP_CHUNK, GMLP_CHUNK), 1)
    tri = r >= c
    wmix = [jnp.where(tri, ws_ref[g], 0.0).astype(BF16) for g in range(GMLP_GROUPS)]
    for ch in range(chunks):
        rows = pl.ds(ch * GMLP_CHUNK, GMLP_CHUNK)
        v = v_ref[rows, :].astype(F32)
        mu = jnp.mean(v, axis=-1, keepdims=True)
        d = v - mu
        var = jnp.mean(d * d, axis=-1, keepdims=True)
        vn = (d * lax.rsqrt(var + NORM_EPS) * lg_ref[...] + lb_ref[...]).astype(BF16)
        for g in range(GMLP_GROUPS):
            cols = pl.ds(g * GMLP_GROUP_WIDTH, GMLP_GROUP_WIDTH)
            mixed = jnp.dot(wmix[g], vn[:, g * GMLP_GROUP_WIDTH:(g + 1) * GMLP_GROUP_WIDTH],
                            preferred_element_type=F32) + bst_ref[:, g:g + 1]
            y = u_ref[rows, cols].astype(F32) * mixed * z_ref[rows, cols].astype(F32)
            o_ref[rows, cols] = y.astype(o_ref.dtype)


def _gmlp(proj, ln_g, ln_b, ws, bst, layer, ts=512):
    m = proj.shape[0]
    w = BRANCH_WIDTH
    ublk, vblk, zblk = (_proj_block(n, w) for n in ("a_u", "a_v", "a_z"))
    return pl.pallas_call(
        functools.partial(_gmlp_body, ts // GMLP_CHUNK),
        grid=(m // ts,),
        in_specs=[
            pl.BlockSpec((ts, w), lambda i: (i, ublk)),
            pl.BlockSpec((ts, w), lambda i: (i, vblk)),
            pl.BlockSpec((ts, w), lambda i: (i, zblk)),
            pl.BlockSpec((None, 1, w), lambda i: (layer, 0, 0)),
            pl.BlockSpec((None, 1, w), lambda i: (layer, 0, 0)),
            pl.BlockSpec((None, GMLP_GROUPS, GMLP_CHUNK, GMLP_CHUNK), lambda i: (layer, 0, 0, 0)),
            pl.BlockSpec((None, GMLP_CHUNK, GMLP_GROUPS), lambda i: (layer, 0, 0)),
        ],
        out_specs=pl.BlockSpec((ts, w), lambda i: (i, 0)),
        out_shape=jax.ShapeDtypeStruct((m, w), BF16),
        compiler_params=_params("arbitrary"),
        name="gmlp",
    )(proj, proj, proj, ln_g, ln_b, ws, bst)


def _rope_tables(pos_ref, inv_ref):
    ang = pos_ref[...] * inv_ref[...]
    d = lax.broadcasted_iota(jnp.int32, ang.shape, 1) % SWA_HEAD_DIM
    half = ROT_DIM // 2
    cosf = jnp.where(d < ROT_DIM, jnp.cos(ang), 1.0)
    sinf = jnp.sin(ang)
    s_lo = jnp.where(d < half, -sinf, 0.0)
    s_hi = jnp.where((d >= half) & (d < ROT_DIM), sinf, 0.0)
    return cosf, s_lo, s_hi


def _rope_tile(t, tables):
    cosf, s_lo, s_hi = tables
    half = ROT_DIM // 2
    up = pltpu.roll(t, LANES - half, 1)
    dn = pltpu.roll(t, half, 1)
    return t * cosf + up * s_lo + dn * s_hi


def _rope_body(pos_ref, inv_ref, o_ref):
    cosf, s_lo, s_hi = _rope_tables(pos_ref, inv_ref)
    o_ref[0] = cosf
    o_ref[1] = s_lo
    o_ref[2] = s_hi


def _rope_call(posf, inv_lane, tm=512):
    m = posf.shape[0]
    return pl.pallas_call(
        _rope_body,
        grid=(m // tm,),
        in_specs=[pl.BlockSpec((tm, 1), lambda i: (i, 0)),
                  pl.BlockSpec((1, LANES), lambda i: (0, 0))],
        out_specs=pl.BlockSpec((3, tm, LANES), lambda i: (0, i, 0)),
        out_shape=jax.ShapeDtypeStruct((3, m, LANES), F32),
        compiler_params=_params("arbitrary"),
        name="rope_tables",
    )(posf, inv_lane)


def _swa_body(layer, q_ref, kc_ref, kp_ref, vc_ref, vp_ref, z_ref, tc_ref, tp_ref, sink_ref, o_ref):
    n = pl.program_id(1)
    tab_c = (tc_ref[0], tc_ref[1], tc_ref[2])
    tab_p = (tp_ref[0], tp_ref[1], tp_ref[2])
    heads_per_tile = LANES // SWA_HEAD_DIM
    tiles_per_group = SWA_GROUP // heads_per_tile
    rows = tiles_per_group * SWA_BLOCK

    lane_half = lax.broadcasted_iota(jnp.int32, (2 * SWA_BLOCK, LANES), 1) // SWA_HEAD_DIM
    key_row = lax.broadcasted_iota(jnp.int32, (2 * SWA_BLOCK, LANES), 0)
    qi = lax.broadcasted_iota(jnp.int32, (rows, LANES), 0) % SWA_BLOCK
    kj = lax.broadcasted_iota(jnp.int32, (rows, LANES), 1)
    mask_prev = (kj > qi) & (n > 0)
    mask_cur = kj <= qi
    sink_slot = lax.broadcasted_iota(jnp.int32, (SWA_BLOCK, LANES), 1) == 0

    scale = SWA_HEAD_DIM ** -0.5
    batches = []
    for kt in range(SWA_KV_WIDTH // LANES):
        cols = pl.ds(kt * LANES, LANES)
        k_band = jnp.concatenate(
            [_rope_tile(kp_ref[:, cols].astype(F32), tab_p),
             _rope_tile(kc_ref[:, cols].astype(F32), tab_c)], axis=0)
        v_band = jnp.concatenate([vp_ref[:, cols], vc_ref[:, cols]], axis=0).astype(F32)
        v_band = jnp.where(key_row == 0, 0.0, v_band)
        for hh in range(heads_per_tile):
            kvh = kt * heads_per_tile + hh
            k_keep = jnp.where(lane_half == hh, k_band, 0.0)
            v_keep = jnp.where(lane_half == hh, v_band, 0.0)
            k_swap = pltpu.roll(k_keep, SWA_HEAD_DIM, 1)
            v_swap = pltpu.roll(v_keep, SWA_HEAD_DIM, 1)
            k_at = [k_keep.astype(BF16), k_swap.astype(BF16)]
            v_at = [v_keep.astype(BF16), v_swap.astype(BF16)]
            if hh == 1:
                k_at.reverse()
                v_at.reverse()
            tile0 = kvh * tiles_per_group
            q_stack = jnp.concatenate(
                [(_rope_tile(q_ref[:, pl.ds((tile0 + t) * LANES, LANES)].astype(F32), tab_c)
                  * scale).astype(BF16) for t in range(tiles_per_group)], axis=0)
            for qh in range(heads_per_tile):
                fill = jnp.concatenate(
                    [jnp.where(sink_slot,
                               sink_ref[layer, (tile0 + t) * heads_per_tile + qh], NEG_INF)
                     for t in range(tiles_per_group)], axis=0)
                batches.append((tile0, q_stack, k_at[qh], v_at[qh], fill))

    scores = []
    for _, q_stack, k_rows, _, fill in batches:
        qk = lax.dot_general(q_stack, k_rows, _NT, preferred_element_type=F32)
        scores.append(jnp.concatenate([jnp.where(mask_prev, qk[:, :LANES], fill),
                                       jnp.where(mask_cur, qk[:, LANES:], NEG_INF)], axis=1))
    maxes = [jnp.max(s, axis=-1, keepdims=True) for s in scores]
    probs = [jnp.exp(s - mx) for s, mx in zip(scores, maxes)]
    sums = [jnp.sum(p, axis=-1, keepdims=True) for p in probs]
    outs = [jnp.dot(p.astype(BF16), bt[3], preferred_element_type=F32) / den
            for p, bt, den in zip(probs, batches, sums)]
    for i in range(0, len(batches), heads_per_tile):
        tile0 = batches[i][0]
        o_stack = sum(outs[i + 1:i + heads_per_tile], outs[i])
        for t in range(tiles_per_group):
            qcols = pl.ds((tile0 + t) * LANES, LANES)
            o_t = o_stack[t * SWA_BLOCK:(t + 1) * SWA_BLOCK, :]
            o_ref[:, qcols] = (o_t * z_ref[:, qcols].astype(F32)).astype(o_ref.dtype)


def _swa(proj, rope_tab, sinks, layer, batch, seq):
    m = proj.shape[0]
    nb = seq // SWA_BLOCK
    w = BRANCH_WIDTH
    qblk, zblk = _proj_block("b_q", w), _proj_block("b_z", w)
    kblk, vblk = _proj_block("b_k", SWA_KV_WIDTH), _proj_block("b_v", SWA_KV_WIDTH)

    def cur(b, n):
        return b * nb + n

    def prev(b, n):
        return b * nb + jnp.maximum(n - 1, 0)

    return pl.pallas_call(
        functools.partial(_swa_body, layer),
        grid=(batch, nb),
        in_specs=[
            pl.BlockSpec((SWA_BLOCK, w), lambda b, n: (cur(b, n), qblk)),
            pl.BlockSpec((SWA_BLOCK, SWA_KV_WIDTH), lambda b, n: (cur(b, n), kblk)),
            pl.BlockSpec((SWA_BLOCK, SWA_KV_WIDTH), lambda b, n: (prev(b, n), kblk)),
            pl.BlockSpec((SWA_BLOCK, SWA_KV_WIDTH), lambda b, n: (cur(b, n), vblk)),
            pl.BlockSpec((SWA_BLOCK, SWA_KV_WIDTH), lambda b, n: (prev(b, n), vblk)),
            pl.BlockSpec((SWA_BLOCK, w), lambda b, n: (cur(b, n), zblk)),
            pl.BlockSpec((3, SWA_BLOCK, LANES), lambda b, n: (0, cur(b, n), 0)),
            pl.BlockSpec((3, SWA_BLOCK, LANES), lambda b, n: (0, prev(b, n), 0)),
            pl.BlockSpec(memory_space=pltpu.SMEM),
        ],
        out_specs=pl.BlockSpec((SWA_BLOCK, w), lambda b, n: (cur(b, n), 0)),
        out_shape=jax.ShapeDtypeStruct((m, w), BF16),
        compiler_params=_params("arbitrary", "arbitrary"),
        name="swa",
    )(proj, proj, proj, proj, proj, proj, rope_tab, rope_tab, sinks)


def _mlstm_body(chunks, nb, q_ref, k_ref, v_ref, o_ref, z_ref, gc_ref, gt_ref, ng_ref, y_ref,
                c_ref, n_ref, m_ref):
    L, DK, DV = MLSTM_CHUNK, MLSTM_QK_DIM, MLSTM_V_DIM

    @pl.when(pl.program_id(1) == 0)
    def _():
        c_ref[...] = jnp.zeros_like(c_ref)
        n_ref[...] = jnp.zeros_like(n_ref)
        m_ref[...] = jnp.zeros_like(m_ref)

    r = lax.broadcasted_iota(jnp.int32, (L, L), 0)
    c = lax.broadcasted_iota(jnp.int32, (L, L), 1)
    causal = r >= c
    lane_g = lax.broadcasted_iota(jnp.int32, (L, LANES), 1)
    row_g = lax.broadcasted_iota(jnp.int32, (GATE_ROWS, L), 0)
    chains = [(bi, h) for bi in range(nb) for h in range(MLSTM_HEADS)]

    for ch in range(chunks):
        rows = pl.ds(ch * L, L)
        gates, cums, stab, qkv, inter, houts = [], [], [], [], [], []
        for bi, h in chains:
            gc = gc_ref[bi, rows, :]
            gt = gt_ref[bi, :, ch * L:(ch + 1) * L]
            i_col = jnp.sum(jnp.where(lane_g == h, gc, 0.0), axis=1, keepdims=True)
            f_col = jnp.sum(jnp.where(lane_g == h + MLSTM_HEADS, gc, 0.0), axis=1, keepdims=True)
            i_row = jnp.sum(jnp.where(row_g == h, gt, 0.0), axis=0, keepdims=True)
            f_row = jnp.sum(jnp.where(row_g == h + MLSTM_HEADS, gt, 0.0), axis=0, keepdims=True)
            gates.append((i_col, f_col, i_row, f_row))
        for (bi, h), (i_col, f_col, i_row, f_row) in zip(chains, gates):
            b_col = jnp.sum(jnp.where(causal, f_row, 0.0), axis=1, keepdims=True)
            b_row = jnp.sum(jnp.where(r <= c, f_col, 0.0), axis=0, keepdims=True)
            g_tot = jnp.sum(f_row, axis=1, keepdims=True)
            cums.append((b_col, b_row, g_tot, m_ref[bi * MLSTM_HEADS + h]))
        for (i_col, f_col, i_row, f_row), (b_col, b_row, g_tot, m_prev) in zip(gates, cums):
            log_d = jnp.where(causal, b_col - b_row + i_row, NEG_INF)
            m_inter = b_col + m_prev
            m_t = jnp.maximum(m_inter, jnp.max(log_d, axis=1, keepdims=True))
            stab.append((m_t, jnp.exp(log_d - m_t), jnp.exp(m_inter - m_t)))
        for bi, h in chains:
            qf = q_ref[bi, rows, h * DK:(h + 1) * DK].astype(F32) * (DK ** -0.5)
            qs = qf.astype(BF16)
            k = k_ref[bi, rows, h * DK:(h + 1) * DK]
            v = v_ref[bi, rows, h * DV:(h + 1) * DV]
            qk = lax.dot_general(qs, k, _NT, preferred_element_type=F32)
            qkv.append((qf, qs, k, v, qk))
        for (bi, h), (qf, qs, k, v, qk), (m_t, dmat, a) in zip(chains, qkv, stab):
            c_prev = c_ref[bi * MLSTM_HEADS + h]
            inter.append((qk * dmat, c_prev,
                          jnp.dot(qs, c_prev.astype(BF16), preferred_element_type=F32)))
        for (bi, h), (qf, qs, k, v, qk), (m_t, dmat, a), (s, c_prev, qc) in zip(
                chains, qkv, stab, inter):
            num = jnp.dot(s.astype(BF16), v, preferred_element_type=F32) + a * qc
            qn = jnp.sum(qf * n_ref[bi * MLSTM_HEADS + h], axis=1, keepdims=True)
            den = jnp.sum(s, axis=1, keepdims=True) + a * qn
            houts.append(num / jnp.maximum(jnp.abs(den), jnp.exp(-m_t)))
        for (bi, h), (i_col, f_col, i_row, f_row), (b_col, b_row, g_tot, m_prev), \
                (qf, qs, k, v, qk), (s, c_prev, qc) in zip(chains, gates, cums, qkv, inter):
            idx = bi * MLSTM_HEADS + h
            w_col = g_tot - b_col + i_col
            w_row = g_tot - b_row + i_row
            m_new = jnp.maximum(g_tot + m_prev, jnp.max(w_row, axis=1, keepdims=True))
            wgt = jnp.exp(w_col - m_new)
            decay = jnp.exp(g_tot + m_prev - m_new)
            kw = k.astype(F32) * wgt
            c_ref[idx] = decay * c_prev + lax.dot_general(
                kw.astype(BF16), v, (((0,), (0,)), ((), ())), preferred_element_type=F32)
            n_ref[idx] = decay * n_ref[idx] + jnp.sum(kw, axis=0, keepdims=True)
            m_ref[idx] = m_new
        for (bi, h), hout in zip(chains, houts):
            vcols = pl.ds(h * DV, DV)
            hn = hout * lax.rsqrt(jnp.mean(hout * hout, axis=1, keepdims=True) + NORM_EPS)
            hn = hn * ng_ref[:, vcols]
            y = hn * o_ref[bi, rows, vcols].astype(F32) * z_ref[bi, rows, vcols].astype(F32)
            y_ref[bi, rows, vcols] = y.astype(y_ref.dtype)


def _mlstm(proj, gcol, gtr, norm_g, layer, batch, seq, ts=128, nb=2):
    m = proj.shape[0]
    nt = seq // ts
    qw, vw = MLSTM_HEADS * MLSTM_QK_DIM, BRANCH_WIDTH
    qblk, kblk = _proj_block("c_q", qw), _proj_block("c_k", qw)
    vblk, oblk, zblk = (_proj_block(n, vw) for n in ("c_v", "c_o", "c_z"))
    proj3 = proj.reshape(batch, seq, proj.shape[-1])
    assert batch % nb == 0

    def tok3(a):
        return a.reshape(batch, seq, a.shape[-1])

    chains = nb * MLSTM_HEADS
    out = pl.pallas_call(
        functools.partial(_mlstm_body, ts // MLSTM_CHUNK, nb),
        grid=(batch // nb, nt),
        in_specs=[
            pl.BlockSpec((nb, ts, qw), lambda b, t: (b, t, qblk)),
            pl.BlockSpec((nb, ts, qw), lambda b, t: (b, t, kblk)),
            pl.BlockSpec((nb, ts, vw), lambda b, t: (b, t, vblk)),
            pl.BlockSpec((nb, ts, vw), lambda b, t: (b, t, oblk)),
            pl.BlockSpec((nb, ts, vw), lambda b, t: (b, t, zblk)),
            pl.BlockSpec((nb, ts, LANES), lambda b, t: (b, t, 0)),
            pl.BlockSpec((nb, GATE_ROWS, ts), lambda b, t: (b, 0, t)),
            pl.BlockSpec((None, 1, vw), lambda b, t: (layer, 0, 0)),
        ],
        out_specs=pl.BlockSpec((nb, ts, vw), lambda b, t: (b, t, 0)),
        out_shape=jax.ShapeDtypeStruct((batch, seq, BRANCH_WIDTH), BF16),
        scratch_shapes=[
            pltpu.VMEM((chains, MLSTM_QK_DIM, MLSTM_V_DIM), F32),
            pltpu.VMEM((chains, 1, MLSTM_QK_DIM), F32),
            pltpu.VMEM((chains, 1, 1), F32),
        ],
        compiler_params=_params("arbitrary", "arbitrary"),
        name="mlstm",
    )(proj3, proj3, proj3, proj3, proj3, tok3(gcol), gtr, norm_g)
    return out.reshape(m, BRANCH_WIDTH)


def _merge_body(ya_ref, yb_ref, yc_ref, wp_ref, g0_ref, g1_ref, g2_ref, o_ref, wb_ref):
    piece = wp_ref.shape[1]
    rows = pl.ds(pl.multiple_of(pl.program_id(1) * piece, piece), piece)

    def cast(dst):
        dst[:, rows, :] = wp_ref[...].astype(BF16)

    def compute(src):
        acc = g0_ref[...].astype(F32) * jnp.dot(ya_ref[...], src[0], preferred_element_type=F32)
        acc = acc + g1_ref[...].astype(F32) * jnp.dot(yb_ref[...], src[1],
                                                      preferred_element_type=F32)
        acc = acc + g2_ref[...].astype(F32) * jnp.dot(yc_ref[...], src[2],
                                                      preferred_element_type=F32)
        o_ref[...] = acc.astype(o_ref.dtype)

    _stream_step(wb_ref, cast, compute)


def _merge(ya, yb, yc, w_branch, proj, layer, tm=512, tn=1024):
    m = ya.shape[0]
    w = BRANCH_WIDTH
    n_col, n_row = D_MODEL // tn, m // tm
    piece = w // n_row
    gate0 = PROJ_COL["gates"] // tn
    assert PROJ_COL["gates"] % tn == 0
    yspec = pl.BlockSpec((tm, w), lambda jj, i: (_row_of(jj, i), 0))

    def gate_spec(branch):
        return pl.BlockSpec(
            (tm, tn), lambda jj, i: (_row_of(jj, i), gate0 + branch * n_col + _col_of(jj)))

    return pl.pallas_call(
        _merge_body,
        grid=(n_col + 1, n_row),
        in_specs=[
            yspec, yspec, yspec,
            pl.BlockSpec((None, N_BRANCHES, piece, tn),
                         lambda jj, i: (layer, 0, i, _cast_col_of(jj, n_col))),
            gate_spec(0), gate_spec(1), gate_spec(2),
        ],
        out_specs=pl.BlockSpec((tm, tn), lambda jj, i: (_row_of(jj, i), _col_of(jj))),
        out_shape=jax.ShapeDtypeStruct((m, D_MODEL), BF16),
        scratch_shapes=[pltpu.VMEM((2, N_BRANCHES, w, tn), BF16)],
        compiler_params=_params("arbitrary", "arbitrary"),
        name="merge",
    )(ya, yb, yc, w_branch, proj, proj, proj)


def _outproj_body(a_ref, wp_ref, t_ref, ssq_ref, wb_ref):
    jj = pl.program_id(0)
    i = pl.program_id(1)
    piece = wp_ref.shape[0]
    rows = pl.ds(pl.multiple_of(i * piece, piece), piece)

    def cast(dst):
        dst[rows, :] = wp_ref[...].astype(BF16)

    def compute(src):
        acc = jnp.dot(a_ref[...], src[...], preferred_element_type=F32)
        t_ref[...] = acc.astype(t_ref.dtype)
        out_rows = pl.ds(pl.multiple_of(i * TM, TM), TM)
        part = jnp.broadcast_to(jnp.sum(acc * acc, axis=-1, keepdims=True), (TM, LANES))

        @pl.when(jj == 1)
        def _():
            ssq_ref[out_rows, :] = part

        @pl.when(jj > 1)
        def _():
            ssq_ref[out_rows, :] += part

    _stream_step(wb_ref, cast, compute)


def _outproj(a, w_out, layer, tn=1024):
    m, k = a.shape
    n = w_out.shape[-1]
    n_col, n_row = n // tn, m // TM
    piece = k // n_row
    return pl.pallas_call(
        _outproj_body,
        grid=(n_col + 1, n_row),
        in_specs=[
            pl.BlockSpec((TM, k), lambda jj, i: (_row_of(jj, i), 0)),
            pl.BlockSpec((None, piece, tn), lambda jj, i: (layer, i, _cast_col_of(jj, n_col))),
        ],
        out_specs=[
            pl.BlockSpec((TM, tn), lambda jj, i: (_row_of(jj, i), _col_of(jj))),
            pl.BlockSpec((m, LANES), lambda jj, i: (0, 0)),
        ],
        out_shape=[
            jax.ShapeDtypeStruct((m, n), BF16),
            jax.ShapeDtypeStruct((m, LANES), F32),
        ],
        scratch_shapes=[pltpu.VMEM((2, k, tn), BF16)],
        compiler_params=_params("arbitrary", "arbitrary"),
        name="outproj",
    )(a, w_out)


def _post_residual(x, t, ssq, g):
    rs = lax.rsqrt(ssq * (1.0 / D_MODEL) + NORM_EPS)
    return x + t.astype(F32) * rs * g


def _postnorm_body(x_ref, t_ref, ssq_ref, g_ref, x1b_ref):
    x1b_ref[...] = _post_residual(x_ref[...], t_ref[...], ssq_ref[:, 0:1], g_ref[...]).astype(BF16)


def _postnorm(x2d, t, ssq, norm_post, layer, tm=TM_ROWWISE):
    m = x2d.shape[0]
    return pl.pallas_call(
        _postnorm_body,
        grid=(m // tm,),
        in_specs=[
            pl.BlockSpec((tm, D_MODEL), lambda i: (i, 0)),
            pl.BlockSpec((tm, D_MODEL), lambda i: (i, 0)),
            pl.BlockSpec((tm, LANES), lambda i: (i, 0)),
            pl.BlockSpec((None, 1, D_MODEL), lambda i: (layer, 0, 0)),
        ],
        out_specs=pl.BlockSpec((tm, D_MODEL), lambda i: (i, 0)),
        out_shape=jax.ShapeDtypeStruct((m, D_MODEL), BF16),
        compiler_params=_params("arbitrary"),
        name="postnorm",
    )(x2d, t, ssq, norm_post)


def _ple_body(p_ref, w_ref, g_ref, e_ref, wb_ref):
    @pl.when(pl.program_id(0) == 0)
    def _():
        wb_ref[...] = w_ref[...].astype(BF16)

    t = jnp.dot(p_ref[...].astype(BF16), wb_ref[...], preferred_element_type=F32)
    e = t * lax.rsqrt(jnp.mean(t * t, axis=-1, keepdims=True) + NORM_EPS) * g_ref[...]
    e_ref[...] = e.astype(e_ref.dtype)


def _ple(p3d, ple_proj, ple_norm, layer, tm=TM_ROWWISE):
    m = p3d.shape[1]
    return pl.pallas_call(
        _ple_body,
        grid=(m // tm,),
        in_specs=[
            pl.BlockSpec((None, tm, PLE_DIM), lambda i: (layer, i, 0)),
            pl.BlockSpec((None, PLE_DIM, D_MODEL), lambda i: (layer, 0, 0)),
            pl.BlockSpec((None, 1, D_MODEL), lambda i: (layer, 0, 0)),
        ],
        out_specs=pl.BlockSpec((tm, D_MODEL), lambda i: (i, 0)),
        out_shape=jax.ShapeDtypeStruct((m, D_MODEL), BF16),
        scratch_shapes=[pltpu.VMEM((PLE_DIM, D_MODEL), BF16)],
        compiler_params=_params("arbitrary"),
        name="ple_embed",
    )(p3d, ple_proj, ple_norm)


def _plegate_body(a_ref, wp_ref, x_ref, t_ref, ssq_ref, g_ref, e_ref, o_ref, wb_ref):
    piece = wp_ref.shape[0]
    rows = pl.ds(pl.multiple_of(pl.program_id(1) * piece, piece), piece)

    def cast(dst):
        dst[rows, :] = wp_ref[...].astype(BF16)

    def compute(src):
        acc = jnp.dot(a_ref[...], src[...], preferred_element_type=F32)
        x1 = _post_residual(x_ref[...], t_ref[...], ssq_ref[:, 0:1], g_ref[...])
        o_ref[...] = x1 + _sigmoid(acc) * e_ref[...].astype(F32)

    _stream_step(wb_ref, cast, compute)


def _plegate(x1b, ple_gate, x2d, t, ssq, norm_post, e, layer, tm=512, tn=1024):
    m, k = x1b.shape
    n = ple_gate.shape[-1]
    n_col, n_row = n // tn, m // tm
    piece = k // n_row
    tile = pl.BlockSpec((tm, tn), lambda jj, i: (_row_of(jj, i), _col_of(jj)))
    return pl.pallas_call(
        _plegate_body,
        grid=(n_col + 1, n_row),
        in_specs=[
            pl.BlockSpec((tm, k), lambda jj, i: (_row_of(jj, i), 0)),
            pl.BlockSpec((None, piece, tn), lambda jj, i: (layer, i, _cast_col_of(jj, n_col))),
            tile, tile,
            pl.BlockSpec((tm, LANES), lambda jj, i: (_row_of(jj, i), 0)),
            pl.BlockSpec((None, 1, tn), lambda jj, i: (layer, 0, _col_of(jj))),
            tile,
        ],
        out_specs=tile,
        out_shape=jax.ShapeDtypeStruct((m, n), F32),
        scratch_shapes=[pltpu.VMEM((2, k, tn), BF16)],
        compiler_params=_params("arbitrary", "arbitrary"),
        name="ple_gate",
    )(x1b, ple_gate, x2d, t, ssq, norm_post, e)


def kernel(x, p, positions, norm_pre, w_in, gmlp_ln_g, gmlp_ln_b, gmlp_ws, gmlp_bs, attn_sinks,
           mlstm_ib, mlstm_fb, mlstm_norm_g, w_branch, w_out, norm_post, ple_proj, ple_norm,
           ple_gate):
    batch, seq, d = x.shape
    depth = w_in.shape[0]
    m = batch * seq
    assert d == D_MODEL and w_in.shape[-1] == N_IN and seq % 512 == 0 and m % TM == 0

    inv_freq = ROPE_THETA ** (-jnp.arange(0, ROT_DIM, 2, dtype=F32) / ROT_DIM)
    lane_d = jnp.arange(LANES) % SWA_HEAD_DIM
    inv_lane = jnp.where(lane_d < ROT_DIM, inv_freq[lane_d % (ROT_DIM // 2)], 0.0).reshape(1, LANES)
    posf = positions.astype(F32).reshape(m, 1)
    p3d = p.reshape(depth, m, PLE_DIM)
    bst = jnp.swapaxes(gmlp_bs, 1, 2)
    gate_bias = jnp.pad(jnp.concatenate([mlstm_ib, mlstm_fb], axis=1).astype(F32),
                        ((0, 0), (0, LANES - GATE_COLS)))

    def rows3(a):
        return a.reshape(depth, 1, a.shape[-1])

    norm_pre, norm_post, ple_norm = rows3(norm_pre), rows3(norm_post), rows3(ple_norm)
    gmlp_ln_g, gmlp_ln_b = rows3(gmlp_ln_g), rows3(gmlp_ln_b)
    mlstm_norm_g = rows3(mlstm_norm_g)
    rope_tab = _rope_call(posf, inv_lane)

    w_t = jnp.swapaxes(w_in, 1, 2)

    xc = x.reshape(m, d)
    for l in range(depth):
        h, gcol, gtr = _prenorm(xc, norm_pre, w_t, gate_bias[l:l + 1], l, batch, seq)
        proj = _proj(h, w_t, l)

        ya = _gmlp(proj, gmlp_ln_g, gmlp_ln_b, gmlp_ws, bst, l)
        yb = _swa(proj, rope_tab, attn_sinks, l, batch, seq)
        yc = _mlstm(proj, gcol, gtr, mlstm_norm_g, l, batch, seq)

        mixed = _merge(ya, yb, yc, w_branch, proj, l)
        t, ssq = _outproj(mixed, w_out, l)
        x1b = _postnorm(xc, t, ssq, norm_post, l)
        e = _ple(p3d, ple_proj, ple_norm, l)
        xc = _plegate(x1b, ple_gate, xc, t, ssq, norm_post, e, l)
    return xc.reshape(batch, seq, d)
```

```python
import functools

import jax
import jax.numpy as jnp
from jax import lax
from jax.experimental import pallas as pl
from jax.experimental.pallas import tpu as pltpu

F32 = jnp.float32
BF16 = jnp.bfloat16

D_MODEL = 4096
PLE_DIM = 256
N_BRANCHES = 3
BRANCH_WIDTH = D_MODEL // 2
GMLP_CHUNK = 128
GMLP_GROUPS = 8
GMLP_GROUP_WIDTH = BRANCH_WIDTH // GMLP_GROUPS
SWA_HEAD_DIM = 64
SWA_Q_HEADS = BRANCH_WIDTH // SWA_HEAD_DIM
SWA_KV_HEADS = 4
SWA_GROUP = SWA_Q_HEADS // SWA_KV_HEADS
SWA_BLOCK = 128
SWA_KV_WIDTH = SWA_KV_HEADS * SWA_HEAD_DIM
ROT_DIM = SWA_HEAD_DIM // 4
ROPE_THETA = 500000.0
MLSTM_HEADS = 4
MLSTM_V_DIM = BRANCH_WIDTH // MLSTM_HEADS
MLSTM_QK_DIM = MLSTM_V_DIM // 2
MLSTM_CHUNK = 64
GATE_SOFTCAP = 15.0
NORM_EPS = 1e-6
NEG_INF = -1e30

LANES = 128
SUBLANES = 8
BF16_ROWS = 16
GATE_ROWS = BF16_ROWS
VMEM_LIMIT = 56 * 1024 * 1024
TM = 1024
TN = 1024
TM_ROWWISE = 512

_SPLIT = (
    ("a_u", BRANCH_WIDTH), ("a_v", BRANCH_WIDTH), ("a_z", BRANCH_WIDTH),
    ("b_q", BRANCH_WIDTH), ("b_k", SWA_KV_WIDTH), ("b_v", SWA_KV_WIDTH), ("b_z", BRANCH_WIDTH),
    ("c_q", MLSTM_HEADS * MLSTM_QK_DIM), ("c_k", MLSTM_HEADS * MLSTM_QK_DIM),
    ("c_v", BRANCH_WIDTH), ("c_i", MLSTM_HEADS), ("c_f", MLSTM_HEADS),
    ("c_o", BRANCH_WIDTH), ("c_z", BRANCH_WIDTH), ("gates", N_BRANCHES * D_MODEL),
)
_SRC = {}
_off = 0
for _name, _size in _SPLIT:
    _SRC[_name] = (_off, _size)
    _off += _size
N_IN = _off
GATE_COLS = 2 * MLSTM_HEADS
GATE_START = _SRC["c_i"][0]


def _params(*sem):
    return pltpu.CompilerParams(dimension_semantics=sem, vmem_limit_bytes=VMEM_LIMIT)


def _gelu(x):
    return 0.5 * x * (1.0 + lax.erf(x * (0.5 ** 0.5)))


def _sigmoid(x):
    return jax.nn.sigmoid(x)


def _softcap(z):
    return GATE_SOFTCAP * jnp.tanh(z / GATE_SOFTCAP)


def _log_sigmoid(x):
    return -(jnp.maximum(-x, 0.0) + jnp.log1p(jnp.exp(-jnp.abs(x))))


def _stream_step(wb_ref, cast, compute):
    jj = pl.program_id(0)
    cast(wb_ref.at[jj % 2])

    @pl.when(jj > 0)
    def _():
        compute(wb_ref.at[(jj + 1) % 2])


def _row_of(jj, i):
    return jnp.where(jj > 0, i, 0)


def _col_of(jj):
    return jnp.maximum(jj - 1, 0)


def _cast_col_of(jj, n_col):
    return jnp.minimum(jj, n_col - 1)


def _prenorm_body(x_ref, g_ref, wif_ref, brow_ref, h_ref, gc_ref, gt_ref, wb_ref):
    @pl.when(pl.program_id(0) == 0)
    def _():
        wb_ref[...] = wif_ref[...].astype(BF16)

    x = x_ref[...]
    y = x * lax.rsqrt(jnp.mean(x * x, axis=-1, keepdims=True) + NORM_EPS) * g_ref[...]
    hb = y.astype(BF16)
    h_ref[...] = hb
    pre = lax.dot_general(hb, wb_ref[...], (((1,), (1,)), ((), ())),
                          preferred_element_type=F32) + brow_ref[...]
    sc = _softcap(pre)
    lane = lax.broadcasted_iota(jnp.int32, sc.shape, 1)
    gc = jnp.where(lane >= MLSTM_HEADS, _log_sigmoid(sc), sc)
    gc_ref[...] = gc
    gt_ref[...] = gc.T[:GATE_ROWS, :]


def _prenorm(x2d, norm_pre, w_t, brow, layer, batch, seq, tm=TM_ROWWISE):
    m = x2d.shape[0]
    per_seq = seq // tm
    assert GATE_START % LANES == 0 and seq % tm == 0
    return pl.pallas_call(
        _prenorm_body,
        grid=(m // tm,),
        in_specs=[
            pl.BlockSpec((tm, D_MODEL), lambda i: (i, 0)),
            pl.BlockSpec((None, 1, D_MODEL), lambda i: (layer, 0, 0)),
            pl.BlockSpec((None, LANES, D_MODEL), lambda i: (layer, GATE_START // LANES, 0)),
            pl.BlockSpec((1, LANES), lambda i: (0, 0)),
        ],
        out_specs=[
            pl.BlockSpec((tm, D_MODEL), lambda i: (i, 0)),
            pl.BlockSpec((tm, LANES), lambda i: (i, 0)),
            pl.BlockSpec((None, GATE_ROWS, tm), lambda i: (i // per_seq, 0, i % per_seq)),
        ],
        out_shape=[
            jax.ShapeDtypeStruct((m, D_MODEL), BF16),
            jax.ShapeDtypeStruct((m, LANES), F32),
            jax.ShapeDtypeStruct((batch, GATE_ROWS, seq), F32),
        ],
        scratch_shapes=[pltpu.VMEM((LANES, D_MODEL), BF16)],
        compiler_params=_params("arbitrary"),
        name="prenorm",
    )(x2d, norm_pre, w_t, brow)


_NT = (((1,), (1,)), ((), ()))

_PROJ_ORDER = (
    ("a_u", "gelu"), ("a_v", "gelu"), ("a_z", "silu"), ("b_q", "linear"), ("b_z", "silu"),
    ("c_q", "linear"), ("c_k", "linear"), ("c_v", "linear"), ("c_o", "sigmoid"), ("c_z", "silu"),
    ("gates", "sigmoid"), ("b_k", "linear"),
)


def _proj_layout():
    tiles, cols, col = [], {}, 0
    for name, kind in _PROJ_ORDER:
        start, width = _SRC[name]
        cols[name] = col
        n_tiles = -(-width // TN)
        tiles += [(start + i * TN, kind) for i in range(n_tiles)]
        col += n_tiles * TN
    cols["b_v"] = cols["b_k"] + _SRC["b_k"][1]
    assert all(s % SUBLANES == 0 and s + TN <= N_IN for s, _ in tiles)
    return tuple(tiles), cols, col


_PROJ_TILES, PROJ_COL, PROJ_WIDTH = _proj_layout()


def _proj_block(name, width):
    assert PROJ_COL[name] % width == 0
    return PROJ_COL[name] // width


_N_GELU = sum(kind == "gelu" for _, kind in _PROJ_TILES)
assert all(kind == "gelu" for _, kind in _PROJ_TILES[:_N_GELU])


def _tile_is(t, kind):
    hit = False
    for idx, (_, k) in enumerate(_PROJ_TILES):
        if k == kind:
            hit = (t == idx) | hit
    return hit


def _tile_src_start(t):
    start = _PROJ_TILES[0][0] + t * TN
    for idx in range(1, len(_PROJ_TILES)):
        if _PROJ_TILES[idx][0] != _PROJ_TILES[idx - 1][0] + TN:
            start = jnp.where(t >= idx, _PROJ_TILES[idx][0] + (t - idx) * TN, start)
    return start


def _proj_body(h_ref, wp_ref, o_ref, wb_ref):
    jj = pl.program_id(0)
    t = jj - 1
    piece = wp_ref.shape[0]
    rows = pl.ds(pl.multiple_of(pl.program_id(1) * piece, piece), piece)
    wb_ref[jj % 2, rows, :] = wp_ref[...].astype(BF16)

    def matmul():
        return lax.dot_general(h_ref[...], wb_ref[(jj + 1) % 2], _NT, preferred_element_type=F32)

    @pl.when((jj > 0) & (t < _N_GELU))
    def _():
        o_ref[...] = _gelu(matmul()).astype(o_ref.dtype)

    @pl.when(t >= _N_GELU)
    def _():
        acc = matmul()
        sig = _sigmoid(acc)
        out = jnp.where(_tile_is(t, "linear"), acc, jnp.where(_tile_is(t, "silu"), acc * sig, sig))
        o_ref[...] = out.astype(o_ref.dtype)


def _proj(h, w_t, layer):
    m, k = h.shape
    n_col, n_row = len(_PROJ_TILES), m // TM
    piece = TN // n_row
    assert TN % n_row == 0 and piece % BF16_ROWS == 0

    def w_index(jj, i):
        src = _tile_src_start(_cast_col_of(jj, n_col))
        return layer, pl.multiple_of(src + i * piece, SUBLANES), 0

    return pl.pallas_call(
        _proj_body,
        grid=(n_col + 1, n_row),
        in_specs=[
            pl.BlockSpec((TM, k), lambda jj, i: (_row_of(jj, i), 0)),
            pl.BlockSpec((None, pl.Element(piece), pl.Element(k)), w_index),
        ],
        out_specs=pl.BlockSpec((TM, TN), lambda jj, i: (_row_of(jj, i), _col_of(jj))),
        out_shape=jax.ShapeDtypeStruct((m, PROJ_WIDTH), BF16),
        scratch_shapes=[pltpu.VMEM((2, TN, k), BF16)],
        compiler_params=_params("arbitrary", "arbitrary"),
        name="proj",
    )(h, w_t)


def _gmlp_body(chunks, u_ref, v_ref, z_ref, lg_ref, lb_ref, ws_ref, bst_ref, o_ref):
    r = lax.broadcasted_iota(jnp.int32, (GMLP_CHUNK, GMLP_CHUNK), 0)
    c = lax.broadcasted_iota(jnp.int32, (GMLP_CHUNK, GMLP_CHUNK), 1)
    tri = r >= c
    wmix = [jnp.where(tri, ws_ref[g], 0.0).astype(BF16) for g in range(GMLP_GROUPS)]
    for ch in range(chunks):
        rows = pl.ds(ch * GMLP_CHUNK, GMLP_CHUNK)
        v = v_ref[rows, :].astype(F32)
        mu = jnp.mean(v, axis=-1, keepdims=True)
        d = v - mu
        var = jnp.mean(d * d, axis=-1, keepdims=True)
        vn = (d * lax.rsqrt(var + NORM_EPS) * lg_ref[...] + lb_ref[...]).astype(BF16)
        for g in range(GMLP_GROUPS):
            cols = pl.ds(g * GMLP_GROUP_WIDTH, GMLP_GROUP_WIDTH)
            mixed = jnp.dot(wmix[g], vn[:, g * GMLP_GROUP_WIDTH:(g + 1) * GMLP_GROUP_WIDTH],
                            preferred_element_type=F32) + bst_ref[:, g:g + 1]
            y = u_ref[rows, cols].astype(F32) * mixed * z_ref[rows, cols].astype(F32)
            o_ref[rows, cols] = y.astype(o_ref.dtype)


def _gmlp(proj, ln_g, ln_b, ws, bst, layer, ts=512):
    m = proj.shape[0]
    w = BRANCH_WIDTH
    ublk, vblk, zblk = (_proj_block(n, w) for n in ("a_u", "a_v", "a_z"))
    return pl.pallas_call(
        functools.partial(_gmlp_body, ts // GMLP_CHUNK),
        grid=(m // ts,),
        in_specs=[
            pl.BlockSpec((ts, w), lambda i: (i, ublk)),
            pl.BlockSpec((ts, w), lambda i: (i, vblk)),
            pl.BlockSpec((ts, w), lambda i: (i, zblk)),
            pl.BlockSpec((None, 1, w), lambda i: (layer, 0, 0)),
            pl.BlockSpec((None, 1, w), lambda i: (layer, 0, 0)),
            pl.BlockSpec((None, GMLP_GROUPS, GMLP_CHUNK, GMLP_CHUNK), lambda i: (layer, 0, 0, 0)),
            pl.BlockSpec((None, GMLP_CHUNK, GMLP_GROUPS), lambda i: (layer, 0, 0)),
        ],
        out_specs=pl.BlockSpec((ts, w), lambda i: (i, 0)),
        out_shape=jax.ShapeDtypeStruct((m, w), BF16),
        compiler_params=_params("arbitrary"),
        name="gmlp",
    )(proj, proj, proj, ln_g, ln_b, ws, bst)


def _rope_tables(pos_ref, inv_ref):
    ang = pos_ref[...] * inv_ref[...]
    d = lax.broadcasted_iota(jnp.int32, ang.shape, 1) % SWA_HEAD_DIM
    half = ROT_DIM // 2
    cosf = jnp.where(d < ROT_DIM, jnp.cos(ang), 1.0)
    sinf = jnp.sin(ang)
    s_lo = jnp.where(d < half, -sinf, 0.0)
    s_hi = jnp.where((d >= half) & (d < ROT_DIM), sinf, 0.0)
    return cosf, s_lo, s_hi


def _rope_tile(t, tables):
    cosf, s_lo, s_hi = tables
    half = ROT_DIM // 2
    up = pltpu.roll(t, LANES - half, 1)
    dn = pltpu.roll(t, half, 1)
    return t * cosf + up * s_lo + dn * s_hi


def _rope_body(pos_ref, inv_ref, o_ref):
    cosf, s_lo, s_hi = _rope_tables(pos_ref, inv_ref)
    o_ref[0] = cosf
    o_ref[1] = s_lo
    o_ref[2] = s_hi


def _rope_call(posf, inv_lane, tm=512):
    m = posf.shape[0]
    return pl.pallas_call(
        _rope_body,
        grid=(m // tm,),
        in_specs=[pl.BlockSpec((tm, 1), lambda i: (i, 0)),
                  pl.BlockSpec((1, LANES), lambda i: (0, 0))],
        out_specs=pl.BlockSpec((3, tm, LANES), lambda i: (0, i, 0)),
        out_shape=jax.ShapeDtypeStruct((3, m, LANES), F32),
        compiler_params=_params("arbitrary"),
        name="rope_tables",
    )(posf, inv_lane)


def _swa_body(layer, q_ref, kc_ref, kp_ref, vc_ref, vp_ref, z_ref, tc_ref, tp_ref, sink_ref, o_ref):
    n = pl.program_id(1)
    tab_c = (tc_ref[0], tc_ref[1], tc_ref[2])
    tab_p = (tp_ref[0], tp_ref[1], tp_ref[2])
    heads_per_tile = LANES // SWA_HEAD_DIM
    tiles_per_group = SWA_GROUP // heads_per_tile
    rows = tiles_per_group * SWA_BLOCK

    lane_half = lax.broadcasted_iota(jnp.int32, (2 * SWA_BLOCK, LANES), 1) // SWA_HEAD_DIM
    key_row = lax.broadcasted_iota(jnp.int32, (2 * SWA_BLOCK, LANES), 0)
    qi = lax.broadcasted_iota(jnp.int32, (rows, LANES), 0) % SWA_BLOCK
    kj = lax.broadcasted_iota(jnp.int32, (rows, LANES), 1)
    mask_prev = (kj > qi) & (n > 0)
    mask_cur = kj <= qi
    sink_slot = lax.broadcasted_iota(jnp.int32, (SWA_BLOCK, LANES), 1) == 0

    scale = SWA_HEAD_DIM ** -0.5
    batches = []
    for kt in range(SWA_KV_WIDTH // LANES):
        cols = pl.ds(kt * LANES, LANES)
        k_band = jnp.concatenate(
            [_rope_tile(kp_ref[:, cols].astype(F32), tab_p),
             _rope_tile(kc_ref[:, cols].astype(F32), tab_c)], axis=0)
        v_band = jnp.concatenate([vp_ref[:, cols], vc_ref[:, cols]], axis=0).astype(F32)
        v_band = jnp.where(key_row == 0, 0.0, v_band)
        for hh in range(heads_per_tile):
            kvh = kt * heads_per_tile + hh
            k_keep = jnp.where(lane_half == hh, k_band, 0.0)
            v_keep = jnp.where(lane_half == hh, v_band, 0.0)
            k_swap = pltpu.roll(k_keep, SWA_HEAD_DIM, 1)
            v_swap = pltpu.roll(v_keep, SWA_HEAD_DIM, 1)
            k_at = [k_keep.astype(BF16), k_swap.astype(BF16)]
            v_at = [v_keep.astype(BF16), v_swap.astype(BF16)]
            if hh == 1:
                k_at.reverse()
                v_at.reverse()
            tile0 = kvh * tiles_per_group
            q_stack = jnp.concatenate(
                [(_rope_tile(q_ref[:, pl.ds((tile0 + t) * LANES, LANES)].astype(F32), tab_c)
                  * scale).astype(BF16) for t in range(tiles_per_group)], axis=0)
            for qh in range(heads_per_tile):
                fill = jnp.concatenate(
                    [jnp.where(sink_slot,
                               sink_ref[layer, (tile0 + t) * heads_per_tile + qh], NEG_INF)
                     for t in range(tiles_per_group)], axis=0)
                batches.append((tile0, q_stack, k_at[qh], v_at[qh], fill))

    scores = []
    for _, q_stack, k_rows, _, fill in batches:
        qk = lax.dot_general(q_stack, k_rows, _NT, preferred_element_type=F32)
        scores.append(jnp.concatenate([jnp.where(mask_prev, qk[:, :LANES], fill),
                                       jnp.where(mask_cur, qk[:, LANES:], NEG_INF)], axis=1))
    maxes = [jnp.max(s, axis=-1, keepdims=True) for s in scores]
    probs = [jnp.exp(s - mx) for s, mx in zip(scores, maxes)]
    sums = [jnp.sum(p, axis=-1, keepdims=True) for p in probs]
    outs = [jnp.dot(p.astype(BF16), bt[3], preferred_element_type=F32) / den
            for p, bt, den in zip(probs, batches, sums)]
    for i in range(0, len(batches), heads_per_tile):
        tile0 = batches[i][0]
        o_stack = sum(outs[i + 1:i + heads_per_tile], outs[i])
        for t in range(tiles_per_group):
            qcols = pl.ds((tile0 + t) * LANES, LANES)
            o_t = o_stack[t * SWA_BLOCK:(t + 1) * SWA_BLOCK, :]
            o_ref[:, qcols] = (o_t * z_ref[:, qcols].astype(F32)).astype(o_ref.dtype)


def _swa(proj, rope_tab, sinks, layer, batch, seq):
    m = proj.shape[0]
    nb = seq // SWA_BLOCK
    w = BRANCH_WIDTH
    qblk, zblk = _proj_block("b_q", w), _proj_block("b_z", w)
    kblk, vblk = _proj_block("b_k", SWA_KV_WIDTH), _proj_block("b_v", SWA_KV_WIDTH)

    def cur(b, n):
        return b * nb + n

    def prev(b, n):
        return b * nb + jnp.maximum(n - 1, 0)

    return pl.pallas_call(
        functools.partial(_swa_body, layer),
        grid=(batch, nb),
        in_specs=[
            pl.BlockSpec((SWA_BLOCK, w), lambda b, n: (cur(b, n), qblk)),
            pl.BlockSpec((SWA_BLOCK, SWA_KV_WIDTH), lambda b, n: (cur(b, n), kblk)),
            pl.BlockSpec((SWA_BLOCK, SWA_KV_WIDTH), lambda b, n: (prev(b, n), kblk)),
            pl.BlockSpec((SWA_BLOCK, SWA_KV_WIDTH), lambda b, n: (cur(b, n), vblk)),
            pl.BlockSpec((SWA_BLOCK, SWA_KV_WIDTH), lambda b, n: (prev(b, n), vblk)),
            pl.BlockSpec((SWA_BLOCK, w), lambda b, n: (cur(b, n), zblk)),
            pl.BlockSpec((3, SWA_BLOCK, LANES), lambda b, n: (0, cur(b, n), 0)),
            pl.BlockSpec((3, SWA_BLOCK, LANES), lambda b, n: (0, prev(b, n), 0)),
            pl.BlockSpec(memory_space=pltpu.SMEM),
        ],
        out_specs=pl.BlockSpec((SWA_BLOCK, w), lambda b, n: (cur(b, n), 0)),
        out_shape=jax.ShapeDtypeStruct((m, w), BF16),
        compiler_params=_params("arbitrary", "arbitrary"),
        name="swa",
    )(proj, proj, proj, proj, proj, proj, rope_tab, rope_tab, sinks)


def _mlstm_body(chunks, nb, q_ref, k_ref, v_ref, o_ref, z_ref, gc_ref, gt_ref, ng_ref, y_ref,
                c_ref, n_ref, m_ref):
    L, DK, DV = MLSTM_CHUNK, MLSTM_QK_DIM, MLSTM_V_DIM

    @pl.when(pl.program_id(1) == 0)
    def _():
        c_ref[...] = jnp.zeros_like(c_ref)
        n_ref[...] = jnp.zeros_like(n_ref)
        m_ref[...] = jnp.zeros_like(m_ref)

    r = lax.broadcasted_iota(jnp.int32, (L, L), 0)
    c = lax.broadcasted_iota(jnp.int32, (L, L), 1)
    causal = r >= c
    lane_g = lax.broadcasted_iota(jnp.int32, (L, LANES), 1)
    row_g = lax.broadcasted_iota(jnp.int32, (GATE_ROWS, L), 0)
    chains = [(bi, h) for bi in range(nb) for h in range(MLSTM_HEADS)]

    for ch in range(chunks):
        rows = pl.ds(ch * L, L)
        gates, cums, stab, qkv, inter, houts = [], [], [], [], [], []
        for bi, h in chains:
            gc = gc_ref[bi, rows, :]
            gt = gt_ref[bi, :, ch * L:(ch + 1) * L]
            i_col = jnp.sum(jnp.where(lane_g == h, gc, 0.0), axis=1, keepdims=True)
            f_col = jnp.sum(jnp.where(lane_g == h + MLSTM_HEADS, gc, 0.0), axis=1, keepdims=True)
            i_row = jnp.sum(jnp.where(row_g == h, gt, 0.0), axis=0, keepdims=True)
            f_row = jnp.sum(jnp.where(row_g == h + MLSTM_HEADS, gt, 0.0), axis=0, keepdims=True)
            gates.append((i_col, f_col, i_row, f_row))
        for (bi, h), (i_col, f_col, i_row, f_row) in zip(chains, gates):
            b_col = jnp.sum(jnp.where(causal, f_row, 0.0), axis=1, keepdims=True)
            b_row = jnp.sum(jnp.where(r <= c, f_col, 0.0), axis=0, keepdims=True)
            g_tot = jnp.sum(f_row, axis=1, keepdims=True)
            cums.append((b_col, b_row, g_tot, m_ref[bi * MLSTM_HEADS + h]))
        for (i_col, f_col, i_row, f_row), (b_col, b_row, g_tot, m_prev) in zip(gates, cums):
            log_d = jnp.where(causal, b_col - b_row + i_row, NEG_INF)
            m_inter = b_col + m_prev
            m_t = jnp.maximum(m_inter, jnp.max(log_d, axis=1, keepdims=True))
            stab.append((m_t, jnp.exp(log_d - m_t), jnp.exp(m_inter - m_t)))
        for bi, h in chains:
            qf = q_ref[bi, rows, h * DK:(h + 1) * DK].astype(F32) * (DK ** -0.5)
            qs = qf.astype(BF16)
            k = k_ref[bi, rows, h * DK:(h + 1) * DK]
            v = v_ref[bi, rows, h * DV:(h + 1) * DV]
            qk = lax.dot_general(qs, k, _NT, preferred_element_type=F32)
            qkv.append((qf, qs, k, v, qk))
        for (bi, h), (qf, qs, k, v, qk), (m_t, dmat, a) in zip(chains, qkv, stab):
            c_prev = c_ref[bi * MLSTM_HEADS + h]
            inter.append((qk * dmat, c_prev,
                          jnp.dot(qs, c_prev.astype(BF16), preferred_element_type=F32)))
        for (bi, h), (qf, qs, k, v, qk), (m_t, dmat, a), (s, c_prev, qc) in zip(
                chains, qkv, stab, inter):
            num = jnp.dot(s.astype(BF16), v, preferred_element_type=F32) + a * qc
            qn = jnp.sum(qf * n_ref[bi * MLSTM_HEADS + h], axis=1, keepdims=True)
            den = jnp.sum(s, axis=1, keepdims=True) + a * qn
            houts.append(num / jnp.maximum(jnp.abs(den), jnp.exp(-m_t)))
        for (bi, h), (i_col, f_col, i_row, f_row), (b_col, b_row, g_tot, m_prev), \
                (qf, qs, k, v, qk), (s, c_prev, qc) in zip(chains, gates, cums, qkv, inter):
            idx = bi * MLSTM_HEADS + h
            w_col = g_tot - b_col + i_col
            w_row = g_tot - b_row + i_row
            m_new = jnp.maximum(g_tot + m_prev, jnp.max(w_row, axis=1, keepdims=True))
            wgt = jnp.exp(w_col - m_new)
            decay = jnp.exp(g_tot + m_prev - m_new)
            kw = k.astype(F32) * wgt
            c_ref[idx] = decay * c_prev + lax.dot_general(
                kw.astype(BF16), v, (((0,), (0,)), ((), ())), preferred_element_type=F32)
            n_ref[idx] = decay * n_ref[idx] + jnp.sum(kw, axis=0, keepdims=True)
            m_ref[idx] = m_new
        for (bi, h), hout in zip(chains, houts):
            vcols = pl.ds(h * DV, DV)
            hn = hout * lax.rsqrt(jnp.mean(hout * hout, axis=1, keepdims=True) + NORM_EPS)
            hn = hn * ng_ref[:, vcols]
            y = hn * o_ref[bi, rows, vcols].astype(F32) * z_ref[bi, rows, vcols].astype(F32)
            y_ref[bi, rows, vcols] = y.astype(y_ref.dtype)


def _mlstm(proj, gcol, gtr, norm_g, layer, batch, seq, ts=128, nb=2):
    m = proj.shape[0]
    nt = seq // ts
    qw, vw = MLSTM_HEADS * MLSTM_QK_DIM, BRANCH_WIDTH
    qblk, kblk = _proj_block("c_q", qw), _proj_block("c_k", qw)
    vblk, oblk, zblk = (_proj_block(n, vw) for n in ("c_v", "c_o", "c_z"))
    proj3 = proj.reshape(batch, seq, proj.shape[-1])
    assert batch % nb == 0

    def tok3(a):
        return a.reshape(batch, seq, a.shape[-1])

    chains = nb * MLSTM_HEADS
    out = pl.pallas_call(
        functools.partial(_mlstm_body, ts // MLSTM_CHUNK, nb),
        grid=(batch // nb, nt),
        in_specs=[
            pl.BlockSpec((nb, ts, qw), lambda b, t: (b, t, qblk)),
            pl.BlockSpec((nb, ts, qw), lambda b, t: (b, t, kblk)),
            pl.BlockSpec((nb, ts, vw), lambda b, t: (b, t, vblk)),
            pl.BlockSpec((nb, ts, vw), lambda b, t: (b, t, oblk)),
            pl.BlockSpec((nb, ts, vw), lambda b, t: (b, t, zblk)),
            pl.BlockSpec((nb, ts, LANES), lambda b, t: (b, t, 0)),
            pl.BlockSpec((nb, GATE_ROWS, ts), lambda b, t: (b, 0, t)),
            pl.BlockSpec((None, 1, vw), lambda b, t: (layer, 0, 0)),
        ],
        out_specs=pl.BlockSpec((nb, ts, vw), lambda b, t: (b, t, 0)),
        out_shape=jax.ShapeDtypeStruct((batch, seq, BRANCH_WIDTH), BF16),
        scratch_shapes=[
            pltpu.VMEM((chains, MLSTM_QK_DIM, MLSTM_V_DIM), F32),
            pltpu.VMEM((chains, 1, MLSTM_QK_DIM), F32),
            pltpu.VMEM((chains, 1, 1), F32),
        ],
        compiler_params=_params("arbitrary", "arbitrary"),
        name="mlstm",
    )(proj3, proj3, proj3, proj3, proj3, tok3(gcol), gtr, norm_g)
    return out.reshape(m, BRANCH_WIDTH)


def _merge_body(ya_ref, yb_ref, yc_ref, wp_ref, g0_ref, g1_ref, g2_ref, o_ref, wb_ref):
    piece = wp_ref.shape[1]
    rows = pl.ds(pl.multiple_of(pl.program_id(1) * piece, piece), piece)

    def cast(dst):
        dst[:, rows, :] = wp_ref[...].astype(BF16)

    def compute(src):
        acc = g0_ref[...].astype(F32) * jnp.dot(ya_ref[...], src[0], preferred_element_type=F32)
        acc = acc + g1_ref[...].astype(F32) * jnp.dot(yb_ref[...], src[1],
                                                      preferred_element_type=F32)
        acc = acc + g2_ref[...].astype(F32) * jnp.dot(yc_ref[...], src[2],
                                                      preferred_element_type=F32)
        o_ref[...] = acc.astype(o_ref.dtype)

    _stream_step(wb_ref, cast, compute)


def _merge(ya, yb, yc, w_branch, proj, layer, tm=512, tn=1024):
    m = ya.shape[0]
    w = BRANCH_WIDTH
    n_col, n_row = D_MODEL // tn, m // tm
    piece = w // n_row
    gate0 = PROJ_COL["gates"] // tn
    assert PROJ_COL["gates"] % tn == 0
    yspec = pl.BlockSpec((tm, w), lambda jj, i: (_row_of(jj, i), 0))

    def gate_spec(branch):
        return pl.BlockSpec(
            (tm, tn), lambda jj, i: (_row_of(jj, i), gate0 + branch * n_col + _col_of(jj)))

    return pl.pallas_call(
        _merge_body,
        grid=(n_col + 1, n_row),
        in_specs=[
            yspec, yspec, yspec,
            pl.BlockSpec((None, N_BRANCHES, piece, tn),
                         lambda jj, i: (layer, 0, i, _cast_col_of(jj, n_col))),
            gate_spec(0), gate_spec(1), gate_spec(2),
        ],
        out_specs=pl.BlockSpec((tm, tn), lambda jj, i: (_row_of(jj, i), _col_of(jj))),
        out_shape=jax.ShapeDtypeStruct((m, D_MODEL), BF16),
        scratch_shapes=[pltpu.VMEM((2, N_BRANCHES, w, tn), BF16)],
        compiler_params=_params("arbitrary", "arbitrary"),
        name="merge",
    )(ya, yb, yc, w_branch, proj, proj, proj)


def _outproj_body(a_ref, wp_ref, t_ref, ssq_ref, wb_ref):
    jj = pl.program_id(0)
    i = pl.program_id(1)
    piece = wp_ref.shape[0]
    rows = pl.ds(pl.multiple_of(i * piece, piece), piece)

    def cast(dst):
        dst[rows, :] = wp_ref[...].astype(BF16)

    def compute(src):
        acc = jnp.dot(a_ref[...], src[...], preferred_element_type=F32)
        t_ref[...] = acc.astype(t_ref.dtype)
        out_rows = pl.ds(pl.multiple_of(i * TM, TM), TM)
        part = jnp.broadcast_to(jnp.sum(acc * acc, axis=-1, keepdims=True), (TM, LANES))

        @pl.when(jj == 1)
        def _():
            ssq_ref[out_rows, :] = part

        @pl.when(jj > 1)
        def _():
            ssq_ref[out_rows, :] += part

    _stream_step(wb_ref, cast, compute)


def _outproj(a, w_out, layer, tn=1024):
    m, k = a.shape
    n = w_out.shape[-1]
    n_col, n_row = n // tn, m // TM
    piece = k // n_row
    return pl.pallas_call(
        _outproj_body,
        grid=(n_col + 1, n_row),
        in_specs=[
            pl.BlockSpec((TM, k), lambda jj, i: (_row_of(jj, i), 0)),
            pl.BlockSpec((None, piece, tn), lambda jj, i: (layer, i, _cast_col_of(jj, n_col))),
        ],
        out_specs=[
            pl.BlockSpec((TM, tn), lambda jj, i: (_row_of(jj, i), _col_of(jj))),
            pl.BlockSpec((m, LANES), lambda jj, i: (0, 0)),
        ],
        out_shape=[
            jax.ShapeDtypeStruct((m, n), BF16),
            jax.ShapeDtypeStruct((m, LANES), F32),
        ],
        scratch_shapes=[pltpu.VMEM((2, k, tn), BF16)],
        compiler_params=_params("arbitrary", "arbitrary"),
        name="outproj",
    )(a, w_out)


def _post_residual(x, t, ssq, g):
    rs = lax.rsqrt(ssq * (1.0 / D_MODEL) + NORM_EPS)
    return x + t.astype(F32) * rs * g


def _postnorm_body(x_ref, t_ref, ssq_ref, g_ref, x1b_ref):
    x1b_ref[...] = _post_residual(x_ref[...], t_ref[...], ssq_ref[:, 0:1], g_ref[...]).astype(BF16)


def _postnorm(x2d, t, ssq, norm_post, layer, tm=TM_ROWWISE):
    m = x2d.shape[0]
    return pl.pallas_call(
        _postnorm_body,
        grid=(m // tm,),
        in_specs=[
            pl.BlockSpec((tm, D_MODEL), lambda i: (i, 0)),
            pl.BlockSpec((tm, D_MODEL), lambda i: (i, 0)),
            pl.BlockSpec((tm, LANES), lambda i: (i, 0)),
            pl.BlockSpec((None, 1, D_MODEL), lambda i: (layer, 0, 0)),
        ],
        out_specs=pl.BlockSpec((tm, D_MODEL), lambda i: (i, 0)),
        out_shape=jax.ShapeDtypeStruct((m, D_MODEL), BF16),
        compiler_params=_params("arbitrary"),
        name="postnorm",
    )(x2d, t, ssq, norm_post)


def _ple_body(p_ref, w_ref, g_ref, e_ref, wb_ref):
    @pl.when(pl.program_id(0) == 0)
    def _():
        wb_ref[...] = w_ref[...].astype(BF16)

    t = jnp.dot(p_ref[...].astype(BF16), wb_ref[...], preferred_element_type=F32)
    e = t * lax.rsqrt(jnp.mean(t * t, axis=-1, keepdims=True) + NORM_EPS) * g_ref[...]
    e_ref[...] = e.astype(e_ref.dtype)


def _ple(p3d, ple_proj, ple_norm, layer, tm=TM_ROWWISE):
    m = p3d.shape[1]
    return pl.pallas_call(
        _ple_body,
        grid=(m // tm,),
        in_specs=[
            pl.BlockSpec((None, tm, PLE_DIM), lambda i: (layer, i, 0)),
            pl.BlockSpec((None, PLE_DIM, D_MODEL), lambda i: (layer, 0, 0)),
            pl.BlockSpec((None, 1, D_MODEL), lambda i: (layer, 0, 0)),
        ],
        out_specs=pl.BlockSpec((tm, D_MODEL), lambda i: (i, 0)),
        out_shape=jax.ShapeDtypeStruct((m, D_MODEL), BF16),
        scratch_shapes=[pltpu.VMEM((PLE_DIM, D_MODEL), BF16)],
        compiler_params=_params("arbitrary"),
        name="ple_embed",
    )(p3d, ple_proj, ple_norm)


def _plegate_body(a_ref, wp_ref, x_ref, t_ref, ssq_ref, g_ref, e_ref, o_ref, wb_ref):
    piece = wp_ref.shape[0]
    rows = pl.ds(pl.multiple_of(pl.program_id(1) * piece, piece), piece)

    def cast(dst):
        dst[rows, :] = wp_ref[...].astype(BF16)

    def compute(src):
        acc = jnp.dot(a_ref[...], src[...], preferred_element_type=F32)
        x1 = _post_residual(x_ref[...], t_ref[...], ssq_ref[:, 0:1], g_ref[...])
        o_ref[...] = x1 + _sigmoid(acc) * e_ref[...].astype(F32)

    _stream_step(wb_ref, cast, compute)


def _plegate(x1b, ple_gate, x2d, t, ssq, norm_post, e, layer, tm=512, tn=1024):
    m, k = x1b.shape
    n = ple_gate.shape[-1]
    n_col, n_row = n // tn, m // tm
    piece = k // n_row
    tile = pl.BlockSpec((tm, tn), lambda jj, i: (_row_of(jj, i), _col_of(jj)))
    return pl.pallas_call(
        _plegate_body,
        grid=(n_col + 1, n_row),
        in_specs=[
            pl.BlockSpec((tm, k), lambda jj, i: (_row_of(jj, i), 0)),
            pl.BlockSpec((None, piece, tn), lambda jj, i: (layer, i, _cast_col_of(jj, n_col))),
            tile, tile,
            pl.BlockSpec((tm, LANES), lambda jj, i: (_row_of(jj, i), 0)),
            pl.BlockSpec((None, 1, tn), lambda jj, i: (layer, 0, _col_of(jj))),
            tile,
        ],
        out_specs=tile,
        out_shape=jax.ShapeDtypeStruct((m, n), F32),
        scratch_shapes=[pltpu.VMEM((2, k, tn), BF16)],
        compiler_params=_params("arbitrary", "arbitrary"),
        name="ple_gate",
    )(x1b, ple_gate, x2d, t, ssq, norm_post, e)


def kernel(x, p, positions, norm_pre, w_in, gmlp_ln_g, gmlp_ln_b, gmlp_ws, gmlp_bs, attn_sinks,
           mlstm_ib, mlstm_fb, mlstm_norm_g, w_branch, w_out, norm_post, ple_proj, ple_norm,
           ple_gate):
    batch, seq, d = x.shape
    depth = w_in.shape[0]
    m = batch * seq
    assert d == D_MODEL and w_in.shape[-1] == N_IN and seq % 512 == 0 and m % TM == 0

    inv_freq = ROPE_THETA ** (-jnp.arange(0, ROT_DIM, 2, dtype=F32) / ROT_DIM)
    lane_d = jnp.arange(LANES) % SWA_HEAD_DIM
    inv_lane = jnp.where(lane_d < ROT_DIM, inv_freq[lane_d % (ROT_DIM // 2)], 0.0).reshape(1, LANES)
    posf = positions.astype(F32).reshape(m, 1)
    p3d = p.reshape(depth, m, PLE_DIM)
    bst = jnp.swapaxes(gmlp_bs, 1, 2)
    gate_bias = jnp.pad(jnp.concatenate([mlstm_ib, mlstm_fb], axis=1).astype(F32),
                        ((0, 0), (0, LANES - GATE_COLS)))

    def rows3(a):
        return a.reshape(depth, 1, a.shape[-1])

    norm_pre, norm_post, ple_norm = rows3(norm_pre), rows3(norm_post), rows3(ple_norm)
    gmlp_ln_g, gmlp_ln_b = rows3(gmlp_ln_g), rows3(gmlp_ln_b)
    mlstm_norm_g = rows3(mlstm_norm_g)
    rope_tab = _rope_call(posf, inv_lane)

    w_t = jnp.swapaxes(w_in, 1, 2)

    xc = x.reshape(m, d)
    for l in range(depth):
        h, gcol, gtr = _prenorm(xc, norm_pre, w_t, gate_bias[l:l + 1], l, batch, seq)
        proj = _proj(h, w_t, l)

        ya = _gmlp(proj, gmlp_ln_g, gmlp_ln_b, gmlp_ws, bst, l)
        yb = _swa(proj, rope_tab, attn_sinks, l, batch, seq)
        yc = _mlstm(proj, gcol, gtr, mlstm_norm_g, l, batch, seq)

        mixed = _merge(ya, yb, yc, w_branch, proj, l)
        t, ssq = _outproj(mixed, w_out, l)
        x1b = _postnorm(xc, t, ssq, norm_post, l)
        e = _ple(p3d, ple_proj, ple_norm, l)
        xc = _plegate(x1b, ple_gate, xc, t, ssq, norm_post, e, l)
    return xc.reshape(batch, seq, d)
```

```python
import functools

import jax
import jax.numpy as jnp
from jax import lax
from jax.experimental import pallas as pl
from jax.experimental.pallas import tpu as pltpu

F32 = jnp.float32
BF16 = jnp.bfloat16

D_MODEL = 4096
PLE_DIM = 256
N_BRANCHES = 3
BRANCH_WIDTH = D_MODEL // 2
GMLP_CHUNK = 128
GMLP_GROUPS = 8
GMLP_GROUP_WIDTH = BRANCH_WIDTH // GMLP_GROUPS
SWA_HEAD_DIM = 64
SWA_Q_HEADS = BRANCH_WIDTH // SWA_HEAD_DIM
SWA_KV_HEADS = 4
SWA_GROUP = SWA_Q_HEADS // SWA_KV_HEADS
SWA_BLOCK = 128
SWA_KV_WIDTH = SWA_KV_HEADS * SWA_HEAD_DIM
ROT_DIM = SWA_HEAD_DIM // 4
ROPE_THETA = 500000.0
MLSTM_HEADS = 4
MLSTM_V_DIM = BRANCH_WIDTH // MLSTM_HEADS
MLSTM_QK_DIM = MLSTM_V_DIM // 2
MLSTM_CHUNK = 256
GATE_SOFTCAP = 15.0
NORM_EPS = 1e-6
NEG_INF = -1e30

LANES = 128
SUBLANES = 8
BF16_ROWS = 16
GATE_ROWS = BF16_ROWS
VMEM_LIMIT = 56 * 1024 * 1024
TM = 1024
TN = 1024
TM_ROWWISE = 512

_SPLIT = (
    ("a_u", BRANCH_WIDTH), ("a_v", BRANCH_WIDTH), ("a_z", BRANCH_WIDTH),
    ("b_q", BRANCH_WIDTH), ("b_k", SWA_KV_WIDTH), ("b_v", SWA_KV_WIDTH), ("b_z", BRANCH_WIDTH),
    ("c_q", MLSTM_HEADS * MLSTM_QK_DIM), ("c_k", MLSTM_HEADS * MLSTM_QK_DIM),
    ("c_v", BRANCH_WIDTH), ("c_i", MLSTM_HEADS), ("c_f", MLSTM_HEADS),
    ("c_o", BRANCH_WIDTH), ("c_z", BRANCH_WIDTH), ("gates", N_BRANCHES * D_MODEL),
)
_SRC = {}
_off = 0
for _name, _size in _SPLIT:
    _SRC[_name] = (_off, _size)
    _off += _size
N_IN = _off
GATE_COLS = 2 * MLSTM_HEADS
GATE_START = _SRC["c_i"][0]


def _params(*sem):
    return pltpu.CompilerParams(dimension_semantics=sem, vmem_limit_bytes=VMEM_LIMIT)


def _gelu(x):
    return 0.5 * x * (1.0 + lax.erf(x * (0.5 ** 0.5)))


def _sigmoid(x):
    return jax.nn.sigmoid(x)


def _softcap(z):
    return GATE_SOFTCAP * jnp.tanh(z / GATE_SOFTCAP)


def _log_sigmoid(x):
    return -(jnp.maximum(-x, 0.0) + jnp.log1p(jnp.exp(-jnp.abs(x))))


def _stream_step(wb_ref, cast, compute):
    jj = pl.program_id(0)
    cast(wb_ref.at[jj % 2])

    @pl.when(jj > 0)
    def _():
        compute(wb_ref.at[(jj + 1) % 2])


def _row_of(jj, i):
    return jnp.where(jj > 0, i, 0)


def _col_of(jj):
    return jnp.maximum(jj - 1, 0)


def _cast_col_of(jj, n_col):
    return jnp.minimum(jj, n_col - 1)


def _prenorm_body(x_ref, g_ref, wif_ref, brow_ref, h_ref, gc_ref, gt_ref, wb_ref):
    @pl.when(pl.program_id(0) == 0)
    def _():
        wb_ref[...] = wif_ref[...].astype(BF16)

    x = x_ref[...]
    y = x * lax.rsqrt(jnp.mean(x * x, axis=-1, keepdims=True) + NORM_EPS) * g_ref[...]
    hb = y.astype(BF16)
    h_ref[...] = hb
    pre = lax.dot_general(hb, wb_ref[...], (((1,), (1,)), ((), ())),
                          preferred_element_type=F32) + brow_ref[...]
    sc = _softcap(pre)
    lane = lax.broadcasted_iota(jnp.int32, sc.shape, 1)
    gc = jnp.where(lane >= MLSTM_HEADS, _log_sigmoid(sc), sc)
    gc_ref[...] = gc
    gt_ref[...] = gc.T[:GATE_ROWS, :]


def _prenorm(x2d, norm_pre, w_t, brow, layer, batch, seq, tm=TM_ROWWISE):
    m = x2d.shape[0]
    per_seq = seq // tm
    assert GATE_START % LANES == 0 and seq % tm == 0
    return pl.pallas_call(
        _prenorm_body,
        grid=(m // tm,),
        in_specs=[
            pl.BlockSpec((tm, D_MODEL), lambda i: (i, 0)),
            pl.BlockSpec((None, 1, D_MODEL), lambda i: (layer, 0, 0)),
            pl.BlockSpec((None, LANES, D_MODEL), lambda i: (layer, GATE_START // LANES, 0)),
            pl.BlockSpec((1, LANES), lambda i: (0, 0)),
        ],
        out_specs=[
            pl.BlockSpec((tm, D_MODEL), lambda i: (i, 0)),
            pl.BlockSpec((tm, LANES), lambda i: (i, 0)),
            pl.BlockSpec((None, GATE_ROWS, tm), lambda i: (i // per_seq, 0, i % per_seq)),
        ],
        out_shape=[
            jax.ShapeDtypeStruct((m, D_MODEL), BF16),
            jax.ShapeDtypeStruct((m, LANES), F32),
            jax.ShapeDtypeStruct((batch, GATE_ROWS, seq), F32),
        ],
        scratch_shapes=[pltpu.VMEM((LANES, D_MODEL), BF16)],
        compiler_params=_params("arbitrary"),
        name="prenorm",
    )(x2d, norm_pre, w_t, brow)


_NT = (((1,), (1,)), ((), ()))

_PROJ_ORDER = (
    ("a_u", "gelu"), ("a_v", "gelu"), ("a_z", "silu"), ("b_q", "linear"), ("b_z", "silu"),
    ("c_q", "linear"), ("c_k", "linear"), ("c_v", "linear"), ("c_o", "sigmoid"), ("c_z", "silu"),
    ("gates", "sigmoid"), ("b_k", "linear"),
)


def _proj_layout():
    tiles, cols, col = [], {}, 0
    for name, kind in _PROJ_ORDER:
        start, width = _SRC[name]
        cols[name] = col
        n_tiles = -(-width // TN)
        tiles += [(start + i * TN, kind) for i in range(n_tiles)]
        col += n_tiles * TN
    cols["b_v"] = cols["b_k"] + _SRC["b_k"][1]
    assert all(s % SUBLANES == 0 and s + TN <= N_IN for s, _ in tiles)
    return tuple(tiles), cols, col


_PROJ_TILES, PROJ_COL, PROJ_WIDTH = _proj_layout()


def _proj_block(name, width):
    assert PROJ_COL[name] % width == 0
    return PROJ_COL[name] // width


_N_GELU = sum(kind == "gelu" for _, kind in _PROJ_TILES)
assert all(kind == "gelu" for _, kind in _PROJ_TILES[:_N_GELU])


def _tile_is(t, kind):
    hit = False
    for idx, (_, k) in enumerate(_PROJ_TILES):
        if k == kind:
            hit = (t == idx) | hit
    return hit


def _tile_src_start(t):
    start = _PROJ_TILES[0][0] + t * TN
    for idx in range(1, len(_PROJ_TILES)):
        if _PROJ_TILES[idx][0] != _PROJ_TILES[idx - 1][0] + TN:
            start = jnp.where(t >= idx, _PROJ_TILES[idx][0] + (t - idx) * TN, start)
    return start


def _proj_body(h_ref, wp_ref, o_ref, wb_ref):
    jj = pl.program_id(0)
    t = jj - 1
    piece = wp_ref.shape[0]
    rows = pl.ds(pl.multiple_of(pl.program_id(1) * piece, piece), piece)
    wb_ref[jj % 2, rows, :] = wp_ref[...].astype(BF16)

    def matmul():
        return lax.dot_general(h_ref[...], wb_ref[(jj + 1) % 2], _NT, preferred_element_type=F32)

    @pl.when((jj > 0) & (t < _N_GELU))
    def _():
        o_ref[...] = _gelu(matmul()).astype(o_ref.dtype)

    @pl.when(t >= _N_GELU)
    def _():
        acc = matmul()
        sig = _sigmoid(acc)
        out = jnp.where(_tile_is(t, "linear"), acc, jnp.where(_tile_is(t, "silu"), acc * sig, sig))
        o_ref[...] = out.astype(o_ref.dtype)


def _proj(h, w_t, layer):
    m, k = h.shape
    n_col, n_row = len(_PROJ_TILES), m // TM
    piece = TN // n_row
    assert TN % n_row == 0 and piece % BF16_ROWS == 0

    def w_index(jj, i):
        src = _tile_src_start(_cast_col_of(jj, n_col))
        return layer, pl.multiple_of(src + i * piece, SUBLANES), 0

    return pl.pallas_call(
        _proj_body,
        grid=(n_col + 1, n_row),
        in_specs=[
            pl.BlockSpec((TM, k), lambda jj, i: (_row_of(jj, i), 0)),
            pl.BlockSpec((None, pl.Element(piece), pl.Element(k)), w_index),
        ],
        out_specs=pl.BlockSpec((TM, TN), lambda jj, i: (_row_of(jj, i), _col_of(jj))),
        out_shape=jax.ShapeDtypeStruct((m, PROJ_WIDTH), BF16),
        scratch_shapes=[pltpu.VMEM((2, TN, k), BF16)],
        compiler_params=_params("arbitrary", "arbitrary"),
        name="proj",
    )(h, w_t)


def _gmlp_body(chunks, u_ref, v_ref, z_ref, lg_ref, lb_ref, ws_ref, bst_ref, o_ref):
    r = lax.broadcasted_iota(jnp.int32, (GMLP_CHUNK, GMLP_CHUNK), 0)
    c = lax.broadcasted_iota(jnp.int32, (GMLP_CHUNK, GMLP_CHUNK), 1)
    tri = r >= c
    wmix = [jnp.where(tri, ws_ref[g], 0.0).astype(BF16) for g in range(GMLP_GROUPS)]
    for ch in range(chunks):
        rows = pl.ds(ch * GMLP_CHUNK, GMLP_CHUNK)
        v = v_ref[rows, :].astype(F32)
        mu = jnp.mean(v, axis=-1, keepdims=True)
        d = v - mu
        var = jnp.mean(d * d, axis=-1, keepdims=True)
        vn = (d * lax.rsqrt(var + NORM_EPS) * lg_ref[...] + lb_ref[...]).astype(BF16)
        for g in range(GMLP_GROUPS):
            cols = pl.ds(g * GMLP_GROUP_WIDTH, GMLP_GROUP_WIDTH)
            mixed = jnp.dot(wmix[g], vn[:, g * GMLP_GROUP_WIDTH:(g + 1) * GMLP_GROUP_WIDTH],
                            preferred_element_type=F32) + bst_ref[:, g:g + 1]
            y = u_ref[rows, cols].astype(F32) * mixed * z_ref[rows, cols].astype(F32)
            o_ref[rows, cols] = y.astype(o_ref.dtype)


def _gmlp(proj, ln_g, ln_b, ws, bst, layer, ts=512):
    m = proj.shape[0]
    w = BRANCH_WIDTH
    ublk, vblk, zblk = (_proj_block(n, w) for n in ("a_u", "a_v", "a_z"))
    return pl.pallas_call(
        functools.partial(_gmlp_body, ts // GMLP_CHUNK),
        grid=(m // ts,),
        in_specs=[
            pl.BlockSpec((ts, w), lambda i: (i, ublk)),
            pl.BlockSpec((ts, w), lambda i: (i, vblk)),
            pl.BlockSpec((ts, w), lambda i: (i, zblk)),
            pl.BlockSpec((None, 1, w), lambda i: (layer, 0, 0)),
            pl.BlockSpec((None, 1, w), lambda i: (layer, 0, 0)),
            pl.BlockSpec((None, GMLP_GROUPS, GMLP_CHUNK, GMLP_CHUNK), lambda i: (layer, 0, 0, 0)),
            pl.BlockSpec((None, GMLP_CHUNK, GMLP_GROUPS), lambda i: (layer, 0, 0)),
        ],
        out_specs=pl.BlockSpec((ts, w), lambda i: (i, 0)),
        out_shape=jax.ShapeDtypeStruct((m, w), BF16),
        compiler_params=_params("arbitrary"),
        name="gmlp",
    )(proj, proj, proj, ln_g, ln_b, ws, bst)


def _rope_tables(pos_ref, inv_ref):
    ang = pos_ref[...] * inv_ref[...]
    d = lax.broadcasted_iota(jnp.int32, ang.shape, 1) % SWA_HEAD_DIM
    half = ROT_DIM // 2
    cosf = jnp.where(d < ROT_DIM, jnp.cos(ang), 1.0)
    sinf = jnp.sin(ang)
    s_lo = jnp.where(d < half, -sinf, 0.0)
    s_hi = jnp.where((d >= half) & (d < ROT_DIM), sinf, 0.0)
    return cosf, s_lo, s_hi


def _rope_tile(t, tables):
    cosf, s_lo, s_hi = tables
    half = ROT_DIM // 2
    up = pltpu.roll(t, LANES - half, 1)
    dn = pltpu.roll(t, half, 1)
    return t * cosf + up * s_lo + dn * s_hi


def _rope_body(pos_ref, inv_ref, o_ref):
    cosf, s_lo, s_hi = _rope_tables(pos_ref, inv_ref)
    o_ref[0] = cosf
    o_ref[1] = s_lo
    o_ref[2] = s_hi


def _rope_call(posf, inv_lane, tm=512):
    m = posf.shape[0]
    return pl.pallas_call(
        _rope_body,
        grid=(m // tm,),
        in_specs=[pl.BlockSpec((tm, 1), lambda i: (i, 0)),
                  pl.BlockSpec((1, LANES), lambda i: (0, 0))],
        out_specs=pl.BlockSpec((3, tm, LANES), lambda i: (0, i, 0)),
        out_shape=jax.ShapeDtypeStruct((3, m, LANES), F32),
        compiler_params=_params("arbitrary"),
        name="rope_tables",
    )(posf, inv_lane)


def _swa_body(layer, q_ref, kc_ref, kp_ref, vc_ref, vp_ref, z_ref, tc_ref, tp_ref, sink_ref, o_ref):
    n = pl.program_id(1)
    tab_c = (tc_ref[0], tc_ref[1], tc_ref[2])
    tab_p = (tp_ref[0], tp_ref[1], tp_ref[2])
    heads_per_tile = LANES // SWA_HEAD_DIM
    tiles_per_group = SWA_GROUP // heads_per_tile
    rows = tiles_per_group * SWA_BLOCK

    lane_half = lax.broadcasted_iota(jnp.int32, (2 * SWA_BLOCK, LANES), 1) // SWA_HEAD_DIM
    key_row = lax.broadcasted_iota(jnp.int32, (2 * SWA_BLOCK, LANES), 0)
    qi = lax.broadcasted_iota(jnp.int32, (rows, LANES), 0) % SWA_BLOCK
    kj = lax.broadcasted_iota(jnp.int32, (rows, LANES), 1)
    mask_prev = (kj > qi) & (n > 0)
    mask_cur = kj <= qi
    sink_slot = lax.broadcasted_iota(jnp.int32, (SWA_BLOCK, LANES), 1) == 0

    scale = SWA_HEAD_DIM ** -0.5
    batches = []
    for kt in range(SWA_KV_WIDTH // LANES):
        cols = pl.ds(kt * LANES, LANES)
        k_band = jnp.concatenate(
            [_rope_tile(kp_ref[:, cols].astype(F32), tab_p),
             _rope_tile(kc_ref[:, cols].astype(F32), tab_c)], axis=0)
        v_band = jnp.concatenate([vp_ref[:, cols], vc_ref[:, cols]], axis=0).astype(F32)
        v_band = jnp.where(key_row == 0, 0.0, v_band)
        for hh in range(heads_per_tile):
            kvh = kt * heads_per_tile + hh
            k_keep = jnp.where(lane_half == hh, k_band, 0.0)
            v_keep = jnp.where(lane_half == hh, v_band, 0.0)
            k_swap = pltpu.roll(k_keep, SWA_HEAD_DIM, 1)
            v_swap = pltpu.roll(v_keep, SWA_HEAD_DIM, 1)
            k_at = [k_keep.astype(BF16), k_swap.astype(BF16)]
            v_at = [v_keep.astype(BF16), v_swap.astype(BF16)]
            if hh == 1:
                k_at.reverse()
                v_at.reverse()
            tile0 = kvh * tiles_per_group
            q_stack = jnp.concatenate(
                [(_rope_tile(q_ref[:, pl.ds((tile0 + t) * LANES, LANES)].astype(F32), tab_c)
                  * scale).astype(BF16) for t in range(tiles_per_group)], axis=0)
            for qh in range(heads_per_tile):
                fill = jnp.concatenate(
                    [jnp.where(sink_slot,
                               sink_ref[layer, (tile0 + t) * heads_per_tile + qh], NEG_INF)
                     for t in range(tiles_per_group)], axis=0)
                batches.append((tile0, q_stack, k_at[qh], v_at[qh], fill))

    scores = []
    for _, q_stack, k_rows, _, fill in batches:
        qk = lax.dot_general(q_stack, k_rows, _NT, preferred_element_type=F32)
        scores.append(jnp.concatenate([jnp.where(mask_prev, qk[:, :LANES], fill),
                                       jnp.where(mask_cur, qk[:, LANES:], NEG_INF)], axis=1))
    maxes = [jnp.max(s, axis=-1, keepdims=True) for s in scores]
    probs = [jnp.exp(s - mx) for s, mx in zip(scores, maxes)]
    sums = [jnp.sum(p, axis=-1, keepdims=True) for p in probs]
    outs = [jnp.dot(p.astype(BF16), bt[3], preferred_element_type=F32) / den
            for p, bt, den in zip(probs, batches, sums)]
    for i in range(0, len(batches), heads_per_tile):
        tile0 = batches[i][0]
        o_stack = sum(outs[i + 1:i + heads_per_tile], outs[i])
        for t in range(tiles_per_group):
            qcols = pl.ds((tile0 + t) * LANES, LANES)
            o_t = o_stack[t * SWA_BLOCK:(t + 1) * SWA_BLOCK, :]
            o_ref[:, qcols] = (o_t * z_ref[:, qcols].astype(F32)).astype(o_ref.dtype)


def _swa(proj, rope_tab, sinks, layer, batch, seq):
    m = proj.shape[0]
    nb = seq // SWA_BLOCK
    w = BRANCH_WIDTH
    qblk, zblk = _proj_block("b_q", w), _proj_block("b_z", w)
    kblk, vblk = _proj_block("b_k", SWA_KV_WIDTH), _proj_block("b_v", SWA_KV_WIDTH)

    def cur(b, n):
        return b * nb + n

    def prev(b, n):
        return b * nb + jnp.maximum(n - 1, 0)

    return pl.pallas_call(
        functools.partial(_swa_body, layer),
        grid=(batch, nb),
        in_specs=[
            pl.BlockSpec((SWA_BLOCK, w), lambda b, n: (cur(b, n), qblk)),
            pl.BlockSpec((SWA_BLOCK, SWA_KV_WIDTH), lambda b, n: (cur(b, n), kblk)),
            pl.BlockSpec((SWA_BLOCK, SWA_KV_WIDTH), lambda b, n: (prev(b, n), kblk)),
            pl.BlockSpec((SWA_BLOCK, SWA_KV_WIDTH), lambda b, n: (cur(b, n), vblk)),
            pl.BlockSpec((SWA_BLOCK, SWA_KV_WIDTH), lambda b, n: (prev(b, n), vblk)),
            pl.BlockSpec((SWA_BLOCK, w), lambda b, n: (cur(b, n), zblk)),
            pl.BlockSpec((3, SWA_BLOCK, LANES), lambda b, n: (0, cur(b, n), 0)),
            pl.BlockSpec((3, SWA_BLOCK, LANES), lambda b, n: (0, prev(b, n), 0)),
            pl.BlockSpec(memory_space=pltpu.SMEM),
        ],
        out_specs=pl.BlockSpec((SWA_BLOCK, w), lambda b, n: (cur(b, n), 0)),
        out_shape=jax.ShapeDtypeStruct((m, w), BF16),
        compiler_params=_params("arbitrary", "arbitrary"),
        name="swa",
    )(proj, proj, proj, proj, proj, proj, rope_tab, rope_tab, sinks)


def _mlstm_body(chunks, nb, q_ref, k_ref, v_ref, o_ref, z_ref, gc_ref, gt_ref, ng_ref, y_ref,
                c_ref, n_ref, m_ref):
    L, DK, DV = MLSTM_CHUNK, MLSTM_QK_DIM, MLSTM_V_DIM

    @pl.when(pl.program_id(1) == 0)
    def _():
        c_ref[...] = jnp.zeros_like(c_ref)
        n_ref[...] = jnp.zeros_like(n_ref)
        m_ref[...] = jnp.zeros_like(m_ref)

    r = lax.broadcasted_iota(jnp.int32, (L, L), 0)
    c = lax.broadcasted_iota(jnp.int32, (L, L), 1)
    causal = r >= c
    lane_g = lax.broadcasted_iota(jnp.int32, (L, LANES), 1)
    row_g = lax.broadcasted_iota(jnp.int32, (GATE_ROWS, L), 0)
    chains = [(bi, h) for bi in range(nb) for h in range(MLSTM_HEADS)]

    for ch in range(chunks):
        rows = pl.ds(ch * L, L)
        gates, cums, stab, qkv, inter, houts = [], [], [], [], [], []
        for bi, h in chains:
            gc = gc_ref[bi, rows, :]
            gt = gt_ref[bi, :, ch * L:(ch + 1) * L]
            i_col = jnp.sum(jnp.where(lane_g == h, gc, 0.0), axis=1, keepdims=True)
            f_col = jnp.sum(jnp.where(lane_g == h + MLSTM_HEADS, gc, 0.0), axis=1, keepdims=True)
            i_row = jnp.sum(jnp.where(row_g == h, gt, 0.0), axis=0, keepdims=True)
            f_row = jnp.sum(jnp.where(row_g == h + MLSTM_HEADS, gt, 0.0), axis=0, keepdims=True)
            gates.append((i_col, f_col, i_row, f_row))
        for (bi, h), (i_col, f_col, i_row, f_row) in zip(chains, gates):
            b_col = jnp.sum(jnp.where(causal, f_row, 0.0), axis=1, keepdims=True)
            b_row = jnp.sum(jnp.where(r <= c, f_col, 0.0), axis=0, keepdims=True)
            g_tot = jnp.sum(f_row, axis=1, keepdims=True)
            cums.append((b_col, b_row, g_tot, m_ref[bi * MLSTM_HEADS + h]))
        for (i_col, f_col, i_row, f_row), (b_col, b_row, g_tot, m_prev) in zip(gates, cums):
            log_d = jnp.where(causal, b_col - b_row + i_row, NEG_INF)
            m_inter = b_col + m_prev
            m_t = jnp.maximum(m_inter, jnp.max(log_d, axis=1, keepdims=True))
            stab.append((m_t, jnp.exp(log_d - m_t), jnp.exp(m_inter - m_t)))
        for bi, h in chains:
            qf = q_ref[bi, rows, h * DK:(h + 1) * DK].astype(F32) * (DK ** -0.5)
            qs = qf.astype(BF16)
            k = k_ref[bi, rows, h * DK:(h + 1) * DK]
            v = v_ref[bi, rows, h * DV:(h + 1) * DV]
            qk = lax.dot_general(qs, k, _NT, preferred_element_type=F32)
            qkv.append((qf, qs, k, v, qk))
        for (bi, h), (qf, qs, k, v, qk), (m_t, dmat, a) in zip(chains, qkv, stab):
            c_prev = c_ref[bi * MLSTM_HEADS + h]
            inter.append((qk * dmat, c_prev,
                          jnp.dot(qs, c_prev.astype(BF16), preferred_element_type=F32)))
        for (bi, h), (qf, qs, k, v, qk), (m_t, dmat, a), (s, c_prev, qc) in zip(
                chains, qkv, stab, inter):
            num = jnp.dot(s.astype(BF16), v, preferred_element_type=F32) + a * qc
            qn = jnp.sum(qf * n_ref[bi * MLSTM_HEADS + h], axis=1, keepdims=True)
            den = jnp.sum(s, axis=1, keepdims=True) + a * qn
            houts.append(num / jnp.maximum(jnp.abs(den), jnp.exp(-m_t)))
        for (bi, h), (i_col, f_col, i_row, f_row), (b_col, b_row, g_tot, m_prev), \
                (qf, qs, k, v, qk), (s, c_prev, qc) in zip(chains, gates, cums, qkv, inter):
            idx = bi * MLSTM_HEADS + h
            w_col = g_tot - b_col + i_col
            w_row = g_tot - b_row + i_row
            m_new = jnp.maximum(g_tot + m_prev, jnp.max(w_row, axis=1, keepdims=True))
            wgt = jnp.exp(w_col - m_new)
            decay = jnp.exp(g_tot + m_prev - m_new)
            kw = k.astype(F32) * wgt
            c_ref[idx] = decay * c_prev + lax.dot_general(
                kw.astype(BF16), v, (((0,), (0,)), ((), ())), preferred_element_type=F32)
            n_ref[idx] = decay * n_ref[idx] + jnp.sum(kw, axis=0, keepdims=True)
            m_ref[idx] = m_new
        for (bi, h), hout in zip(chains, houts):
            vcols = pl.ds(h * DV, DV)
            hn = hout * lax.rsqrt(jnp.mean(hout * hout, axis=1, keepdims=True) + NORM_EPS)
            hn = hn * ng_ref[:, vcols]
            y = hn * o_ref[bi, rows, vcols].astype(F32) * z_ref[bi, rows, vcols].astype(F32)
            y_ref[bi, rows, vcols] = y.astype(y_ref.dtype)


def _mlstm(proj, gcol, gtr, norm_g, layer, batch, seq, ts=MLSTM_CHUNK, nb=2):
    m = proj.shape[0]
    nt = seq // ts
    qw, vw = MLSTM_HEADS * MLSTM_QK_DIM, BRANCH_WIDTH
    qblk, kblk = _proj_block("c_q", qw), _proj_block("c_k", qw)
    vblk, oblk, zblk = (_proj_block(n, vw) for n in ("c_v", "c_o", "c_z"))
    proj3 = proj.reshape(batch, seq, proj.shape[-1])
    assert batch % nb == 0

    def tok3(a):
        return a.reshape(batch, seq, a.shape[-1])

    chains = nb * MLSTM_HEADS
    out = pl.pallas_call(
        functools.partial(_mlstm_body, ts // MLSTM_CHUNK, nb),
        grid=(batch // nb, nt),
        in_specs=[
            pl.BlockSpec((nb, ts, qw), lambda b, t: (b, t, qblk)),
            pl.BlockSpec((nb, ts, qw), lambda b, t: (b, t, kblk)),
            pl.BlockSpec((nb, ts, vw), lambda b, t: (b, t, vblk)),
            pl.BlockSpec((nb, ts, vw), lambda b, t: (b, t, oblk)),
            pl.BlockSpec((nb, ts, vw), lambda b, t: (b, t, zblk)),
            pl.BlockSpec((nb, ts, LANES), lambda b, t: (b, t, 0)),
            pl.BlockSpec((nb, GATE_ROWS, ts), lambda b, t: (b, 0, t)),
            pl.BlockSpec((None, 1, vw), lambda b, t: (layer, 0, 0)),
        ],
        out_specs=pl.BlockSpec((nb, ts, vw), lambda b, t: (b, t, 0)),
        out_shape=jax.ShapeDtypeStruct((batch, seq, BRANCH_WIDTH), BF16),
        scratch_shapes=[
            pltpu.VMEM((chains, MLSTM_QK_DIM, MLSTM_V_DIM), F32),
            pltpu.VMEM((chains, 1, MLSTM_QK_DIM), F32),
            pltpu.VMEM((chains, 1, 1), F32),
        ],
        compiler_params=_params("arbitrary", "arbitrary"),
        name="mlstm",
    )(proj3, proj3, proj3, proj3, proj3, tok3(gcol), gtr, norm_g)
    return out.reshape(m, BRANCH_WIDTH)


def _merge_body(ya_ref, yb_ref, yc_ref, wp_ref, g0_ref, g1_ref, g2_ref, o_ref, wb_ref):
    piece = wp_ref.shape[1]
    rows = pl.ds(pl.multiple_of(pl.program_id(1) * piece, piece), piece)

    def cast(dst):
        dst[:, rows, :] = wp_ref[...].astype(BF16)

    def compute(src):
        acc = g0_ref[...].astype(F32) * jnp.dot(ya_ref[...], src[0], preferred_element_type=F32)
        acc = acc + g1_ref[...].astype(F32) * jnp.dot(yb_ref[...], src[1],
                                                      preferred_element_type=F32)
        acc = acc + g2_ref[...].astype(F32) * jnp.dot(yc_ref[...], src[2],
                                                      preferred_element_type=F32)
        o_ref[...] = acc.astype(o_ref.dtype)

    _stream_step(wb_ref, cast, compute)


def _merge(ya, yb, yc, w_branch, proj, layer, tm=1024, tn=512):
    m = ya.shape[0]
    w = BRANCH_WIDTH
    n_col, n_row = D_MODEL // tn, m // tm
    piece = w // n_row
    gate0 = PROJ_COL["gates"] // tn
    assert PROJ_COL["gates"] % tn == 0
    yspec = pl.BlockSpec((tm, w), lambda jj, i: (_row_of(jj, i), 0))

    def gate_spec(branch):
        return pl.BlockSpec(
            (tm, tn), lambda jj, i: (_row_of(jj, i), gate0 + branch * n_col + _col_of(jj)))

    return pl.pallas_call(
        _merge_body,
        grid=(n_col + 1, n_row),
        in_specs=[
            yspec, yspec, yspec,
            pl.BlockSpec((None, N_BRANCHES, piece, tn),
                         lambda jj, i: (layer, 0, i, _cast_col_of(jj, n_col))),
            gate_spec(0), gate_spec(1), gate_spec(2),
        ],
        out_specs=pl.BlockSpec((tm, tn), lambda jj, i: (_row_of(jj, i), _col_of(jj))),
        out_shape=jax.ShapeDtypeStruct((m, D_MODEL), BF16),
        scratch_shapes=[pltpu.VMEM((2, N_BRANCHES, w, tn), BF16)],
        compiler_params=_params("arbitrary", "arbitrary"),
        name="merge",
    )(ya, yb, yc, w_branch, proj, proj, proj)


def _outproj_body(a_ref, wp_ref, t_ref, ssq_ref, wb_ref):
    jj = pl.program_id(0)
    i = pl.program_id(1)
    piece = wp_ref.shape[0]
    rows = pl.ds(pl.multiple_of(i * piece, piece), piece)

    def cast(dst):
        dst[rows, :] = wp_ref[...].astype(BF16)

    def compute(src):
        acc = jnp.dot(a_ref[...], src[...], preferred_element_type=F32)
        t_ref[...] = acc.astype(t_ref.dtype)
        out_rows = pl.ds(pl.multiple_of(i * TM, TM), TM)
        part = jnp.broadcast_to(jnp.sum(acc * acc, axis=-1, keepdims=True), (TM, LANES))

        @pl.when(jj == 1)
        def _():
            ssq_ref[out_rows, :] = part

        @pl.when(jj > 1)
        def _():
            ssq_ref[out_rows, :] += part

    _stream_step(wb_ref, cast, compute)


def _outproj(a, w_out, layer, tn=1024):
    m, k = a.shape
    n = w_out.shape[-1]
    n_col, n_row = n // tn, m // TM
    piece = k // n_row
    return pl.pallas_call(
        _outproj_body,
        grid=(n_col + 1, n_row),
        in_specs=[
            pl.BlockSpec((TM, k), lambda jj, i: (_row_of(jj, i), 0)),
            pl.BlockSpec((None, piece, tn), lambda jj, i: (layer, i, _cast_col_of(jj, n_col))),
        ],
        out_specs=[
            pl.BlockSpec((TM, tn), lambda jj, i: (_row_of(jj, i), _col_of(jj))),
            pl.BlockSpec((m, LANES), lambda jj, i: (0, 0)),
        ],
        out_shape=[
            jax.ShapeDtypeStruct((m, n), BF16),
            jax.ShapeDtypeStruct((m, LANES), F32),
        ],
        scratch_shapes=[pltpu.VMEM((2, k, tn), BF16)],
        compiler_params=_params("arbitrary", "arbitrary"),
        name="outproj",
    )(a, w_out)


def _post_residual(x, t, ssq, g):
    rs = lax.rsqrt(ssq * (1.0 / D_MODEL) + NORM_EPS)
    return x + t.astype(F32) * rs * g


def _postnorm_body(x_ref, t_ref, ssq_ref, g_ref, x1b_ref):
    x1b_ref[...] = _post_residual(x_ref[...], t_ref[...], ssq_ref[:, 0:1], g_ref[...]).astype(BF16)


def _postnorm(x2d, t, ssq, norm_post, layer, tm=TM_ROWWISE):
    m = x2d.shape[0]
    return pl.pallas_call(
        _postnorm_body,
        grid=(m // tm,),
        in_specs=[
            pl.BlockSpec((tm, D_MODEL), lambda i: (i, 0)),
            pl.BlockSpec((tm, D_MODEL), lambda i: (i, 0)),
            pl.BlockSpec((tm, LANES), lambda i: (i, 0)),
            pl.BlockSpec((None, 1, D_MODEL), lambda i: (layer, 0, 0)),
        ],
        out_specs=pl.BlockSpec((tm, D_MODEL), lambda i: (i, 0)),
        out_shape=jax.ShapeDtypeStruct((m, D_MODEL), BF16),
        compiler_params=_params("arbitrary"),
        name="postnorm",
    )(x2d, t, ssq, norm_post)


def _ple_body(p_ref, w_ref, g_ref, e_ref, wb_ref):
    @pl.when(pl.program_id(0) == 0)
    def _():
        wb_ref[...] = w_ref[...].astype(BF16)

    t = jnp.dot(p_ref[...].astype(BF16), wb_ref[...], preferred_element_type=F32)
    e = t * lax.rsqrt(jnp.mean(t * t, axis=-1, keepdims=True) + NORM_EPS) * g_ref[...]
    e_ref[...] = e.astype(e_ref.dtype)


def _ple(p3d, ple_proj, ple_norm, layer, tm=TM_ROWWISE):
    m = p3d.shape[1]
    return pl.pallas_call(
        _ple_body,
        grid=(m // tm,),
        in_specs=[
            pl.BlockSpec((None, tm, PLE_DIM), lambda i: (layer, i, 0)),
            pl.BlockSpec((None, PLE_DIM, D_MODEL), lambda i: (layer, 0, 0)),
            pl.BlockSpec((None, 1, D_MODEL), lambda i: (layer, 0, 0)),
        ],
        out_specs=pl.BlockSpec((tm, D_MODEL), lambda i: (i, 0)),
        out_shape=jax.ShapeDtypeStruct((m, D_MODEL), BF16),
        scratch_shapes=[pltpu.VMEM((PLE_DIM, D_MODEL), BF16)],
        compiler_params=_params("arbitrary"),
        name="ple_embed",
    )(p3d, ple_proj, ple_norm)


def _plegate_body(a_ref, wp_ref, x_ref, t_ref, ssq_ref, g_ref, e_ref, o_ref, wb_ref):
    piece = wp_ref.shape[0]
    rows = pl.ds(pl.multiple_of(pl.program_id(1) * piece, piece), piece)

    def cast(dst):
        dst[rows, :] = wp_ref[...].astype(BF16)

    def compute(src):
        acc = jnp.dot(a_ref[...], src[...], preferred_element_type=F32)
        x1 = _post_residual(x_ref[...], t_ref[...], ssq_ref[:, 0:1], g_ref[...])
        o_ref[...] = x1 + _sigmoid(acc) * e_ref[...].astype(F32)

    _stream_step(wb_ref, cast, compute)


def _plegate(x1b, ple_gate, x2d, t, ssq, norm_post, e, layer, tm=512, tn=1024):
    m, k = x1b.shape
    n = ple_gate.shape[-1]
    n_col, n_row = n // tn, m // tm
    piece = k // n_row
    tile = pl.BlockSpec((tm, tn), lambda jj, i: (_row_of(jj, i), _col_of(jj)))
    return pl.pallas_call(
        _plegate_body,
        grid=(n_col + 1, n_row),
        in_specs=[
            pl.BlockSpec((tm, k), lambda jj, i: (_row_of(jj, i), 0)),
            pl.BlockSpec((None, piece, tn), lambda jj, i: (layer, i, _cast_col_of(jj, n_col))),
            tile, tile,
            pl.BlockSpec((tm, LANES), lambda jj, i: (_row_of(jj, i), 0)),
            pl.BlockSpec((None, 1, tn), lambda jj, i: (layer, 0, _col_of(jj))),
            tile,
        ],
        out_specs=tile,
        out_shape=jax.ShapeDtypeStruct((m, n), F32),
        scratch_shapes=[pltpu.VMEM((2, k, tn), BF16)],
        compiler_params=_params("arbitrary", "arbitrary"),
        name="ple_gate",
    )(x1b, ple_gate, x2d, t, ssq, norm_post, e)


def kernel(x, p, positions, norm_pre, w_in, gmlp_ln_g, gmlp_ln_b, gmlp_ws, gmlp_bs, attn_sinks,
           mlstm_ib, mlstm_fb, mlstm_norm_g, w_branch, w_out, norm_post, ple_proj, ple_norm,
           ple_gate):
    batch, seq, d = x.shape
    depth = w_in.shape[0]
    m = batch * seq
    assert d == D_MODEL and w_in.shape[-1] == N_IN and seq % 512 == 0 and m % TM == 0

    inv_freq = ROPE_THETA ** (-jnp.arange(0, ROT_DIM, 2, dtype=F32) / ROT_DIM)
    lane_d = jnp.arange(LANES) % SWA_HEAD_DIM
    inv_lane = jnp.where(lane_d < ROT_DIM, inv_freq[lane_d % (ROT_DIM // 2)], 0.0).reshape(1, LANES)
    posf = positions.astype(F32).reshape(m, 1)
    p3d = p.reshape(depth, m, PLE_DIM)
    bst = jnp.swapaxes(gmlp_bs, 1, 2)
    gate_bias = jnp.pad(jnp.concatenate([mlstm_ib, mlstm_fb], axis=1).astype(F32),
                        ((0, 0), (0, LANES - GATE_COLS)))

    def rows3(a):
        return a.reshape(depth, 1, a.shape[-1])

    norm_pre, norm_post, ple_norm = rows3(norm_pre), rows3(norm_post), rows3(ple_norm)
    gmlp_ln_g, gmlp_ln_b = rows3(gmlp_ln_g), rows3(gmlp_ln_b)
    mlstm_norm_g = rows3(mlstm_norm_g)
    rope_tab = _rope_call(posf, inv_lane)

    w_t = jnp.swapaxes(w_in, 1, 2)

    xc = x.reshape(m, d)
    for l in range(depth):
        h, gcol, gtr = _prenorm(xc, norm_pre, w_t, gate_bias[l:l + 1], l, batch, seq)
        proj = _proj(h, w_t, l)

        ya = _gmlp(proj, gmlp_ln_g, gmlp_ln_b, gmlp_ws, bst, l)
        yb = _swa(proj, rope_tab, attn_sinks, l, batch, seq)
        yc = _mlstm(proj, gcol, gtr, mlstm_norm_g, l, batch, seq)

        mixed = _merge(ya, yb, yc, w_branch, proj, l)
        t, ssq = _outproj(mixed, w_out, l)
        x1b = _postnorm(xc, t, ssq, norm_post, l)
        e = _ple(p3d, ple_proj, ple_norm, l)
        xc = _plegate(x1b, ple_gate, xc, t, ssq, norm_post, e, l)
    return xc.reshape(batch, seq, d)
```

```python
import functools

import jax
import jax.numpy as jnp
from jax import lax
from jax.experimental import pallas as pl
from jax.experimental.pallas import tpu as pltpu

F32 = jnp.float32
BF16 = jnp.bfloat16

D_MODEL = 4096
PLE_DIM = 256
N_BRANCHES = 3
BRANCH_WIDTH = D_MODEL // 2
GMLP_CHUNK = 128
GMLP_GROUPS = 8
GMLP_GROUP_WIDTH = BRANCH_WIDTH // GMLP_GROUPS
SWA_HEAD_DIM = 64
SWA_Q_HEADS = BRANCH_WIDTH // SWA_HEAD_DIM
SWA_KV_HEADS = 4
SWA_GROUP = SWA_Q_HEADS // SWA_KV_HEADS
SWA_BLOCK = 128
SWA_KV_WIDTH = SWA_KV_HEADS * SWA_HEAD_DIM
ROT_DIM = SWA_HEAD_DIM // 4
ROPE_THETA = 500000.0
MLSTM_HEADS = 4
MLSTM_V_DIM = BRANCH_WIDTH // MLSTM_HEADS
MLSTM_QK_DIM = MLSTM_V_DIM // 2
MLSTM_CHUNK = 256
GATE_SOFTCAP = 15.0
NORM_EPS = 1e-6
NEG_INF = -1e30

LANES = 128
SUBLANES = 8
BF16_ROWS = 16
GATE_ROWS = BF16_ROWS
VMEM_LIMIT = 56 * 1024 * 1024
TM = 1024
TN = 1024
TM_ROWWISE = 512

_SPLIT = (
    ("a_u", BRANCH_WIDTH), ("a_v", BRANCH_WIDTH), ("a_z", BRANCH_WIDTH),
    ("b_q", BRANCH_WIDTH), ("b_k", SWA_KV_WIDTH), ("b_v", SWA_KV_WIDTH), ("b_z", BRANCH_WIDTH),
    ("c_q", MLSTM_HEADS * MLSTM_QK_DIM), ("c_k", MLSTM_HEADS * MLSTM_QK_DIM),
    ("c_v", BRANCH_WIDTH), ("c_i", MLSTM_HEADS), ("c_f", MLSTM_HEADS),
    ("c_o", BRANCH_WIDTH), ("c_z", BRANCH_WIDTH), ("gates", N_BRANCHES * D_MODEL),
)
_SRC = {}
_off = 0
for _name, _size in _SPLIT:
    _SRC[_name] = (_off, _size)
    _off += _size
N_IN = _off
GATE_COLS = 2 * MLSTM_HEADS
GATE_START = _SRC["c_i"][0]


def _params(*sem):
    return pltpu.CompilerParams(dimension_semantics=sem, vmem_limit_bytes=VMEM_LIMIT)


def _gelu(x):
    return 0.5 * x * (1.0 + lax.erf(x * (0.5 ** 0.5)))


def _sigmoid(x):
    return jax.nn.sigmoid(x)


def _softcap(z):
    return GATE_SOFTCAP * jnp.tanh(z / GATE_SOFTCAP)


def _log_sigmoid(x):
    return -(jnp.maximum(-x, 0.0) + jnp.log1p(jnp.exp(-jnp.abs(x))))


def _stream_step(wb_ref, cast, compute):
    jj = pl.program_id(0)
    cast(wb_ref.at[jj % 2])

    @pl.when(jj > 0)
    def _():
        compute(wb_ref.at[(jj + 1) % 2])


def _row_of(jj, i):
    return jnp.where(jj > 0, i, 0)


def _col_of(jj):
    return jnp.maximum(jj - 1, 0)


def _cast_col_of(jj, n_col):
    return jnp.minimum(jj, n_col - 1)


def _prenorm_body(x_ref, g_ref, wif_ref, brow_ref, h_ref, gc_ref, gt_ref, wb_ref):
    @pl.when(pl.program_id(0) == 0)
    def _():
        wb_ref[...] = wif_ref[...].astype(BF16)

    x = x_ref[...]
    y = x * lax.rsqrt(jnp.mean(x * x, axis=-1, keepdims=True) + NORM_EPS) * g_ref[...]
    hb = y.astype(BF16)
    h_ref[...] = hb
    pre = lax.dot_general(hb, wb_ref[...], (((1,), (1,)), ((), ())),
                          preferred_element_type=F32) + brow_ref[...]
    sc = _softcap(pre)
    lane = lax.broadcasted_iota(jnp.int32, sc.shape, 1)
    gc = jnp.where(lane >= MLSTM_HEADS, _log_sigmoid(sc), sc)
    gc_ref[...] = gc
    gt_ref[...] = gc.T[:GATE_ROWS, :]


def _prenorm(x2d, norm_pre, w_t, brow, layer, batch, seq, tm=TM_ROWWISE):
    m = x2d.shape[0]
    per_seq = seq // tm
    assert GATE_START % LANES == 0 and seq % tm == 0
    return pl.pallas_call(
        _prenorm_body,
        grid=(m // tm,),
        in_specs=[
            pl.BlockSpec((tm, D_MODEL), lambda i: (i, 0)),
            pl.BlockSpec((None, 1, D_MODEL), lambda i: (layer, 0, 0)),
            pl.BlockSpec((None, LANES, D_MODEL), lambda i: (layer, GATE_START // LANES, 0)),
            pl.BlockSpec((1, LANES), lambda i: (0, 0)),
        ],
        out_specs=[
            pl.BlockSpec((tm, D_MODEL), lambda i: (i, 0)),
            pl.BlockSpec((tm, LANES), lambda i: (i, 0)),
            pl.BlockSpec((None, GATE_ROWS, tm), lambda i: (i // per_seq, 0, i % per_seq)),
        ],
        out_shape=[
            jax.ShapeDtypeStruct((m, D_MODEL), BF16),
            jax.ShapeDtypeStruct((m, LANES), F32),
            jax.ShapeDtypeStruct((batch, GATE_ROWS, seq), F32),
        ],
        scratch_shapes=[pltpu.VMEM((LANES, D_MODEL), BF16)],
        compiler_params=_params("arbitrary"),
        name="prenorm",
    )(x2d, norm_pre, w_t, brow)


_NT = (((1,), (1,)), ((), ()))

_PROJ_ORDER = (
    ("a_u", "gelu"), ("a_v", "gelu"), ("a_z", "silu"), ("b_q", "linear"), ("b_z", "silu"),
    ("c_q", "linear"), ("c_k", "linear"), ("c_v", "linear"), ("c_o", "sigmoid"), ("c_z", "silu"),
    ("gates", "sigmoid"), ("b_k", "linear"),
)


def _proj_layout():
    tiles, cols, col = [], {}, 0
    for name, kind in _PROJ_ORDER:
        start, width = _SRC[name]
        cols[name] = col
        n_tiles = -(-width // TN)
        tiles += [(start + i * TN, kind) for i in range(n_tiles)]
        col += n_tiles * TN
    cols["b_v"] = cols["b_k"] + _SRC["b_k"][1]
    assert all(s % SUBLANES == 0 and s + TN <= N_IN for s, _ in tiles)
    return tuple(tiles), cols, col


_PROJ_TILES, PROJ_COL, PROJ_WIDTH = _proj_layout()


def _proj_block(name, width):
    assert PROJ_COL[name] % width == 0
    return PROJ_COL[name] // width


_N_GELU = sum(kind == "gelu" for _, kind in _PROJ_TILES)
assert all(kind == "gelu" for _, kind in _PROJ_TILES[:_N_GELU])
_TAIL_TILE = len(_PROJ_TILES) - 1
_TAIL_USED = _SRC["b_k"][1] + _SRC["b_v"][1]
assert _PROJ_ORDER[-1] == ("b_k", "linear") and _TAIL_USED % LANES == 0 and _TAIL_USED < TN


def _tile_is(t, kind):
    hit = False
    for idx, (_, k) in enumerate(_PROJ_TILES):
        if k == kind:
            hit = (t == idx) | hit
    return hit


def _tile_src_start(t):
    start = _PROJ_TILES[0][0] + t * TN
    for idx in range(1, len(_PROJ_TILES)):
        if _PROJ_TILES[idx][0] != _PROJ_TILES[idx - 1][0] + TN:
            start = jnp.where(t >= idx, _PROJ_TILES[idx][0] + (t - idx) * TN, start)
    return start


def _proj_body(h_ref, wp_ref, o_ref, wb_ref):
    jj = pl.program_id(0)
    t = jj - 1
    piece = wp_ref.shape[0]
    rows = pl.ds(pl.multiple_of(pl.program_id(1) * piece, piece), piece)
    wb_ref[jj % 2, rows, :] = wp_ref[...].astype(BF16)

    def matmul():
        return lax.dot_general(h_ref[...], wb_ref[(jj + 1) % 2], _NT, preferred_element_type=F32)

    @pl.when((jj > 0) & (t < _N_GELU))
    def _():
        o_ref[...] = _gelu(matmul()).astype(o_ref.dtype)

    @pl.when((t >= _N_GELU) & (t < _TAIL_TILE))
    def _():
        acc = matmul()
        sig = _sigmoid(acc)
        out = jnp.where(_tile_is(t, "linear"), acc, jnp.where(_tile_is(t, "silu"), acc * sig, sig))
        o_ref[...] = out.astype(o_ref.dtype)

    @pl.when(t == _TAIL_TILE)
    def _():
        acc = lax.dot_general(h_ref[...], wb_ref[(jj + 1) % 2, :_TAIL_USED, :], _NT,
                              preferred_element_type=F32)
        o_ref[:, :_TAIL_USED] = acc.astype(o_ref.dtype)
        o_ref[:, _TAIL_USED:] = jnp.zeros((o_ref.shape[0], TN - _TAIL_USED), o_ref.dtype)


def _proj(h, w_t, layer):
    m, k = h.shape
    n_col, n_row = len(_PROJ_TILES), m // TM
    piece = TN // n_row
    assert TN % n_row == 0 and piece % BF16_ROWS == 0

    def w_index(jj, i):
        src = _tile_src_start(_cast_col_of(jj, n_col))
        return layer, pl.multiple_of(src + i * piece, SUBLANES), 0

    return pl.pallas_call(
        _proj_body,
        grid=(n_col + 1, n_row),
        in_specs=[
            pl.BlockSpec((TM, k), lambda jj, i: (_row_of(jj, i), 0)),
            pl.BlockSpec((None, pl.Element(piece), pl.Element(k)), w_index),
        ],
        out_specs=pl.BlockSpec((TM, TN), lambda jj, i: (_row_of(jj, i), _col_of(jj))),
        out_shape=jax.ShapeDtypeStruct((m, PROJ_WIDTH), BF16),
        scratch_shapes=[pltpu.VMEM((2, TN, k), BF16)],
        compiler_params=_params("arbitrary", "arbitrary"),
        name="proj",
    )(h, w_t)


def _gmlp_body(chunks, u_ref, v_ref, z_ref, lg_ref, lb_ref, ws_ref, bst_ref, o_ref):
    r = lax.broadcasted_iota(jnp.int32, (GMLP_CHUNK, GMLP_CHUNK), 0)
    c = lax.broadcasted_iota(jnp.int32, (GMLP_CHUNK, GMLP_CHUNK), 1)
    tri = r >= c
    wmix = [jnp.where(tri, ws_ref[g], 0.0).astype(BF16) for g in range(GMLP_GROUPS)]
    for ch in range(chunks):
        rows = pl.ds(ch * GMLP_CHUNK, GMLP_CHUNK)
        v = v_ref[rows, :].astype(F32)
        mu = jnp.mean(v, axis=-1, keepdims=True)
        d = v - mu
        var = jnp.mean(d * d, axis=-1, keepdims=True)
        vn = (d * lax.rsqrt(var + NORM_EPS) * lg_ref[...] + lb_ref[...]).astype(BF16)
        for g in range(GMLP_GROUPS):
            cols = pl.ds(g * GMLP_GROUP_WIDTH, GMLP_GROUP_WIDTH)
            mixed = jnp.dot(wmix[g], vn[:, g * GMLP_GROUP_WIDTH:(g + 1) * GMLP_GROUP_WIDTH],
                            preferred_element_type=F32) + bst_ref[:, g:g + 1]
            y = u_ref[rows, cols].astype(F32) * mixed * z_ref[rows, cols].astype(F32)
            o_ref[rows, cols] = y.astype(o_ref.dtype)


def _gmlp(proj, ln_g, ln_b, ws, bst, layer, ts=512):
    m = proj.shape[0]
    w = BRANCH_WIDTH
    ublk, vblk, zblk = (_proj_block(n, w) for n in ("a_u", "a_v", "a_z"))
    return pl.pallas_call(
        functools.partial(_gmlp_body, ts // GMLP_CHUNK),
        grid=(m // ts,),
        in_specs=[
            pl.BlockSpec((ts, w), lambda i: (i, ublk)),
            pl.BlockSpec((ts, w), lambda i: (i, vblk)),
            pl.BlockSpec((ts, w), lambda i: (i, zblk)),
            pl.BlockSpec((None, 1, w), lambda i: (layer, 0, 0)),
            pl.BlockSpec((None, 1, w), lambda i: (layer, 0, 0)),
            pl.BlockSpec((None, GMLP_GROUPS, GMLP_CHUNK, GMLP_CHUNK), lambda i: (layer, 0, 0, 0)),
            pl.BlockSpec((None, GMLP_CHUNK, GMLP_GROUPS), lambda i: (layer, 0, 0)),
        ],
        out_specs=pl.BlockSpec((ts, w), lambda i: (i, 0)),
        out_shape=jax.ShapeDtypeStruct((m, w), BF16),
        compiler_params=_params("arbitrary"),
        name="gmlp",
    )(proj, proj, proj, ln_g, ln_b, ws, bst)


def _rope_tables(pos_ref, inv_ref):
    ang = pos_ref[...] * inv_ref[...]
    d = lax.broadcasted_iota(jnp.int32, ang.shape, 1) % SWA_HEAD_DIM
    half = ROT_DIM // 2
    cosf = jnp.where(d < ROT_DIM, jnp.cos(ang), 1.0)
    sinf = jnp.sin(ang)
    s_lo = jnp.where(d < half, -sinf, 0.0)
    s_hi = jnp.where((d >= half) & (d < ROT_DIM), sinf, 0.0)
    return cosf, s_lo, s_hi


def _rope_tile(t, tables):
    cosf, s_lo, s_hi = tables
    half = ROT_DIM // 2
    up = pltpu.roll(t, LANES - half, 1)
    dn = pltpu.roll(t, half, 1)
    return t * cosf + up * s_lo + dn * s_hi


def _rope_body(pos_ref, inv_ref, o_ref):
    cosf, s_lo, s_hi = _rope_tables(pos_ref, inv_ref)
    o_ref[0] = cosf
    o_ref[1] = s_lo
    o_ref[2] = s_hi


def _rope_call(posf, inv_lane, tm=512):
    m = posf.shape[0]
    return pl.pallas_call(
        _rope_body,
        grid=(m // tm,),
        in_specs=[pl.BlockSpec((tm, 1), lambda i: (i, 0)),
                  pl.BlockSpec((1, LANES), lambda i: (0, 0))],
        out_specs=pl.BlockSpec((3, tm, LANES), lambda i: (0, i, 0)),
        out_shape=jax.ShapeDtypeStruct((3, m, LANES), F32),
        compiler_params=_params("arbitrary"),
        name="rope_tables",
    )(posf, inv_lane)


def _swa_body(layer, q_ref, kc_ref, kp_ref, vc_ref, vp_ref, z_ref, tc_ref, tp_ref, sink_ref, o_ref):
    n = pl.program_id(1)
    tab_c = (tc_ref[0], tc_ref[1], tc_ref[2])
    tab_p = (tp_ref[0], tp_ref[1], tp_ref[2])
    heads_per_tile = LANES // SWA_HEAD_DIM
    tiles_per_group = SWA_GROUP // heads_per_tile
    rows = tiles_per_group * SWA_BLOCK

    lane_half = lax.broadcasted_iota(jnp.int32, (2 * SWA_BLOCK, LANES), 1) // SWA_HEAD_DIM
    key_row = lax.broadcasted_iota(jnp.int32, (2 * SWA_BLOCK, LANES), 0)
    qi = lax.broadcasted_iota(jnp.int32, (rows, LANES), 0) % SWA_BLOCK
    kj = lax.broadcasted_iota(jnp.int32, (rows, LANES), 1)
    mask_prev = (kj > qi) & (n > 0)
    mask_cur = kj <= qi
    sink_slot = lax.broadcasted_iota(jnp.int32, (SWA_BLOCK, LANES), 1) == 0

    scale = SWA_HEAD_DIM ** -0.5
    batches = []
    for kt in range(SWA_KV_WIDTH // LANES):
        cols = pl.ds(kt * LANES, LANES)
        k_band = jnp.concatenate(
            [_rope_tile(kp_ref[:, cols].astype(F32), tab_p),
             _rope_tile(kc_ref[:, cols].astype(F32), tab_c)], axis=0)
        v_band = jnp.concatenate([vp_ref[:, cols], vc_ref[:, cols]], axis=0).astype(F32)
        v_band = jnp.where(key_row == 0, 0.0, v_band)
        for hh in range(heads_per_tile):
            kvh = kt * heads_per_tile + hh
            k_keep = jnp.where(lane_half == hh, k_band, 0.0)
            v_keep = jnp.where(lane_half == hh, v_band, 0.0)
            k_swap = pltpu.roll(k_keep, SWA_HEAD_DIM, 1)
            v_swap = pltpu.roll(v_keep, SWA_HEAD_DIM, 1)
            k_at = [k_keep.astype(BF16), k_swap.astype(BF16)]
            v_at = [v_keep.astype(BF16), v_swap.astype(BF16)]
            if hh == 1:
                k_at.reverse()
                v_at.reverse()
            tile0 = kvh * tiles_per_group
            q_stack = jnp.concatenate(
                [(_rope_tile(q_ref[:, pl.ds((tile0 + t) * LANES, LANES)].astype(F32), tab_c)
                  * scale).astype(BF16) for t in range(tiles_per_group)], axis=0)
            for qh in range(heads_per_tile):
                fill = jnp.concatenate(
                    [jnp.where(sink_slot,
                               sink_ref[layer, (tile0 + t) * heads_per_tile + qh], NEG_INF)
                     for t in range(tiles_per_group)], axis=0)
                batches.append((tile0, q_stack, k_at[qh], v_at[qh], fill))

    scores = []
    for _, q_stack, k_rows, _, fill in batches:
        qk = lax.dot_general(q_stack, k_rows, _NT, preferred_element_type=F32)
        scores.append(jnp.concatenate([jnp.where(mask_prev, qk[:, :LANES], fill),
                                       jnp.where(mask_cur, qk[:, LANES:], NEG_INF)], axis=1))
    maxes = [jnp.max(s, axis=-1, keepdims=True) for s in scores]
    probs = [jnp.exp(s - mx) for s, mx in zip(scores, maxes)]
    sums = [jnp.sum(p, axis=-1, keepdims=True) for p in probs]
    outs = [jnp.dot(p.astype(BF16), bt[3], preferred_element_type=F32) / den
            for p, bt, den in zip(probs, batches, sums)]
    for i in range(0, len(batches), heads_per_tile):
        tile0 = batches[i][0]
        o_stack = sum(outs[i + 1:i + heads_per_tile], outs[i])
        for t in range(tiles_per_group):
            qcols = pl.ds((tile0 + t) * LANES, LANES)
            o_t = o_stack[t * SWA_BLOCK:(t + 1) * SWA_BLOCK, :]
            o_ref[:, qcols] = (o_t * z_ref[:, qcols].astype(F32)).astype(o_ref.dtype)


def _swa(proj, rope_tab, sinks, layer, batch, seq):
    m = proj.shape[0]
    nb = seq // SWA_BLOCK
    w = BRANCH_WIDTH
    qblk, zblk = _proj_block("b_q", w), _proj_block("b_z", w)
    kblk, vblk = _proj_block("b_k", SWA_KV_WIDTH), _proj_block("b_v", SWA_KV_WIDTH)

    def cur(b, n):
        return b * nb + n

    def prev(b, n):
        return b * nb + jnp.maximum(n - 1, 0)

    return pl.pallas_call(
        functools.partial(_swa_body, layer),
        grid=(batch, nb),
        in_specs=[
            pl.BlockSpec((SWA_BLOCK, w), lambda b, n: (cur(b, n), qblk)),
            pl.BlockSpec((SWA_BLOCK, SWA_KV_WIDTH), lambda b, n: (cur(b, n), kblk)),
            pl.BlockSpec((SWA_BLOCK, SWA_KV_WIDTH), lambda b, n: (prev(b, n), kblk)),
            pl.BlockSpec((SWA_BLOCK, SWA_KV_WIDTH), lambda b, n: (cur(b, n), vblk)),
            pl.BlockSpec((SWA_BLOCK, SWA_KV_WIDTH), lambda b, n: (prev(b, n), vblk)),
            pl.BlockSpec((SWA_BLOCK, w), lambda b, n: (cur(b, n), zblk)),
            pl.BlockSpec((3, SWA_BLOCK, LANES), lambda b, n: (0, cur(b, n), 0)),
            pl.BlockSpec((3, SWA_BLOCK, LANES), lambda b, n: (0, prev(b, n), 0)),
            pl.BlockSpec(memory_space=pltpu.SMEM),
        ],
        out_specs=pl.BlockSpec((SWA_BLOCK, w), lambda b, n: (cur(b, n), 0)),
        out_shape=jax.ShapeDtypeStruct((m, w), BF16),
        compiler_params=_params("arbitrary", "arbitrary"),
        name="swa",
    )(proj, proj, proj, proj, proj, proj, rope_tab, rope_tab, sinks)


def _mlstm_body(chunks, nb, q_ref, k_ref, v_ref, o_ref, z_ref, gc_ref, gt_ref, ng_ref, y_ref,
                c_ref, n_ref, m_ref):
    L, DK, DV = MLSTM_CHUNK, MLSTM_QK_DIM, MLSTM_V_DIM

    @pl.when(pl.program_id(1) == 0)
    def _():
        c_ref[...] = jnp.zeros_like(c_ref)
        n_ref[...] = jnp.zeros_like(n_ref)
        m_ref[...] = jnp.zeros_like(m_ref)

    r = lax.broadcasted_iota(jnp.int32, (L, L), 0)
    c = lax.broadcasted_iota(jnp.int32, (L, L), 1)
    causal = r >= c
    lane_g = lax.broadcasted_iota(jnp.int32, (L, LANES), 1)
    row_g = lax.broadcasted_iota(jnp.int32, (GATE_ROWS, L), 0)
    chains = [(bi, h) for bi in range(nb) for h in range(MLSTM_HEADS)]

    for ch in range(chunks):
        rows = pl.ds(ch * L, L)
        gates, cums, stab, qkv, inter, houts = [], [], [], [], [], []
        for bi, h in chains:
            gc = gc_ref[bi, rows, :]
            gt = gt_ref[bi, :, ch * L:(ch + 1) * L]
            i_col = jnp.sum(jnp.where(lane_g == h, gc, 0.0), axis=1, keepdims=True)
            f_col = jnp.sum(jnp.where(lane_g == h + MLSTM_HEADS, gc, 0.0), axis=1, keepdims=True)
            i_row = jnp.sum(jnp.where(row_g == h, gt, 0.0), axis=0, keepdims=True)
            f_row = jnp.sum(jnp.where(row_g == h + MLSTM_HEADS, gt, 0.0), axis=0, keepdims=True)
            gates.append((i_col, f_col, i_row, f_row))
        for (bi, h), (i_col, f_col, i_row, f_row) in zip(chains, gates):
            b_col = jnp.sum(jnp.where(causal, f_row, 0.0), axis=1, keepdims=True)
            b_row = jnp.sum(jnp.where(r <= c, f_col, 0.0), axis=0, keepdims=True)
            g_tot = jnp.sum(f_row, axis=1, keepdims=True)
            cums.append((b_col, b_row, g_tot, m_ref[bi * MLSTM_HEADS + h]))
        for (i_col, f_col, i_row, f_row), (b_col, b_row, g_tot, m_prev) in zip(gates, cums):
            log_d = jnp.where(causal, b_col - b_row + i_row, NEG_INF)
            m_inter = b_col + m_prev
            m_t = jnp.maximum(m_inter, jnp.max(log_d, axis=1, keepdims=True))
            stab.append((m_t, jnp.exp(log_d - m_t), jnp.exp(m_inter - m_t)))
        for bi, h in chains:
            qf = q_ref[bi, rows, h * DK:(h + 1) * DK].astype(F32) * (DK ** -0.5)
            qs = qf.astype(BF16)
            k = k_ref[bi, rows, h * DK:(h + 1) * DK]
            v = v_ref[bi, rows, h * DV:(h + 1) * DV]
            qk = lax.dot_general(qs, k, _NT, preferred_element_type=F32)
            qkv.append((qf, qs, k, v, qk))
        for (bi, h), (qf, qs, k, v, qk), (m_t, dmat, a) in zip(chains, qkv, stab):
            c_prev = c_ref[bi * MLSTM_HEADS + h]
            inter.append((qk * dmat, c_prev,
                          jnp.dot(qs, c_prev.astype(BF16), preferred_element_type=F32)))
        for (bi, h), (qf, qs, k, v, qk), (m_t, dmat, a), (s, c_prev, qc) in zip(
                chains, qkv, stab, inter):
            num = jnp.dot(s.astype(BF16), v, preferred_element_type=F32) + a * qc
            qn = jnp.sum(qf * n_ref[bi * MLSTM_HEADS + h], axis=1, keepdims=True)
            den = jnp.sum(s, axis=1, keepdims=True) + a * qn
            houts.append(num / jnp.maximum(jnp.abs(den), jnp.exp(-m_t)))
        for (bi, h), (i_col, f_col, i_row, f_row), (b_col, b_row, g_tot, m_prev), \
                (qf, qs, k, v, qk), (s, c_prev, qc) in zip(chains, gates, cums, qkv, inter):
            idx = bi * MLSTM_HEADS + h
            w_col = g_tot - b_col + i_col
            w_row = g_tot - b_row + i_row
            m_new = jnp.maximum(g_tot + m_prev, jnp.max(w_row, axis=1, keepdims=True))
            wgt = jnp.exp(w_col - m_new)
            decay = jnp.exp(g_tot + m_prev - m_new)
            kw = k.astype(F32) * wgt
            c_ref[idx] = decay * c_prev + lax.dot_general(
                kw.astype(BF16), v, (((0,), (0,)), ((), ())), preferred_element_type=F32)
            n_ref[idx] = decay * n_ref[idx] + jnp.sum(kw, axis=0, keepdims=True)
            m_ref[idx] = m_new
        for (bi, h), hout in zip(chains, houts):
            vcols = pl.ds(h * DV, DV)
            hn = hout * lax.rsqrt(jnp.mean(hout * hout, axis=1, keepdims=True) + NORM_EPS)
            hn = hn * ng_ref[:, vcols]
            y = hn * o_ref[bi, rows, vcols].astype(F32) * z_ref[bi, rows, vcols].astype(F32)
            y_ref[bi, rows, vcols] = y.astype(y_ref.dtype)


def _mlstm(proj, gcol, gtr, norm_g, layer, batch, seq, ts=MLSTM_CHUNK, nb=2):
    m = proj.shape[0]
    nt = seq // ts
    qw, vw = MLSTM_HEADS * MLSTM_QK_DIM, BRANCH_WIDTH
    qblk, kblk = _proj_block("c_q", qw), _proj_block("c_k", qw)
    vblk, oblk, zblk = (_proj_block(n, vw) for n in ("c_v", "c_o", "c_z"))
    proj3 = proj.reshape(batch, seq, proj.shape[-1])
    assert batch % nb == 0

    def tok3(a):
        return a.reshape(batch, seq, a.shape[-1])

    chains = nb * MLSTM_HEADS
    out = pl.pallas_call(
        functools.partial(_mlstm_body, ts // MLSTM_CHUNK, nb),
        grid=(batch // nb, nt),
        in_specs=[
            pl.BlockSpec((nb, ts, qw), lambda b, t: (b, t, qblk)),
            pl.BlockSpec((nb, ts, qw), lambda b, t: (b, t, kblk)),
            pl.BlockSpec((nb, ts, vw), lambda b, t: (b, t, vblk)),
            pl.BlockSpec((nb, ts, vw), lambda b, t: (b, t, oblk)),
            pl.BlockSpec((nb, ts, vw), lambda b, t: (b, t, zblk)),
            pl.BlockSpec((nb, ts, LANES), lambda b, t: (b, t, 0)),
            pl.BlockSpec((nb, GATE_ROWS, ts), lambda b, t: (b, 0, t)),
            pl.BlockSpec((None, 1, vw), lambda b, t: (layer, 0, 0)),
        ],
        out_specs=pl.BlockSpec((nb, ts, vw), lambda b, t: (b, t, 0)),
        out_shape=jax.ShapeDtypeStruct((batch, seq, BRANCH_WIDTH), BF16),
        scratch_shapes=[
            pltpu.VMEM((chains, MLSTM_QK_DIM, MLSTM_V_DIM), F32),
            pltpu.VMEM((chains, 1, MLSTM_QK_DIM), F32),
            pltpu.VMEM((chains, 1, 1), F32),
        ],
        compiler_params=_params("arbitrary", "arbitrary"),
        name="mlstm",
    )(proj3, proj3, proj3, proj3, proj3, tok3(gcol), gtr, norm_g)
    return out.reshape(m, BRANCH_WIDTH)


def _merge_body(ya_ref, yb_ref, yc_ref, wp_ref, g0_ref, g1_ref, g2_ref, o_ref, wb_ref):
    piece = wp_ref.shape[1]
    rows = pl.ds(pl.multiple_of(pl.program_id(1) * piece, piece), piece)

    def cast(dst):
        dst[:, rows, :] = wp_ref[...].astype(BF16)

    def compute(src):
        acc = g0_ref[...].astype(F32) * jnp.dot(ya_ref[...], src[0], preferred_element_type=F32)
        acc = acc + g1_ref[...].astype(F32) * jnp.dot(yb_ref[...], src[1],
                                                      preferred_element_type=F32)
        acc = acc + g2_ref[...].astype(F32) * jnp.dot(yc_ref[...], src[2],
                                                      preferred_element_type=F32)
        o_ref[...] = acc.astype(o_ref.dtype)

    _stream_step(wb_ref, cast, compute)


def _merge(ya, yb, yc, w_branch, proj, layer, tm=1024, tn=512):
    m = ya.shape[0]
    w = BRANCH_WIDTH
    n_col, n_row = D_MODEL // tn, m // tm
    piece = w // n_row
    gate0 = PROJ_COL["gates"] // tn
    assert PROJ_COL["gates"] % tn == 0
    yspec = pl.BlockSpec((tm, w), lambda jj, i: (_row_of(jj, i), 0))

    def gate_spec(branch):
        return pl.BlockSpec(
            (tm, tn), lambda jj, i: (_row_of(jj, i), gate0 + branch * n_col + _col_of(jj)))

    return pl.pallas_call(
        _merge_body,
        grid=(n_col + 1, n_row),
        in_specs=[
            yspec, yspec, yspec,
            pl.BlockSpec((None, N_BRANCHES, piece, tn),
                         lambda jj, i: (layer, 0, i, _cast_col_of(jj, n_col))),
            gate_spec(0), gate_spec(1), gate_spec(2),
        ],
        out_specs=pl.BlockSpec((tm, tn), lambda jj, i: (_row_of(jj, i), _col_of(jj))),
        out_shape=jax.ShapeDtypeStruct((m, D_MODEL), BF16),
        scratch_shapes=[pltpu.VMEM((2, N_BRANCHES, w, tn), BF16)],
        compiler_params=_params("arbitrary", "arbitrary"),
        name="merge",
    )(ya, yb, yc, w_branch, proj, proj, proj)


def _outproj_body(a_ref, wp_ref, t_ref, ssq_ref, wb_ref):
    jj = pl.program_id(0)
    i = pl.program_id(1)
    piece = wp_ref.shape[0]
    rows = pl.ds(pl.multiple_of(i * piece, piece), piece)

    def cast(dst):
        dst[rows, :] = wp_ref[...].astype(BF16)

    def compute(src):
        acc = jnp.dot(a_ref[...], src[...], preferred_element_type=F32)
        t_ref[...] = acc.astype(t_ref.dtype)
        out_rows = pl.ds(pl.multiple_of(i * TM, TM), TM)
        part = jnp.broadcast_to(jnp.sum(acc * acc, axis=-1, keepdims=True), (TM, LANES))

        @pl.when(jj == 1)
        def _():
            ssq_ref[out_rows, :] = part

        @pl.when(jj > 1)
        def _():
            ssq_ref[out_rows, :] += part

    _stream_step(wb_ref, cast, compute)


def _outproj(a, w_out, layer, tn=1024):
    m, k = a.shape
    n = w_out.shape[-1]
    n_col, n_row = n // tn, m // TM
    piece = k // n_row
    return pl.pallas_call(
        _outproj_body,
        grid=(n_col + 1, n_row),
        in_specs=[
            pl.BlockSpec((TM, k), lambda jj, i: (_row_of(jj, i), 0)),
            pl.BlockSpec((None, piece, tn), lambda jj, i: (layer, i, _cast_col_of(jj, n_col))),
        ],
        out_specs=[
            pl.BlockSpec((TM, tn), lambda jj, i: (_row_of(jj, i), _col_of(jj))),
            pl.BlockSpec((m, LANES), lambda jj, i: (0, 0)),
        ],
        out_shape=[
            jax.ShapeDtypeStruct((m, n), BF16),
            jax.ShapeDtypeStruct((m, LANES), F32),
        ],
        scratch_shapes=[pltpu.VMEM((2, k, tn), BF16)],
        compiler_params=_params("arbitrary", "arbitrary"),
        name="outproj",
    )(a, w_out)


def _post_residual(x, t, ssq, g):
    rs = lax.rsqrt(ssq * (1.0 / D_MODEL) + NORM_EPS)
    return x + t.astype(F32) * rs * g


def _postnorm_body(x_ref, t_ref, ssq_ref, g_ref, x1b_ref):
    x1b_ref[...] = _post_residual(x_ref[...], t_ref[...], ssq_ref[:, 0:1], g_ref[...]).astype(BF16)


def _postnorm(x2d, t, ssq, norm_post, layer, tm=TM_ROWWISE):
    m = x2d.shape[0]
    return pl.pallas_call(
        _postnorm_body,
        grid=(m // tm,),
        in_specs=[
            pl.BlockSpec((tm, D_MODEL), lambda i: (i, 0)),
            pl.BlockSpec((tm, D_MODEL), lambda i: (i, 0)),
            pl.BlockSpec((tm, LANES), lambda i: (i, 0)),
            pl.BlockSpec((None, 1, D_MODEL), lambda i: (layer, 0, 0)),
        ],
        out_specs=pl.BlockSpec((tm, D_MODEL), lambda i: (i, 0)),
        out_shape=jax.ShapeDtypeStruct((m, D_MODEL), BF16),
        compiler_params=_params("arbitrary"),
        name="postnorm",
    )(x2d, t, ssq, norm_post)


def _ple_body(p_ref, w_ref, g_ref, e_ref, wb_ref):
    @pl.when(pl.program_id(0) == 0)
    def _():
        wb_ref[...] = w_ref[...].astype(BF16)

    t = jnp.dot(p_ref[...].astype(BF16), wb_ref[...], preferred_element_type=F32)
    e = t * lax.rsqrt(jnp.mean(t * t, axis=-1, keepdims=True) + NORM_EPS) * g_ref[...]
    e_ref[...] = e.astype(e_ref.dtype)


def _ple(p3d, ple_proj, ple_norm, layer, tm=TM_ROWWISE):
    m = p3d.shape[1]
    return pl.pallas_call(
        _ple_body,
        grid=(m // tm,),
        in_specs=[
            pl.BlockSpec((None, tm, PLE_DIM), lambda i: (layer, i, 0)),
            pl.BlockSpec((None, PLE_DIM, D_MODEL), lambda i: (layer, 0, 0)),
            pl.BlockSpec((None, 1, D_MODEL), lambda i: (layer, 0, 0)),
        ],
        out_specs=pl.BlockSpec((tm, D_MODEL), lambda i: (i, 0)),
        out_shape=jax.ShapeDtypeStruct((m, D_MODEL), BF16),
        scratch_shapes=[pltpu.VMEM((PLE_DIM, D_MODEL), BF16)],
        compiler_params=_params("arbitrary"),
        name="ple_embed",
    )(p3d, ple_proj, ple_norm)


def _plegate_body(a_ref, wp_ref, x_ref, t_ref, ssq_ref, g_ref, e_ref, o_ref, wb_ref):
    piece = wp_ref.shape[0]
    rows = pl.ds(pl.multiple_of(pl.program_id(1) * piece, piece), piece)

    def cast(dst):
        dst[rows, :] = wp_ref[...].astype(BF16)

    def compute(src):
        acc = jnp.dot(a_ref[...], src[...], preferred_element_type=F32)
        x1 = _post_residual(x_ref[...], t_ref[...], ssq_ref[:, 0:1], g_ref[...])
        o_ref[...] = x1 + _sigmoid(acc) * e_ref[...].astype(F32)

    _stream_step(wb_ref, cast, compute)


def _plegate(x1b, ple_gate, x2d, t, ssq, norm_post, e, layer, tm=512, tn=1024):
    m, k = x1b.shape
    n = ple_gate.shape[-1]
    n_col, n_row = n // tn, m // tm
    piece = k // n_row
    tile = pl.BlockSpec((tm, tn), lambda jj, i: (_row_of(jj, i), _col_of(jj)))
    return pl.pallas_call(
        _plegate_body,
        grid=(n_col + 1, n_row),
        in_specs=[
            pl.BlockSpec((tm, k), lambda jj, i: (_row_of(jj, i), 0)),
            pl.BlockSpec((None, piece, tn), lambda jj, i: (layer, i, _cast_col_of(jj, n_col))),
            tile, tile,
            pl.BlockSpec((tm, LANES), lambda jj, i: (_row_of(jj, i), 0)),
            pl.BlockSpec((None, 1, tn), lambda jj, i: (layer, 0, _col_of(jj))),
            tile,
        ],
        out_specs=tile,
        out_shape=jax.ShapeDtypeStruct((m, n), F32),
        scratch_shapes=[pltpu.VMEM((2, k, tn), BF16)],
        compiler_params=_params("arbitrary", "arbitrary"),
        name="ple_gate",
    )(x1b, ple_gate, x2d, t, ssq, norm_post, e)


def kernel(x, p, positions, norm_pre, w_in, gmlp_ln_g, gmlp_ln_b, gmlp_ws, gmlp_bs, attn_sinks,
           mlstm_ib, mlstm_fb, mlstm_norm_g, w_branch, w_out, norm_post, ple_proj, ple_norm,
           ple_gate):
    batch, seq, d = x.shape
    depth = w_in.shape[0]
    m = batch * seq
    assert d == D_MODEL and w_in.shape[-1] == N_IN and seq % 512 == 0 and m % TM == 0

    inv_freq = ROPE_THETA ** (-jnp.arange(0, ROT_DIM, 2, dtype=F32) / ROT_DIM)
    lane_d = jnp.arange(LANES) % SWA_HEAD_DIM
    inv_lane = jnp.where(lane_d < ROT_DIM, inv_freq[lane_d % (ROT_DIM // 2)], 0.0).reshape(1, LANES)
    posf = positions.astype(F32).reshape(m, 1)
    p3d = p.reshape(depth, m, PLE_DIM)
    bst = jnp.swapaxes(gmlp_bs, 1, 2)
    gate_bias = jnp.pad(jnp.concatenate([mlstm_ib, mlstm_fb], axis=1).astype(F32),
                        ((0, 0), (0, LANES - GATE_COLS)))

    def rows3(a):
        return a.reshape(depth, 1, a.shape[-1])

    norm_pre, norm_post, ple_norm = rows3(norm_pre), rows3(norm_post), rows3(ple_norm)
    gmlp_ln_g, gmlp_ln_b = rows3(gmlp_ln_g), rows3(gmlp_ln_b)
    mlstm_norm_g = rows3(mlstm_norm_g)
    rope_tab = _rope_call(posf, inv_lane)

    w_t = jnp.swapaxes(w_in, 1, 2)

    xc = x.reshape(m, d)
    for l in range(depth):
        h, gcol, gtr = _prenorm(xc, norm_pre, w_t, gate_bias[l:l + 1], l, batch, seq)
        proj = _proj(h, w_t, l)

        ya = _gmlp(proj, gmlp_ln_g, gmlp_ln_b, gmlp_ws, bst, l)
        yb = _swa(proj, rope_tab, attn_sinks, l, batch, seq)
        yc = _mlstm(proj, gcol, gtr, mlstm_norm_g, l, batch, seq)

        mixed = _merge(ya, yb, yc, w_branch, proj, l)
        t, ssq = _outproj(mixed, w_out, l)
        x1b = _postnorm(xc, t, ssq, norm_post, l)
        e = _ple(p3d, ple_proj, ple_norm, l)
        xc = _plegate(x1b, ple_gate, xc, t, ssq, norm_post, e, l)
    return xc.reshape(batch, seq, d)
```

```python
import functools

import jax
import jax.numpy as jnp
from jax import lax
from jax.experimental import pallas as pl
from jax.experimental.pallas import tpu as pltpu

F32 = jnp.float32
BF16 = jnp.bfloat16

D_MODEL = 4096
PLE_DIM = 256
N_BRANCHES = 3
BRANCH_WIDTH = D_MODEL // 2
GMLP_CHUNK = 128
GMLP_GROUPS = 8
GMLP_GROUP_WIDTH = BRANCH_WIDTH // GMLP_GROUPS
SWA_HEAD_DIM = 64
SWA_Q_HEADS = BRANCH_WIDTH // SWA_HEAD_DIM
SWA_KV_HEADS = 4
SWA_GROUP = SWA_Q_HEADS // SWA_KV_HEADS
SWA_BLOCK = 128
SWA_KV_WIDTH = SWA_KV_HEADS * SWA_HEAD_DIM
ROT_DIM = SWA_HEAD_DIM // 4
ROPE_THETA = 500000.0
MLSTM_HEADS = 4
MLSTM_V_DIM = BRANCH_WIDTH // MLSTM_HEADS
MLSTM_QK_DIM = MLSTM_V_DIM // 2
MLSTM_CHUNK = 256
GATE_SOFTCAP = 15.0
NORM_EPS = 1e-6
NEG_INF = -1e30

LANES = 128
SUBLANES = 8
BF16_ROWS = 16
GATE_ROWS = BF16_ROWS
VMEM_LIMIT = 56 * 1024 * 1024
TM = 1024
TN = 1024
TM_ROWWISE = 512

_SPLIT = (
    ("a_u", BRANCH_WIDTH), ("a_v", BRANCH_WIDTH), ("a_z", BRANCH_WIDTH),
    ("b_q", BRANCH_WIDTH), ("b_k", SWA_KV_WIDTH), ("b_v", SWA_KV_WIDTH), ("b_z", BRANCH_WIDTH),
    ("c_q", MLSTM_HEADS * MLSTM_QK_DIM), ("c_k", MLSTM_HEADS * MLSTM_QK_DIM),
    ("c_v", BRANCH_WIDTH), ("c_i", MLSTM_HEADS), ("c_f", MLSTM_HEADS),
    ("c_o", BRANCH_WIDTH), ("c_z", BRANCH_WIDTH), ("gates", N_BRANCHES * D_MODEL),
)
_SRC = {}
_off = 0
for _name, _size in _SPLIT:
    _SRC[_name] = (_off, _size)
    _off += _size
N_IN = _off
GATE_COLS = 2 * MLSTM_HEADS
GATE_START = _SRC["c_i"][0]


def _params(*sem):
    return pltpu.CompilerParams(dimension_semantics=sem, vmem_limit_bytes=VMEM_LIMIT)


def _gelu(x):
    return 0.5 * x * (1.0 + lax.erf(x * (0.5 ** 0.5)))


def _sigmoid(x):
    return jax.nn.sigmoid(x)


def _softcap(z):
    return GATE_SOFTCAP * jnp.tanh(z / GATE_SOFTCAP)


def _log_sigmoid(x):
    return -(jnp.maximum(-x, 0.0) + jnp.log1p(jnp.exp(-jnp.abs(x))))


def _stream_step(wb_ref, cast, compute, cast_every=1):
    jj = pl.program_id(0)
    do_cast = (jj < pl.num_programs(0) - 1) & (pl.program_id(1) % cast_every == 0)

    @pl.when(do_cast)
    def _():
        cast(wb_ref.at[jj % 2])

    @pl.when(jj > 0)
    def _():
        compute(wb_ref.at[(jj + 1) % 2])


def _row_of(jj, i):
    return jnp.where(jj > 0, i, 0)


def _col_of(jj):
    return jnp.maximum(jj - 1, 0)


def _cast_col_of(jj, n_col):
    return jnp.minimum(jj, n_col - 1)


def _prenorm_body(x_ref, g_ref, wif_ref, brow_ref, h_ref, gc_ref, gt_ref, wb_ref):
    @pl.when(pl.program_id(0) == 0)
    def _():
        wb_ref[...] = wif_ref[...].astype(BF16)

    x = x_ref[...]
    y = x * lax.rsqrt(jnp.mean(x * x, axis=-1, keepdims=True) + NORM_EPS) * g_ref[...]
    hb = y.astype(BF16)
    h_ref[...] = hb
    pre = lax.dot_general(hb, wb_ref[...], (((1,), (1,)), ((), ())),
                          preferred_element_type=F32) + brow_ref[...]
    sc = _softcap(pre)
    lane = lax.broadcasted_iota(jnp.int32, sc.shape, 1)
    gc = jnp.where(lane >= MLSTM_HEADS, _log_sigmoid(sc), sc)
    gc_ref[...] = gc
    gt_ref[...] = gc.T[:GATE_ROWS, :]


def _prenorm(x2d, norm_pre, w_t, brow, layer, batch, seq, tm=TM_ROWWISE):
    m = x2d.shape[0]
    per_seq = seq // tm
    assert GATE_START % LANES == 0 and seq % tm == 0
    return pl.pallas_call(
        _prenorm_body,
        grid=(m // tm,),
        in_specs=[
            pl.BlockSpec((tm, D_MODEL), lambda i: (i, 0)),
            pl.BlockSpec((None, 1, D_MODEL), lambda i: (layer, 0, 0)),
            pl.BlockSpec((None, LANES, D_MODEL), lambda i: (layer, GATE_START // LANES, 0)),
            pl.BlockSpec((1, LANES), lambda i: (0, 0)),
        ],
        out_specs=[
            pl.BlockSpec((tm, D_MODEL), lambda i: (i, 0)),
            pl.BlockSpec((tm, LANES), lambda i: (i, 0)),
            pl.BlockSpec((None, GATE_ROWS, tm), lambda i: (i // per_seq, 0, i % per_seq)),
        ],
        out_shape=[
            jax.ShapeDtypeStruct((m, D_MODEL), BF16),
            jax.ShapeDtypeStruct((m, LANES), F32),
            jax.ShapeDtypeStruct((batch, GATE_ROWS, seq), F32),
        ],
        scratch_shapes=[pltpu.VMEM((LANES, D_MODEL), BF16)],
        compiler_params=_params("arbitrary"),
        name="prenorm",
    )(x2d, norm_pre, w_t, brow)


_NT = (((1,), (1,)), ((), ()))

_PROJ_ORDER = (
    ("a_u", "gelu"), ("a_v", "gelu"), ("a_z", "silu"), ("b_q", "linear"), ("b_z", "silu"),
    ("c_q", "linear"), ("c_k", "linear"), ("c_v", "linear"), ("c_o", "sigmoid"), ("c_z", "silu"),
    ("gates", "sigmoid"), ("b_k", "linear"),
)


def _proj_layout():
    tiles, cols, col = [], {}, 0
    for name, kind in _PROJ_ORDER:
        start, width = _SRC[name]
        cols[name] = col
        n_tiles = -(-width // TN)
        tiles += [(start + i * TN, kind) for i in range(n_tiles)]
        col += n_tiles * TN
    cols["b_v"] = cols["b_k"] + _SRC["b_k"][1]
    assert all(s % SUBLANES == 0 and s + TN <= N_IN for s, _ in tiles)
    return tuple(tiles), cols, col


_PROJ_TILES, PROJ_COL, PROJ_WIDTH = _proj_layout()


def _proj_block(name, width):
    assert PROJ_COL[name] % width == 0
    return PROJ_COL[name] // width


_N_GELU = sum(kind == "gelu" for _, kind in _PROJ_TILES)
assert all(kind == "gelu" for _, kind in _PROJ_TILES[:_N_GELU])
_TAIL_TILE = len(_PROJ_TILES) - 1
_TAIL_USED = _SRC["b_k"][1] + _SRC["b_v"][1]
assert _PROJ_ORDER[-1] == ("b_k", "linear") and _TAIL_USED % LANES == 0 and _TAIL_USED < TN


def _tile_is(t, kind):
    hit = False
    for idx, (_, k) in enumerate(_PROJ_TILES):
        if k == kind:
            hit = (t == idx) | hit
    return hit


def _tile_src_start(t):
    start = _PROJ_TILES[0][0] + t * TN
    for idx in range(1, len(_PROJ_TILES)):
        if _PROJ_TILES[idx][0] != _PROJ_TILES[idx - 1][0] + TN:
            start = jnp.where(t >= idx, _PROJ_TILES[idx][0] + (t - idx) * TN, start)
    return start


def _proj_body(h_ref, wp_ref, o_ref, wb_ref):
    jj = pl.program_id(0)
    t = jj - 1
    piece = wp_ref.shape[0]
    rows = pl.ds(pl.multiple_of(pl.program_id(1) * piece, piece), piece)

    @pl.when(jj < pl.num_programs(0) - 1)
    def _():
        wb_ref[jj % 2, rows, :] = wp_ref[...].astype(BF16)

    def matmul():
        return lax.dot_general(h_ref[...], wb_ref[(jj + 1) % 2], _NT, preferred_element_type=F32)

    @pl.when((jj > 0) & (t < _N_GELU))
    def _():
        o_ref[...] = _gelu(matmul()).astype(o_ref.dtype)

    @pl.when((t >= _N_GELU) & (t < _TAIL_TILE))
    def _():
        acc = matmul()
        sig = _sigmoid(acc)
        out = jnp.where(_tile_is(t, "linear"), acc, jnp.where(_tile_is(t, "silu"), acc * sig, sig))
        o_ref[...] = out.astype(o_ref.dtype)

    @pl.when(t == _TAIL_TILE)
    def _():
        acc = lax.dot_general(h_ref[...], wb_ref[(jj + 1) % 2, :_TAIL_USED, :], _NT,
                              preferred_element_type=F32)
        o_ref[:, :_TAIL_USED] = acc.astype(o_ref.dtype)
        o_ref[:, _TAIL_USED:] = jnp.zeros((o_ref.shape[0], TN - _TAIL_USED), o_ref.dtype)


def _proj(h, w_t, layer):
    m, k = h.shape
    n_col, n_row = len(_PROJ_TILES), m // TM
    piece = TN // n_row
    assert TN % n_row == 0 and piece % BF16_ROWS == 0

    def w_index(jj, i):
        src = _tile_src_start(_cast_col_of(jj, n_col))
        return layer, pl.multiple_of(src + i * piece, SUBLANES), 0

    return pl.pallas_call(
        _proj_body,
        grid=(n_col + 1, n_row),
        in_specs=[
            pl.BlockSpec((TM, k), lambda jj, i: (_row_of(jj, i), 0)),
            pl.BlockSpec((None, pl.Element(piece), pl.Element(k)), w_index),
        ],
        out_specs=pl.BlockSpec((TM, TN), lambda jj, i: (_row_of(jj, i), _col_of(jj))),
        out_shape=jax.ShapeDtypeStruct((m, PROJ_WIDTH), BF16),
        scratch_shapes=[pltpu.VMEM((2, TN, k), BF16)],
        compiler_params=_params("arbitrary", "arbitrary"),
        name="proj",
    )(h, w_t)


def _gmlp_body(chunks, u_ref, v_ref, z_ref, lg_ref, lb_ref, ws_ref, bst_ref, o_ref):
    r = lax.broadcasted_iota(jnp.int32, (GMLP_CHUNK, GMLP_CHUNK), 0)
    c = lax.broadcasted_iota(jnp.int32, (GMLP_CHUNK, GMLP_CHUNK), 1)
    tri = r >= c
    wmix = [jnp.where(tri, ws_ref[g], 0.0).astype(BF16) for g in range(GMLP_GROUPS)]
    for ch in range(chunks):
        rows = pl.ds(ch * GMLP_CHUNK, GMLP_CHUNK)
        v = v_ref[rows, :].astype(F32)
        mu = jnp.mean(v, axis=-1, keepdims=True)
        d = v - mu
        var = jnp.mean(d * d, axis=-1, keepdims=True)
        vn = (d * lax.rsqrt(var + NORM_EPS) * lg_ref[...] + lb_ref[...]).astype(BF16)
        for g in range(GMLP_GROUPS):
            cols = pl.ds(g * GMLP_GROUP_WIDTH, GMLP_GROUP_WIDTH)
            mixed = jnp.dot(wmix[g], vn[:, g * GMLP_GROUP_WIDTH:(g + 1) * GMLP_GROUP_WIDTH],
                            preferred_element_type=F32) + bst_ref[:, g:g + 1]
            y = u_ref[rows, cols].astype(F32) * mixed * z_ref[rows, cols].astype(F32)
            o_ref[rows, cols] = y.astype(o_ref.dtype)


def _gmlp(proj, ln_g, ln_b, ws, bst, layer, ts=512):
    m = proj.shape[0]
    w = BRANCH_WIDTH
    ublk, vblk, zblk = (_proj_block(n, w) for n in ("a_u", "a_v", "a_z"))
    return pl.pallas_call(
        functools.partial(_gmlp_body, ts // GMLP_CHUNK),
        grid=(m // ts,),
        in_specs=[
            pl.BlockSpec((ts, w), lambda i: (i, ublk)),
            pl.BlockSpec((ts, w), lambda i: (i, vblk)),
            pl.BlockSpec((ts, w), lambda i: (i, zblk)),
            pl.BlockSpec((None, 1, w), lambda i: (layer, 0, 0)),
            pl.BlockSpec((None, 1, w), lambda i: (layer, 0, 0)),
            pl.BlockSpec((None, GMLP_GROUPS, GMLP_CHUNK, GMLP_CHUNK), lambda i: (layer, 0, 0, 0)),
            pl.BlockSpec((None, GMLP_CHUNK, GMLP_GROUPS), lambda i: (layer, 0, 0)),
        ],
        out_specs=pl.BlockSpec((ts, w), lambda i: (i, 0)),
        out_shape=jax.ShapeDtypeStruct((m, w), BF16),
        compiler_params=_params("arbitrary"),
        name="gmlp",
    )(proj, proj, proj, ln_g, ln_b, ws, bst)


def _rope_tables(pos_ref, inv_ref):
    ang = pos_ref[...] * inv_ref[...]
    d = lax.broadcasted_iota(jnp.int32, ang.shape, 1) % SWA_HEAD_DIM
    half = ROT_DIM // 2
    cosf = jnp.where(d < ROT_DIM, jnp.cos(ang), 1.0)
    sinf = jnp.sin(ang)
    s_lo = jnp.where(d < half, -sinf, 0.0)
    s_hi = jnp.where((d >= half) & (d < ROT_DIM), sinf, 0.0)
    return cosf, s_lo, s_hi


def _rope_tile(t, tables):
    cosf, s_lo, s_hi = tables
    half = ROT_DIM // 2
    up = pltpu.roll(t, LANES - half, 1)
    dn = pltpu.roll(t, half, 1)
    return t * cosf + up * s_lo + dn * s_hi


def _rope_body(pos_ref, inv_ref, o_ref):
    cosf, s_lo, s_hi = _rope_tables(pos_ref, inv_ref)
    o_ref[0] = cosf
    o_ref[1] = s_lo
    o_ref[2] = s_hi


def _rope_call(posf, inv_lane, tm=512):
    m = posf.shape[0]
    return pl.pallas_call(
        _rope_body,
        grid=(m // tm,),
        in_specs=[pl.BlockSpec((tm, 1), lambda i: (i, 0)),
                  pl.BlockSpec((1, LANES), lambda i: (0, 0))],
        out_specs=pl.BlockSpec((3, tm, LANES), lambda i: (0, i, 0)),
        out_shape=jax.ShapeDtypeStruct((3, m, LANES), F32),
        compiler_params=_params("arbitrary"),
        name="rope_tables",
    )(posf, inv_lane)


def _swa_body(layer, q_ref, kc_ref, kp_ref, vc_ref, vp_ref, z_ref, tc_ref, tp_ref, sink_ref, o_ref):
    n = pl.program_id(1)
    tab_c = (tc_ref[0], tc_ref[1], tc_ref[2])
    tab_p = (tp_ref[0], tp_ref[1], tp_ref[2])
    heads_per_tile = LANES // SWA_HEAD_DIM
    tiles_per_group = SWA_GROUP // heads_per_tile
    rows = tiles_per_group * SWA_BLOCK

    lane_half = lax.broadcasted_iota(jnp.int32, (2 * SWA_BLOCK, LANES), 1) // SWA_HEAD_DIM
    key_row = lax.broadcasted_iota(jnp.int32, (2 * SWA_BLOCK, LANES), 0)
    qi = lax.broadcasted_iota(jnp.int32, (rows, LANES), 0) % SWA_BLOCK
    kj = lax.broadcasted_iota(jnp.int32, (rows, LANES), 1)
    mask_prev = (kj > qi) & (n > 0)
    mask_cur = kj <= qi
    sink_slot = lax.broadcasted_iota(jnp.int32, (SWA_BLOCK, LANES), 1) == 0

    scale = SWA_HEAD_DIM ** -0.5
    batches = []
    for kt in range(SWA_KV_WIDTH // LANES):
        cols = pl.ds(kt * LANES, LANES)
        k_band = jnp.concatenate(
            [_rope_tile(kp_ref[:, cols].astype(F32), tab_p),
             _rope_tile(kc_ref[:, cols].astype(F32), tab_c)], axis=0)
        v_band = jnp.concatenate([vp_ref[:, cols], vc_ref[:, cols]], axis=0).astype(F32)
        v_band = jnp.where(key_row == 0, 0.0, v_band)
        for hh in range(heads_per_tile):
            kvh = kt * heads_per_tile + hh
            k_keep = jnp.where(lane_half == hh, k_band, 0.0)
            v_keep = jnp.where(lane_half == hh, v_band, 0.0)
            k_swap = pltpu.roll(k_keep, SWA_HEAD_DIM, 1)
            v_swap = pltpu.roll(v_keep, SWA_HEAD_DIM, 1)
            k_at = [k_keep.astype(BF16), k_swap.astype(BF16)]
            v_at = [v_keep.astype(BF16), v_swap.astype(BF16)]
            if hh == 1:
                k_at.reverse()
                v_at.reverse()
            tile0 = kvh * tiles_per_group
            q_stack = jnp.concatenate(
                [(_rope_tile(q_ref[:, pl.ds((tile0 + t) * LANES, LANES)].astype(F32), tab_c)
                  * scale).astype(BF16) for t in range(tiles_per_group)], axis=0)
            for qh in range(heads_per_tile):
                fill = jnp.concatenate(
                    [jnp.where(sink_slot,
                               sink_ref[layer, (tile0 + t) * heads_per_tile + qh], NEG_INF)
                     for t in range(tiles_per_group)], axis=0)
                batches.append((tile0, q_stack, k_at[qh], v_at[qh], fill))

    scores = []
    for _, q_stack, k_rows, _, fill in batches:
        qk = lax.dot_general(q_stack, k_rows, _NT, preferred_element_type=F32)
        scores.append(jnp.concatenate([jnp.where(mask_prev, qk[:, :LANES], fill),
                                       jnp.where(mask_cur, qk[:, LANES:], NEG_INF)], axis=1))
    maxes = [jnp.max(s, axis=-1, keepdims=True) for s in scores]
    probs = [jnp.exp(s - mx) for s, mx in zip(scores, maxes)]
    sums = [jnp.sum(p, axis=-1, keepdims=True) for p in probs]
    outs = [jnp.dot(p.astype(BF16), bt[3], preferred_element_type=F32) / den
            for p, bt, den in zip(probs, batches, sums)]
    for i in range(0, len(batches), heads_per_tile):
        tile0 = batches[i][0]
        o_stack = sum(outs[i + 1:i + heads_per_tile], outs[i])
        for t in range(tiles_per_group):
            qcols = pl.ds((tile0 + t) * LANES, LANES)
            o_t = o_stack[t * SWA_BLOCK:(t + 1) * SWA_BLOCK, :]
            o_ref[:, qcols] = (o_t * z_ref[:, qcols].astype(F32)).astype(o_ref.dtype)


def _swa(proj, rope_tab, sinks, layer, batch, seq):
    m = proj.shape[0]
    nb = seq // SWA_BLOCK
    w = BRANCH_WIDTH
    qblk, zblk = _proj_block("b_q", w), _proj_block("b_z", w)
    kblk, vblk = _proj_block("b_k", SWA_KV_WIDTH), _proj_block("b_v", SWA_KV_WIDTH)

    def cur(b, n):
        return b * nb + n

    def prev(b, n):
        return b * nb + jnp.maximum(n - 1, 0)

    return pl.pallas_call(
        functools.partial(_swa_body, layer),
        grid=(batch, nb),
        in_specs=[
            pl.BlockSpec((SWA_BLOCK, w), lambda b, n: (cur(b, n), qblk)),
            pl.BlockSpec((SWA_BLOCK, SWA_KV_WIDTH), lambda b, n: (cur(b, n), kblk)),
            pl.BlockSpec((SWA_BLOCK, SWA_KV_WIDTH), lambda b, n: (prev(b, n), kblk)),
            pl.BlockSpec((SWA_BLOCK, SWA_KV_WIDTH), lambda b, n: (cur(b, n), vblk)),
            pl.BlockSpec((SWA_BLOCK, SWA_KV_WIDTH), lambda b, n: (prev(b, n), vblk)),
            pl.BlockSpec((SWA_BLOCK, w), lambda b, n: (cur(b, n), zblk)),
            pl.BlockSpec((3, SWA_BLOCK, LANES), lambda b, n: (0, cur(b, n), 0)),
            pl.BlockSpec((3, SWA_BLOCK, LANES), lambda b, n: (0, prev(b, n), 0)),
            pl.BlockSpec(memory_space=pltpu.SMEM),
        ],
        out_specs=pl.BlockSpec((SWA_BLOCK, w), lambda b, n: (cur(b, n), 0)),
        out_shape=jax.ShapeDtypeStruct((m, w), BF16),
        compiler_params=_params("arbitrary", "arbitrary"),
        name="swa",
    )(proj, proj, proj, proj, proj, proj, rope_tab, rope_tab, sinks)


def _mlstm_body(chunks, nb, q_ref, k_ref, v_ref, o_ref, z_ref, gc_ref, gt_ref, ng_ref, y_ref,
                c_ref, n_ref, m_ref):
    L, DK, DV = MLSTM_CHUNK, MLSTM_QK_DIM, MLSTM_V_DIM

    @pl.when(pl.program_id(1) == 0)
    def _():
        c_ref[...] = jnp.zeros_like(c_ref)
        n_ref[...] = jnp.zeros_like(n_ref)
        m_ref[...] = jnp.zeros_like(m_ref)

    r = lax.broadcasted_iota(jnp.int32, (L, L), 0)
    c = lax.broadcasted_iota(jnp.int32, (L, L), 1)
    causal = r >= c
    lane_g = lax.broadcasted_iota(jnp.int32, (L, LANES), 1)
    row_g = lax.broadcasted_iota(jnp.int32, (GATE_ROWS, L), 0)
    chains = [(bi, h) for bi in range(nb) for h in range(MLSTM_HEADS)]

    for ch in range(chunks):
        rows = pl.ds(ch * L, L)
        gates, cums, stab, qkv, inter, houts = [], [], [], [], [], []
        for bi, h in chains:
            gc = gc_ref[bi, rows, :]
            gt = gt_ref[bi, :, ch * L:(ch + 1) * L]
            i_col = jnp.sum(jnp.where(lane_g == h, gc, 0.0), axis=1, keepdims=True)
            f_col = jnp.sum(jnp.where(lane_g == h + MLSTM_HEADS, gc, 0.0), axis=1, keepdims=True)
            i_row = jnp.sum(jnp.where(row_g == h, gt, 0.0), axis=0, keepdims=True)
            f_row = jnp.sum(jnp.where(row_g == h + MLSTM_HEADS, gt, 0.0), axis=0, keepdims=True)
            gates.append((i_col, f_col, i_row, f_row))
        for (bi, h), (i_col, f_col, i_row, f_row) in zip(chains, gates):
            b_col = jnp.sum(jnp.where(causal, f_row, 0.0), axis=1, keepdims=True)
            b_row = jnp.sum(jnp.where(r <= c, f_col, 0.0), axis=0, keepdims=True)
            g_tot = jnp.sum(f_row, axis=1, keepdims=True)
            cums.append((b_col, b_row, g_tot, m_ref[bi * MLSTM_HEADS + h]))
        for (i_col, f_col, i_row, f_row), (b_col, b_row, g_tot, m_prev) in zip(gates, cums):
            log_d = jnp.where(causal, b_col - b_row + i_row, NEG_INF)
            m_inter = b_col + m_prev
            m_t = jnp.maximum(m_inter, jnp.max(log_d, axis=1, keepdims=True))
            stab.append((m_t, jnp.exp(log_d - m_t), jnp.exp(m_inter - m_t)))
        for bi, h in chains:
            qf = q_ref[bi, rows, h * DK:(h + 1) * DK].astype(F32) * (DK ** -0.5)
            qs = qf.astype(BF16)
            k = k_ref[bi, rows, h * DK:(h + 1) * DK]
            v = v_ref[bi, rows, h * DV:(h + 1) * DV]
            qk = lax.dot_general(qs, k, _NT, preferred_element_type=F32)
            qkv.append((qf, qs, k, v, qk))
        for (bi, h), (qf, qs, k, v, qk), (m_t, dmat, a) in zip(chains, qkv, stab):
            c_prev = c_ref[bi * MLSTM_HEADS + h]
            inter.append((qk * dmat, c_prev,
                          jnp.dot(qs, c_prev.astype(BF16), preferred_element_type=F32)))
        for (bi, h), (qf, qs, k, v, qk), (m_t, dmat, a), (s, c_prev, qc) in zip(
                chains, qkv, stab, inter):
            num = jnp.dot(s.astype(BF16), v, preferred_element_type=F32) + a * qc
            qn = jnp.sum(qf * n_ref[bi * MLSTM_HEADS + h], axis=1, keepdims=True)
            den = jnp.sum(s, axis=1, keepdims=True) + a * qn
            houts.append(num / jnp.maximum(jnp.abs(den), jnp.exp(-m_t)))
        for (bi, h), (i_col, f_col, i_row, f_row), (b_col, b_row, g_tot, m_prev), \
                (qf, qs, k, v, qk), (s, c_prev, qc) in zip(chains, gates, cums, qkv, inter):
            idx = bi * MLSTM_HEADS + h
            w_col = g_tot - b_col + i_col
            w_row = g_tot - b_row + i_row
            m_new = jnp.maximum(g_tot + m_prev, jnp.max(w_row, axis=1, keepdims=True))
            wgt = jnp.exp(w_col - m_new)
            decay = jnp.exp(g_tot + m_prev - m_new)
            kw = k.astype(F32) * wgt
            c_ref[idx] = decay * c_prev + lax.dot_general(
                kw.astype(BF16), v, (((0,), (0,)), ((), ())), preferred_element_type=F32)
            n_ref[idx] = decay * n_ref[idx] + jnp.sum(kw, axis=0, keepdims=True)
            m_ref[idx] = m_new
        for (bi, h), hout in zip(chains, houts):
            vcols = pl.ds(h * DV, DV)
            hn = hout * lax.rsqrt(jnp.mean(hout * hout, axis=1, keepdims=True) + NORM_EPS)
            hn = hn * ng_ref[:, vcols]
            y = hn * o_ref[bi, rows, vcols].astype(F32) * z_ref[bi, rows, vcols].astype(F32)
            y_ref[bi, rows, vcols] = y.astype(y_ref.dtype)


def _mlstm(proj, gcol, gtr, norm_g, layer, batch, seq, ts=MLSTM_CHUNK, nb=2):
    m = proj.shape[0]
    nt = seq // ts
    qw, vw = MLSTM_HEADS * MLSTM_QK_DIM, BRANCH_WIDTH
    qblk, kblk = _proj_block("c_q", qw), _proj_block("c_k", qw)
    vblk, oblk, zblk = (_proj_block(n, vw) for n in ("c_v", "c_o", "c_z"))
    proj3 = proj.reshape(batch, seq, proj.shape[-1])
    assert batch % nb == 0

    def tok3(a):
        return a.reshape(batch, seq, a.shape[-1])

    chains = nb * MLSTM_HEADS
    out = pl.pallas_call(
        functools.partial(_mlstm_body, ts // MLSTM_CHUNK, nb),
        grid=(batch // nb, nt),
        in_specs=[
            pl.BlockSpec((nb, ts, qw), lambda b, t: (b, t, qblk)),
            pl.BlockSpec((nb, ts, qw), lambda b, t: (b, t, kblk)),
            pl.BlockSpec((nb, ts, vw), lambda b, t: (b, t, vblk)),
            pl.BlockSpec((nb, ts, vw), lambda b, t: (b, t, oblk)),
            pl.BlockSpec((nb, ts, vw), lambda b, t: (b, t, zblk)),
            pl.BlockSpec((nb, ts, LANES), lambda b, t: (b, t, 0)),
            pl.BlockSpec((nb, GATE_ROWS, ts), lambda b, t: (b, 0, t)),
            pl.BlockSpec((None, 1, vw), lambda b, t: (layer, 0, 0)),
        ],
        out_specs=pl.BlockSpec((nb, ts, vw), lambda b, t: (b, t, 0)),
        out_shape=jax.ShapeDtypeStruct((batch, seq, BRANCH_WIDTH), BF16),
        scratch_shapes=[
            pltpu.VMEM((chains, MLSTM_QK_DIM, MLSTM_V_DIM), F32),
            pltpu.VMEM((chains, 1, MLSTM_QK_DIM), F32),
            pltpu.VMEM((chains, 1, 1), F32),
        ],
        compiler_params=_params("arbitrary", "arbitrary"),
        name="mlstm",
    )(proj3, proj3, proj3, proj3, proj3, tok3(gcol), gtr, norm_g)
    return out.reshape(m, BRANCH_WIDTH)


def _merge_body(ya_ref, yb_ref, yc_ref, wp_ref, g0_ref, g1_ref, g2_ref, o_ref, wb_ref):
    piece = wp_ref.shape[1]
    rows = pl.ds(pl.multiple_of(pl.program_id(1) * piece, piece), piece)

    def cast(dst):
        dst[:, rows, :] = wp_ref[...].astype(BF16)

    def compute(src):
        acc = g0_ref[...].astype(F32) * jnp.dot(ya_ref[...], src[0], preferred_element_type=F32)
        acc = acc + g1_ref[...].astype(F32) * jnp.dot(yb_ref[...], src[1],
                                                      preferred_element_type=F32)
        acc = acc + g2_ref[...].astype(F32) * jnp.dot(yc_ref[...], src[2],
                                                      preferred_element_type=F32)
        o_ref[...] = acc.astype(o_ref.dtype)

    _stream_step(wb_ref, cast, compute)


def _merge(ya, yb, yc, w_branch, proj, layer, tm=1024, tn=512):
    m = ya.shape[0]
    w = BRANCH_WIDTH
    n_col, n_row = D_MODEL // tn, m // tm
    piece = w // n_row
    gate0 = PROJ_COL["gates"] // tn
    assert PROJ_COL["gates"] % tn == 0
    yspec = pl.BlockSpec((tm, w), lambda jj, i: (_row_of(jj, i), 0))

    def gate_spec(branch):
        return pl.BlockSpec(
            (tm, tn), lambda jj, i: (_row_of(jj, i), gate0 + branch * n_col + _col_of(jj)))

    return pl.pallas_call(
        _merge_body,
        grid=(n_col + 1, n_row),
        in_specs=[
            yspec, yspec, yspec,
            pl.BlockSpec((None, N_BRANCHES, piece, tn),
                         lambda jj, i: (layer, 0, i, _cast_col_of(jj, n_col))),
            gate_spec(0), gate_spec(1), gate_spec(2),
        ],
        out_specs=pl.BlockSpec((tm, tn), lambda jj, i: (_row_of(jj, i), _col_of(jj))),
        out_shape=jax.ShapeDtypeStruct((m, D_MODEL), BF16),
        scratch_shapes=[pltpu.VMEM((2, N_BRANCHES, w, tn), BF16)],
        compiler_params=_params("arbitrary", "arbitrary"),
        name="merge",
    )(ya, yb, yc, w_branch, proj, proj, proj)


def _outproj_body(a_ref, wp_ref, t_ref, ssq_ref, wb_ref):
    jj = pl.program_id(0)
    i = pl.program_id(1)
    piece = wp_ref.shape[0]
    rows = pl.ds(pl.multiple_of(i * piece, piece), piece)

    def cast(dst):
        dst[rows, :] = wp_ref[...].astype(BF16)

    def compute(src):
        acc = jnp.dot(a_ref[...], src[...], preferred_element_type=F32)
        t_ref[...] = acc.astype(t_ref.dtype)
        out_rows = pl.ds(pl.multiple_of(i * TM, TM), TM)
        part = jnp.broadcast_to(jnp.sum(acc * acc, axis=-1, keepdims=True), (TM, LANES))

        @pl.when(jj == 1)
        def _():
            ssq_ref[out_rows, :] = part

        @pl.when(jj > 1)
        def _():
            ssq_ref[out_rows, :] += part

    _stream_step(wb_ref, cast, compute)


def _outproj(a, w_out, layer, tn=1024):
    m, k = a.shape
    n = w_out.shape[-1]
    n_col, n_row = n // tn, m // TM
    piece = k // n_row
    return pl.pallas_call(
        _outproj_body,
        grid=(n_col + 1, n_row),
        in_specs=[
            pl.BlockSpec((TM, k), lambda jj, i: (_row_of(jj, i), 0)),
            pl.BlockSpec((None, piece, tn), lambda jj, i: (layer, i, _cast_col_of(jj, n_col))),
        ],
        out_specs=[
            pl.BlockSpec((TM, tn), lambda jj, i: (_row_of(jj, i), _col_of(jj))),
            pl.BlockSpec((m, LANES), lambda jj, i: (0, 0)),
        ],
        out_shape=[
            jax.ShapeDtypeStruct((m, n), BF16),
            jax.ShapeDtypeStruct((m, LANES), F32),
        ],
        scratch_shapes=[pltpu.VMEM((2, k, tn), BF16)],
        compiler_params=_params("arbitrary", "arbitrary"),
        name="outproj",
    )(a, w_out)


def _post_residual(x, t, ssq, g):
    rs = lax.rsqrt(ssq * (1.0 / D_MODEL) + NORM_EPS)
    return x + t.astype(F32) * rs * g


def _postnorm_body(x_ref, t_ref, ssq_ref, g_ref, x1b_ref):
    x1b_ref[...] = _post_residual(x_ref[...], t_ref[...], ssq_ref[:, 0:1], g_ref[...]).astype(BF16)


def _postnorm(x2d, t, ssq, norm_post, layer, tm=TM_ROWWISE):
    m = x2d.shape[0]
    return pl.pallas_call(
        _postnorm_body,
        grid=(m // tm,),
        in_specs=[
            pl.BlockSpec((tm, D_MODEL), lambda i: (i, 0)),
            pl.BlockSpec((tm, D_MODEL), lambda i: (i, 0)),
            pl.BlockSpec((tm, LANES), lambda i: (i, 0)),
            pl.BlockSpec((None, 1, D_MODEL), lambda i: (layer, 0, 0)),
        ],
        out_specs=pl.BlockSpec((tm, D_MODEL), lambda i: (i, 0)),
        out_shape=jax.ShapeDtypeStruct((m, D_MODEL), BF16),
        compiler_params=_params("arbitrary"),
        name="postnorm",
    )(x2d, t, ssq, norm_post)


def _ple_body(p_ref, w_ref, g_ref, e_ref, wb_ref):
    @pl.when(pl.program_id(0) == 0)
    def _():
        wb_ref[...] = w_ref[...].astype(BF16)

    t = jnp.dot(p_ref[...].astype(BF16), wb_ref[...], preferred_element_type=F32)
    e = t * lax.rsqrt(jnp.mean(t * t, axis=-1, keepdims=True) + NORM_EPS) * g_ref[...]
    e_ref[...] = e.astype(e_ref.dtype)


def _ple(p3d, ple_proj, ple_norm, layer, tm=TM_ROWWISE):
    m = p3d.shape[1]
    return pl.pallas_call(
        _ple_body,
        grid=(m // tm,),
        in_specs=[
            pl.BlockSpec((None, tm, PLE_DIM), lambda i: (layer, i, 0)),
            pl.BlockSpec((None, PLE_DIM, D_MODEL), lambda i: (layer, 0, 0)),
            pl.BlockSpec((None, 1, D_MODEL), lambda i: (layer, 0, 0)),
        ],
        out_specs=pl.BlockSpec((tm, D_MODEL), lambda i: (i, 0)),
        out_shape=jax.ShapeDtypeStruct((m, D_MODEL), BF16),
        scratch_shapes=[pltpu.VMEM((PLE_DIM, D_MODEL), BF16)],
        compiler_params=_params("arbitrary"),
        name="ple_embed",
    )(p3d, ple_proj, ple_norm)


def _plegate_body(cast_every, a_ref, wp_ref, x_ref, t_ref, ssq_ref, g_ref, e_ref, o_ref, wb_ref):
    piece = wp_ref.shape[0]
    rows = pl.ds(pl.multiple_of((pl.program_id(1) // cast_every) * piece, piece), piece)

    def cast(dst):
        dst[rows, :] = wp_ref[...].astype(BF16)

    def compute(src):
        acc = jnp.dot(a_ref[...], src[...], preferred_element_type=F32)
        x1 = _post_residual(x_ref[...], t_ref[...], ssq_ref[:, 0:1], g_ref[...])
        o_ref[...] = x1 + _sigmoid(acc) * e_ref[...].astype(F32)

    _stream_step(wb_ref, cast, compute, cast_every)


def _plegate(x1b, ple_gate, x2d, t, ssq, norm_post, e, layer, tm=512, tn=1024):
    m, k = x1b.shape
    n = ple_gate.shape[-1]
    n_col, n_row = n // tn, m // tm
    cast_every = 2
    piece = k * cast_every // n_row
    assert n_row % cast_every == 0
    tile = pl.BlockSpec((tm, tn), lambda jj, i: (_row_of(jj, i), _col_of(jj)))
    return pl.pallas_call(
        functools.partial(_plegate_body, cast_every),
        grid=(n_col + 1, n_row),
        in_specs=[
            pl.BlockSpec((tm, k), lambda jj, i: (_row_of(jj, i), 0)),
            pl.BlockSpec((None, piece, tn),
                         lambda jj, i: (layer, i // cast_every, _cast_col_of(jj, n_col))),
            tile, tile,
            pl.BlockSpec((tm, LANES), lambda jj, i: (_row_of(jj, i), 0)),
            pl.BlockSpec((None, 1, tn), lambda jj, i: (layer, 0, _col_of(jj))),
            tile,
        ],
        out_specs=tile,
        out_shape=jax.ShapeDtypeStruct((m, n), F32),
        scratch_shapes=[pltpu.VMEM((2, k, tn), BF16)],
        compiler_params=_params("arbitrary", "arbitrary"),
        name="ple_gate",
    )(x1b, ple_gate, x2d, t, ssq, norm_post, e)


def kernel(x, p, positions, norm_pre, w_in, gmlp_ln_g, gmlp_ln_b, gmlp_ws, gmlp_bs, attn_sinks,
           mlstm_ib, mlstm_fb, mlstm_norm_g, w_branch, w_out, norm_post, ple_proj, ple_norm,
           ple_gate):
    batch, seq, d = x.shape
    depth = w_in.shape[0]
    m = batch * seq
    assert d == D_MODEL and w_in.shape[-1] == N_IN and seq % 512 == 0 and m % TM == 0

    inv_freq = ROPE_THETA ** (-jnp.arange(0, ROT_DIM, 2, dtype=F32) / ROT_DIM)
    lane_d = jnp.arange(LANES) % SWA_HEAD_DIM
    inv_lane = jnp.where(lane_d < ROT_DIM, inv_freq[lane_d % (ROT_DIM // 2)], 0.0).reshape(1, LANES)
    posf = positions.astype(F32).reshape(m, 1)
    p3d = p.reshape(depth, m, PLE_DIM)
    bst = jnp.swapaxes(gmlp_bs, 1, 2)
    gate_bias = jnp.pad(jnp.concatenate([mlstm_ib, mlstm_fb], axis=1).astype(F32),
                        ((0, 0), (0, LANES - GATE_COLS)))

    def rows3(a):
        return a.reshape(depth, 1, a.shape[-1])

    norm_pre, norm_post, ple_norm = rows3(norm_pre), rows3(norm_post), rows3(ple_norm)
    gmlp_ln_g, gmlp_ln_b = rows3(gmlp_ln_g), rows3(gmlp_ln_b)
    mlstm_norm_g = rows3(mlstm_norm_g)
    rope_tab = _rope_call(posf, inv_lane)

    w_t = jnp.swapaxes(w_in, 1, 2)

    xc = x.reshape(m, d)
    for l in range(depth):
        h, gcol, gtr = _prenorm(xc, norm_pre, w_t, gate_bias[l:l + 1], l, batch, seq)
        proj = _proj(h, w_t, l)

        ya = _gmlp(proj, gmlp_ln_g, gmlp_ln_b, gmlp_ws, bst, l)
        yb = _swa(proj, rope_tab, attn_sinks, l, batch, seq)
        yc = _mlstm(proj, gcol, gtr, mlstm_norm_g, l, batch, seq)

        mixed = _merge(ya, yb, yc, w_branch, proj, l)
        t, ssq = _outproj(mixed, w_out, l)
        x1b = _postnorm(xc, t, ssq, norm_post, l)
        e = _ple(p3d, ple_proj, ple_norm, l)
        xc = _plegate(x1b, ple_gate, xc, t, ssq, norm_post, e, l)
    return xc.reshape(batch, seq, d)
```
